```python
import math
import jax, jax.numpy as jnp
from jax import lax
import numpy as np

D_MODEL = 4096
BATCH = 4
SEQ = 2048
DEPTH = 4
DEC_BATCH = 8
DEC_SEQ = 1
PAST_LEN = 8192
PAGE_SIZE = 128

N_A = DEPTH // 2
N_B = DEPTH - N_A
MLSTM_HEADS = 8
MLSTM_QK_DIM = D_MODEL // 16
MLSTM_V_DIM = D_MODEL // MLSTM_HEADS
MLSTM_CHUNK = 64
GATE_CAP = 15.0
HEAD_DIM = 128
NSA_HEADS = D_MODEL // HEAD_DIM
NSA_KV_HEADS = 4
GRP = NSA_HEADS // NSA_KV_HEADS
CMP_LEN = 32
CMP_STRIDE = 16
CMP_HID = HEAD_DIM
SEL_LEN = 64
SEL_TOP = 16
WINDOW = 512
Q_BLOCK = 128
D_FF = -(-8 * D_MODEL // (3 * 256)) * 256
EPS = 1e-6
NEG = -1e30

kernel_name = 'yoco_mlstm_nsa_decoder'


def rmsnorm(x, g):
    x32 = x.astype(jnp.float32)
    y = x32 * lax.rsqrt(jnp.mean(x32 * x32, axis=-1, keepdims=True) + EPS)
    return (y * g.astype(jnp.float32)).astype(x.dtype)


def swiglu(h, w_gu, w_dn):
    g, u = jnp.split(h @ w_gu, 2, axis=-1)
    return (jax.nn.silu(g) * u) @ w_dn


def masked_softmax(s, valid):
    p = jax.nn.softmax(jnp.where(valid, s, NEG), axis=-1)
    return p * valid


def alibi_slopes():
    return jnp.exp2(-8.0 * jnp.arange(1, NSA_HEADS + 1, dtype=jnp.float32) / NSA_HEADS)


def mlstm_chunk(carry, inp):
    C, n, m = carry
    q, k, v, ig, lf = inp
    L = q.shape[2]
    b = jnp.cumsum(lf, axis=-1)
    causal = jnp.tril(jnp.ones((L, L), dtype=bool))
    log_d = jnp.where(causal, b[..., :, None] - b[..., None, :] + ig[..., None, :], -jnp.inf)
    inter = b + m[..., None]
    m_t = jnp.maximum(jnp.max(log_d, axis=-1), inter)
    d = jnp.exp(log_d - m_t[..., None])
    w_inter = jnp.exp(inter - m_t)
    s = jnp.einsum('bhtd,bhsd->bhts', q, k) * d
    num = w_inter[..., None] * jnp.einsum('bhtd,bhdv->bhtv', q, C) + jnp.einsum('bhts,bhsv->bhtv', s, v)
    den = w_inter * jnp.einsum('bhtd,bhd->bht', q, n) + jnp.sum(s, axis=-1)
    h = num / jnp.maximum(jnp.abs(den), jnp.exp(-m_t))[..., None]
    m_new = m_t[..., -1]
    decay = jnp.exp(b[..., -1:] - b + ig - m_new[..., None])
    carry_w = jnp.exp(b[..., -1] + m - m_new)
    C_new = carry_w[..., None, None] * C + jnp.einsum('bhs,bhsd,bhsv->bhdv', decay, k, v)
    n_new = carry_w[..., None] * n + jnp.einsum('bhs,bhsd->bhd', decay, k)
    return (C_new, n_new, m_new), h


def mlstm_mixer(h, C0, n0, m0, w_in, b_gate, g_head, w_out):
    B, T, _ = h.shape
    f32 = jnp.float32
    qk = MLSTM_HEADS * MLSTM_QK_DIM
    vd = MLSTM_HEADS * MLSTM_V_DIM
    z = h @ w_in

    def heads(a, dh):
        return a.reshape(B, T, MLSTM_HEADS, dh).transpose(0, 2, 1, 3).astype(f32)

    q = heads(z[..., :qk], MLSTM_QK_DIM) * (MLSTM_QK_DIM ** -0.5)
    k = heads(z[..., qk:2 * qk], MLSTM_QK_DIM)
    v = heads(z[..., 2 * qk:2 * qk + vd], MLSTM_V_DIM)
    o = jax.nn.sigmoid(z[..., 2 * qk + vd:2 * qk + 2 * vd].astype(f32))
    gl = (z[..., 2 * qk + 2 * vd:] + b_gate).astype(f32)
    gl = GATE_CAP * jnp.tanh(gl / GATE_CAP)
    ig = gl[..., :MLSTM_HEADS].transpose(0, 2, 1)
    lf = jax.nn.log_sigmoid(gl[..., MLSTM_HEADS:]).transpose(0, 2, 1)
    chunk = MLSTM_CHUNK if T % MLSTM_CHUNK == 0 else T
    nc = T // chunk

    def split(a):
        return jnp.moveaxis(a.reshape(a.shape[:2] + (nc, chunk) + a.shape[3:]), 2, 0)

    (C, n, m), hs = lax.scan(mlstm_chunk, (C0.astype(f32), n0.astype(f32), m0.astype(f32)),
                             (split(q), split(k), split(v), split(ig), split(lf)))
    hs = jnp.moveaxis(hs, 0, 2).reshape(B, MLSTM_HEADS, T, MLSTM_V_DIM).transpose(0, 2, 1, 3)
    hs = hs * lax.rsqrt(jnp.mean(hs * hs, axis=-1, keepdims=True) + EPS)
    hs = hs.reshape(B, T, vd) * g_head.astype(f32) * o
    y = hs.astype(h.dtype) @ w_out
    return y, (C.astype(C0.dtype), n.astype(n0.dtype), m.astype(m0.dtype))


def compress_blocks(rows, pe, w1, w2):
    B, L, G, D = rows.shape
    R = CMP_LEN // CMP_STRIDE
    n_chunks = L // CMP_STRIDE
    n_cmp = n_chunks - R + 1
    chunks = rows[:, :n_chunks * CMP_STRIDE].reshape(B, n_chunks, CMP_STRIDE, G, D)
    w1r = w1.reshape(R, CMP_STRIDE, D, CMP_HID)
    proj = jnp.einsum('bncgd,rcde->rbnge', chunks, w1r)
    pe_term = jnp.einsum('rcd,rcde->e', pe.reshape(R, CMP_STRIDE, D), w1r)
    hidden = sum(proj[r, :, r:r + n_cmp] for r in range(R)) + pe_term
    return jax.nn.gelu(hidden) @ w2


def to_sel_blocks(r):
    B, L = r.shape[:2]
    n_sel = -(-L // SEL_LEN)
    r = jnp.pad(r, ((0, 0), (0, n_sel * SEL_LEN - L), (0, 0), (0, 0)))
    return r.reshape(B, n_sel, SEL_LEN, r.shape[2], r.shape[3])


def selection_overlap(n_cmp, n_sel):
    start = np.arange(n_cmp)[:, None] * CMP_STRIDE
    end = start + CMP_LEN - 1
    j = np.arange(n_sel)[None, :]
    ov = (start <= (j + 1) * SEL_LEN - 1) & (end >= j * SEL_LEN)
    return jnp.asarray(ov.astype(np.float32))


def nsa_attend(q, gate, tpos, kc, vc, cpos, ks, vs, kw, vw, wpos, ov, slopes):
    T, H, D = q.shape
    f32 = jnp.float32
    scale = D ** -0.5
    qg = q.reshape(T, NSA_KV_HEADS, GRP, D)
    sl = slopes.reshape(NSA_KV_HEADS, GRP)[:, :, None, None]
    dist_c = tpos[:, None] - cpos[None, :]
    s_c = jnp.einsum('tgrd,ngd->grtn', qg, kc).astype(f32) * scale - sl * dist_c.astype(f32)
    p_c = masked_softmax(s_c, dist_c >= 0)
    o_c = jnp.einsum('grtn,ngd->tgrd', p_c, vc.astype(f32))
    n_sel = ks.shape[0]
    imp = jnp.einsum('grtn,nj->tgj', p_c, ov)
    j = jnp.arange(n_sel, dtype=jnp.int32)[None, :]
    cur = (tpos // SEL_LEN)[:, None]
    forced = (j == 0) | (j == cur) | (j == cur - 1)
    imp = jnp.where(forced[:, None, :], jnp.inf, jnp.where((j <= cur)[:, None, :], imp, -jnp.inf))
    kk = min(SEL_TOP, n_sel)
    _, idx = lax.top_k(imp, kk)
    gather = jax.vmap(lambda blk, ix: blk[ix], in_axes=(0, 1), out_axes=1)
    ksel = gather(ks.transpose(2, 0, 1, 3), idx)
    vsel = gather(vs.transpose(2, 0, 1, 3), idx)
    spos = idx[..., None] * SEL_LEN + jnp.arange(SEL_LEN, dtype=jnp.int32)
    ds = (tpos[:, None, None, None] - spos).transpose(1, 0, 2, 3)[:, None]
    s_s = jnp.einsum('tgrd,tgkld->grtkl', qg, ksel).astype(f32) * scale - sl[..., None] * ds.astype(f32)
    p_s = masked_softmax(s_s.reshape(NSA_KV_HEADS, GRP, T, kk * SEL_LEN),
                         (ds >= 0).reshape(NSA_KV_HEADS, 1, T, kk * SEL_LEN))
    o_s = jnp.einsum('grtkl,tgkld->tgrd', p_s.reshape(NSA_KV_HEADS, GRP, T, kk, SEL_LEN), vsel.astype(f32))
    dist_w = tpos[:, None] - wpos[None, :]
    valid_w = (dist_w >= 0) & (dist_w <= WINDOW) & (wpos[None, :] >= 0)
    s_w = jnp.einsum('tgrd,wgd->grtw', qg, kw).astype(f32) * scale - sl * dist_w.astype(f32)
    p_w = masked_softmax(s_w, valid_w)
    o_w = jnp.einsum('grtw,wgd->tgrd', p_w, vw.astype(f32))
    g = gate.reshape(T, NSA_KV_HEADS, GRP, 3).astype(f32)
    out = g[..., 0:1] * o_c + g[..., 1:2] * o_s + g[..., 2:3] * o_w
    return out.reshape(T, H, D).astype(q.dtype)


def nsa_branch_keys(full_rows, pe_cmp, w_cmp1, w_cmp2):
    kc = compress_blocks(full_rows[:, :, 0], pe_cmp[0], w_cmp1[0], w_cmp2[0])
    vc = compress_blocks(full_rows[:, :, 1], pe_cmp[1], w_cmp1[1], w_cmp2[1])
    n_cmp = kc.shape[1]
    cpos = jnp.arange(n_cmp, dtype=jnp.int32) * CMP_STRIDE + (CMP_LEN - 1)
    ks = to_sel_blocks(full_rows[:, :, 2])
    vs = to_sel_blocks(full_rows[:, :, 3])
    ov = selection_overlap(n_cmp, ks.shape[1])
    return kc, vc, cpos, ks, vs, ov


def prompt_ctx(rows, pe_cmp, w_cmp1, w_cmp2):
    kc, vc, cpos, ks, vs, ov = nsa_branch_keys(rows[:, :, :4], pe_cmp, w_cmp1, w_cmp2)
    pad = ((0, 0), (WINDOW, 0), (0, 0), (0, 0))
    kw_pad = jnp.pad(rows[:, :, 4], pad)
    vw_pad = jnp.pad(rows[:, :, 5], pad)
    return (kc, vc, cpos, ks, vs, kw_pad, vw_pad, ov)


def sample_ctx(rows, cache_nsa_kv, page_table, state_win_kv, pe_cmp, w_cmp1, w_cmp2):
    past = cache_nsa_kv[page_table]
    DB, NP, P = past.shape[:3]
    past = past.reshape((DB, NP * P) + past.shape[3:])
    full = jnp.concatenate([past, rows[:, :, :4]], axis=1)
    kc, vc, cpos, ks, vs, ov = nsa_branch_keys(full, pe_cmp, w_cmp1, w_cmp2)
    kw = jnp.concatenate([state_win_kv[:, :, 0], rows[:, :, 4]], axis=1)
    vw = jnp.concatenate([state_win_kv[:, :, 1], rows[:, :, 5]], axis=1)
    t0 = NP * P
    wbuf = state_win_kv.shape[1]
    wpos = t0 - wbuf + jnp.arange(wbuf + rows.shape[1], dtype=jnp.int32)
    return (kc, vc, cpos, ks, vs, kw, vw, wpos, ov, t0)


def nsa_prompt(q, gates, ctx, slopes):
    kc, vc, cpos, ks, vs, kw_pad, vw_pad, ov = ctx
    B, T, H, D = q.shape
    qblk = min(Q_BLOCK, T)
    nqb = T // qblk
    wl = WINDOW + qblk

    def body(item):
        qb, gb, b, qs = item
        tpos = qs + jnp.arange(qblk, dtype=jnp.int32)
        kw = lax.dynamic_slice_in_dim(kw_pad[b], qs, wl, axis=0)
        vw = lax.dynamic_slice_in_dim(vw_pad[b], qs, wl, axis=0)
        wpos = qs - WINDOW + jnp.arange(wl, dtype=jnp.int32)
        return nsa_attend(qb, gb, tpos, kc[b], vc[b], cpos, ks[b], vs[b], kw, vw, wpos, ov, slopes)

    qr = q.reshape(B * nqb, qblk, H, D)
    gr = gates.reshape(B * nqb, qblk, H, 3)
    bidx = jnp.repeat(jnp.arange(B, dtype=jnp.int32), nqb)
    qstart = jnp.tile(jnp.arange(nqb, dtype=jnp.int32) * qblk, B)
    out = lax.map(body, (qr, gr, bidx, qstart))
    return out.reshape(B, T, H * D)


def nsa_sample(q, gates, ctx, slopes):
    kc, vc, cpos, ks, vs, kw, vw, wpos, ov, t0 = ctx
    B, T, H, D = q.shape
    tpos = t0 + jnp.arange(T, dtype=jnp.int32)
    f = jax.vmap(nsa_attend, in_axes=(0, 0, None, 0, 0, None, 0, 0, 0, 0, None, None, None))
    return f(q, gates, tpos, kc, vc, cpos, ks, vs, kw, vw, wpos, ov, slopes).reshape(B, T, H * D)


def setup_inputs(seed: int = 0) -> dict:
    key = jax.random.key(seed)
    k = jax.random.split(key, 32)
    f32 = jnp.float32

    def nrm(kk, shape, scale):
        return jax.random.normal(kk, shape, f32) * scale

    n_pages = PAST_LEN // PAGE_SIZE
    n_pool = (5 * DEC_BATCH * n_pages + 3) // 4
    wbuf = min(WINDOW, PAST_LEN)
    qk = MLSTM_HEADS * MLSTM_QK_DIM
    vd = MLSTM_HEADS * MLSTM_V_DIM
    in_a = 2 * qk + 2 * vd + 2 * MLSTM_HEADS
    page_table = jax.random.permutation(k[3], n_pool)[:DEC_BATCH * n_pages].reshape(DEC_BATCH, n_pages).astype(jnp.int32)
    b_gate = jnp.concatenate([nrm(k[9], (N_A, MLSTM_HEADS), 0.1),
                              2.0 + nrm(k[10], (N_A, MLSTM_HEADS), 0.1)], axis=-1)
    return {
        'x_prompt': nrm(k[0], (BATCH, SEQ, D_MODEL), 1.0),
        'x_sample': nrm(k[1], (DEC_BATCH, DEC_SEQ, D_MODEL), 1.0),
        'cache_nsa_kv': nrm(k[2], (n_pool, PAGE_SIZE, 4, NSA_KV_HEADS, HEAD_DIM), 1.0),
        'page_table': page_table,
        'state_win_kv': nrm(k[4], (DEC_BATCH, wbuf, 2, NSA_KV_HEADS, HEAD_DIM), 1.0),
        'state_mlstm_C': nrm(k[5], (N_A, DEC_BATCH, MLSTM_HEADS, MLSTM_QK_DIM, MLSTM_V_DIM), 0.05),
        'state_mlstm_n': nrm(k[6], (N_A, DEC_BATCH, MLSTM_HEADS, MLSTM_QK_DIM), 0.05),
        'state_mlstm_m': nrm(k[7], (N_A, DEC_BATCH, MLSTM_HEADS), 1.0),
        'g_mix': 1.0 + nrm(k[8], (DEPTH, D_MODEL), 0.02),
        'g_ffn': 1.0 + nrm(k[11], (DEPTH, D_MODEL), 0.02),
        'g_kv': 1.0 + nrm(k[12], (D_MODEL,), 0.02),
        'g_final': 1.0 + nrm(k[13], (D_MODEL,), 0.02),
        'w_in_a': nrm(k[14], (N_A, D_MODEL, in_a), D_MODEL ** -0.5),
        'b_gate_a': b_gate,
        'g_head_a': 1.0 + nrm(k[15], (N_A, vd), 0.02),
        'w_out_a': nrm(k[16], (N_A, vd, D_MODEL), vd ** -0.5),
        'w_qg_b': nrm(k[17], (N_B, D_MODEL, NSA_HEADS * HEAD_DIM + 3 * NSA_HEADS), D_MODEL ** -0.5),
        'w_out_b': nrm(k[18], (N_B, NSA_HEADS * HEAD_DIM, D_MODEL), (NSA_HEADS * HEAD_DIM) ** -0.5),
        'w_kv': nrm(k[19], (D_MODEL, 6 * NSA_KV_HEADS * HEAD_DIM), D_MODEL ** -0.5),
        'pe_cmp': nrm(k[20], (2, CMP_LEN, HEAD_DIM), 0.1),
        'w_cmp1': nrm(k[21], (2, CMP_LEN * HEAD_DIM, CMP_HID), (CMP_LEN * HEAD_DIM) ** -0.5),
        'w_cmp2': nrm(k[22], (2, CMP_HID, HEAD_DIM), CMP_HID ** -0.5),
        'w_gate_up': nrm(k[23], (DEPTH, D_MODEL, 2 * D_FF), D_MODEL ** -0.5),
        'w_down': nrm(k[24], (DEPTH, D_FF, D_MODEL), D_FF ** -0.5),
    }


def reference(x_prompt, x_sample, cache_nsa_kv, page_table, state_win_kv, state_mlstm_C, state_mlstm_n, state_mlstm_m,
              g_mix, g_ffn, g_kv, g_final, w_in_a, b_gate_a, g_head_a, w_out_a, w_qg_b, w_out_b, w_kv,
              pe_cmp, w_cmp1, w_cmp2, w_gate_up, w_down):
    slopes = alibi_slopes()
    hd = NSA_HEADS * HEAD_DIM

    def trunk(x, C0, n0, m0, make_ctx, attend):
        B, T, _ = x.shape
        Cs, ns, ms = [], [], []
        rows, ctx = None, None
        for l in range(DEPTH):
            h = rmsnorm(x, g_mix[l])
            if l < N_A:
                y, (C, n, m) = mlstm_mixer(h, C0[l], n0[l], m0[l], w_in_a[l], b_gate_a[l], g_head_a[l], w_out_a[l])
                Cs.append(C)
                ns.append(n)
                ms.append(m)
            else:
                if l == N_A:
                    rows = (rmsnorm(x, g_kv) @ w_kv).reshape(B, T, 6, NSA_KV_HEADS, HEAD_DIM)
                    ctx = make_ctx(rows)
                zq = h @ w_qg_b[l - N_A]
                q = zq[..., :hd].reshape(B, T, NSA_HEADS, HEAD_DIM)
                gates = jax.nn.sigmoid(zq[..., hd:].astype(jnp.float32)).reshape(B, T, NSA_HEADS, 3)
                y = attend(q, gates, ctx, slopes) @ w_out_b[l - N_A]
            x = x + y
            x = x + swiglu(rmsnorm(x, g_ffn[l]), w_gate_up[l], w_down[l])
        return rmsnorm(x, g_final), rows, jnp.stack(Cs), jnp.stack(ns), jnp.stack(ms)

    Bp, Tp, _ = x_prompt.shape
    zC = jnp.zeros((N_A, Bp, MLSTM_HEADS, MLSTM_QK_DIM, MLSTM_V_DIM), x_prompt.dtype)
    zn = jnp.zeros((N_A, Bp, MLSTM_HEADS, MLSTM_QK_DIM), x_prompt.dtype)
    zm = jnp.zeros((N_A, Bp, MLSTM_HEADS), x_prompt.dtype)
    y_prompt, rows_p, C_p, n_p, m_p = trunk(
        x_prompt, zC, zn, zm,
        lambda r: prompt_ctx(r, pe_cmp, w_cmp1, w_cmp2), nsa_prompt)
    y_sample, rows_s, C_s, n_s, m_s = trunk(
        x_sample, state_mlstm_C, state_mlstm_n, state_mlstm_m,
        lambda r: sample_ctx(r, cache_nsa_kv, page_table, state_win_kv, pe_cmp, w_cmp1, w_cmp2), nsa_sample)

    wp = min(WINDOW, Tp)
    kv_rows_prompt = rows_p[:, :, :4]
    win_kv_prompt = rows_p[:, Tp - wp:, 4:]
    kv_rows_sample = rows_s[:, :, :4]
    wbuf = state_win_kv.shape[1]
    win_kv_sample = jnp.concatenate([state_win_kv, rows_s[:, :, 4:]], axis=1)[:, -wbuf:]
    return (y_prompt, y_sample, kv_rows_prompt, win_kv_prompt, C_p, n_p, m_p,
            kv_rows_sample, win_kv_sample, C_s, n_s, m_s)
```

```python
import functools

import numpy as np
import jax
import jax.numpy as jnp
from jax import lax
from jax.experimental import pallas as pl
from jax.experimental.pallas import tpu as pltpu

D_MODEL = 4096
BATCH = 4
SEQ = 2048
DEPTH = 4
DEC_BATCH = 8
DEC_SEQ = 1
PAST_LEN = 8192
PAGE_SIZE = 128

N_A = DEPTH // 2
N_B = DEPTH - N_A
MLSTM_HEADS = 8
MLSTM_QK_DIM = D_MODEL // 16
MLSTM_V_DIM = D_MODEL // MLSTM_HEADS
GATE_CAP = 15.0
HEAD_DIM = 128
NSA_HEADS = D_MODEL // HEAD_DIM
NSA_KV_HEADS = 4
GRP = NSA_HEADS // NSA_KV_HEADS
CMP_LEN = 32
CMP_STRIDE = 16
CMP_HID = HEAD_DIM
SEL_LEN = 64
SEL_TOP = 16
WINDOW = 512
D_FF = -(-8 * D_MODEL // (3 * 256)) * 256
EPS = 1e-6
NEG = -1e30

F32 = jnp.float32
BF16 = jnp.bfloat16

LANES = 128
SUBLANES = 8
VMEM_BYTES_V7X = 64 * 1024 * 1024


def _round_up(n, m):
    return -(-n // m) * m


def _pick_tile(n, cap, align=LANES):
    best = None
    for t in range(align, min(n, cap) + 1, align):
        if n % t == 0:
            best = t
    assert best is not None, (n, cap, align)
    return best


M_PROMPT = BATCH * SEQ
M_SAMPLE = DEC_BATCH * DEC_SEQ
S0 = M_PROMPT
M_PAD = _round_up(M_PROMPT + M_SAMPLE, LANES)
TM = _pick_tile(M_PAD, 2048)
TR = _pick_tile(M_PAD, 256)
MLSTM_L = 128
TQ = 128
TKS = 256
TKW = 128
N_GATE_PAD = LANES
D_FF_PAD = _round_up(D_FF, 1024)
CMP_R = CMP_LEN // CMP_STRIDE
SAMPLE_PAGES_PER_STEP = 8
PERM_ROWS = LANES

assert DEC_SEQ == 1 and S0 % 16 == 0
assert SEQ % PERM_ROWS == 0 and PAGE_SIZE % PERM_ROWS == 0
assert (PERM_ROWS // CMP_STRIDE) & (PERM_ROWS // CMP_STRIDE - 1) == 0
assert SEQ % MLSTM_L == 0 and SEQ % TQ == 0 and SEQ % TKS == 0 and TQ == TKW
assert PAST_LEN % PAGE_SIZE == 0 and PAGE_SIZE % SEL_LEN == 0 and PAGE_SIZE % CMP_STRIDE == 0
assert CMP_R == 2 and SEQ // SEL_LEN <= LANES


def _vmem_limit(block_bytes):
    return int(min(2 * block_bytes + (8 << 20), VMEM_BYTES_V7X - (8 << 20)))


def _params(sem, block_bytes):
    return pltpu.CompilerParams(dimension_semantics=sem, vmem_limit_bytes=_vmem_limit(block_bytes))


def _dot(a, b):
    return jnp.dot(a, b, preferred_element_type=F32)


def _dot_nt(a, b):
    return lax.dot_general(a, b, (((1,), (1,)), ((), ())), preferred_element_type=F32)


def _dot_tn(a, b):
    return lax.dot_general(a, b, (((0,), (0,)), ((), ())), preferred_element_type=F32)


def _dot_exact(a, b):
    return jnp.dot(a, b, precision=lax.Precision.HIGHEST, preferred_element_type=F32)


def _log_sigmoid(x):
    return jnp.minimum(x, 0.0) - jnp.log1p(jnp.exp(-jnp.abs(x)))


def _col_form(row, n):
    return jnp.broadcast_to(row, (n, n)).T


def _rms_kernel(x_ref, g_ref, o_ref):
    x = x_ref[...]
    y = x * lax.rsqrt(jnp.mean(x * x, axis=-1, keepdims=True) + EPS) * g_ref[...]
    o_ref[...] = y.astype(o_ref.dtype)


def _rmsnorm(x, g, out_dtype):
    m, d = x.shape
    blk = TR * d * (4 + jnp.dtype(out_dtype).itemsize)
    return pl.pallas_call(
        _rms_kernel,
        grid=(m // TR,),
        in_specs=[pl.BlockSpec((TR, d), lambda i: (i, 0)), pl.BlockSpec((1, d), lambda i: (0, 0))],
        out_specs=pl.BlockSpec((TR, d), lambda i: (i, 0)),
        out_shape=jax.ShapeDtypeStruct((m, d), out_dtype),
        compiler_params=_params(("parallel",), blk),
        name="rmsnorm",
    )(x, g.reshape(1, d))


def _rms_gate_kernel(x_ref, g_ref, w_ref, b_ref, h_ref, o_ref, *, mlstm_gates):
    x = x_ref[...]
    y = x * lax.rsqrt(jnp.mean(x * x, axis=-1, keepdims=True) + EPS) * g_ref[...]
    h_ref[...] = y.astype(h_ref.dtype)
    z = _dot_exact(y, w_ref[...]) + b_ref[...]
    if mlstm_gates:
        gl = GATE_CAP * jnp.tanh(z / GATE_CAP)
        lane = lax.broadcasted_iota(jnp.int32, z.shape, 1)
        z = jnp.where(lane < MLSTM_HEADS, gl, _log_sigmoid(gl))
    o_ref[...] = z


def _rmsnorm_with_gates(x, g, w_gate, b_gate, mlstm_gates):
    m, d = x.shape
    ng = w_gate.shape[1]
    wp = jnp.pad(w_gate, ((0, 0), (0, N_GATE_PAD - ng)))
    bp = jnp.pad(b_gate, (0, N_GATE_PAD - ng)).reshape(1, N_GATE_PAD)
    blk = TR * d * 6 + d * N_GATE_PAD * 4 + TR * N_GATE_PAD * 4
    return pl.pallas_call(
        functools.partial(_rms_gate_kernel, mlstm_gates=mlstm_gates),
        grid=(m // TR,),
        in_specs=[pl.BlockSpec((TR, d), lambda i: (i, 0)),
                  pl.BlockSpec((1, d), lambda i: (0, 0)),
                  pl.BlockSpec((d, N_GATE_PAD), lambda i: (0, 0)),
                  pl.BlockSpec((1, N_GATE_PAD), lambda i: (0, 0))],
        out_specs=[pl.BlockSpec((TR, d), lambda i: (i, 0)),
                   pl.BlockSpec((TR, N_GATE_PAD), lambda i: (i, 0))],
        out_shape=[jax.ShapeDtypeStruct((m, d), BF16), jax.ShapeDtypeStruct((m, N_GATE_PAD), F32)],
        compiler_params=_params(("parallel",), blk),
        name="rmsnorm_gates",
    )(x, g.reshape(1, d), wp, bp)


def _mm_kernel(*refs, nk, has_res):
    if has_res:
        x_ref, w_ref, r_ref, o_ref, acc_ref = refs
    else:
        x_ref, w_ref, o_ref, acc_ref = refs
    k = pl.program_id(2)

    @pl.when(k == 0)
    def _():
        acc_ref[...] = jnp.zeros_like(acc_ref)

    acc_ref[...] += _dot(x_ref[...], w_ref[...])

    @pl.when(k == nk - 1)
    def _():
        r = acc_ref[...]
        if has_res:
            r = r + r_ref[...]
        o_ref[...] = r.astype(o_ref.dtype)


def _matmul(x, w, out_dtype, residual=None):
    m, kd = x.shape
    n = w.shape[1]
    tn = _pick_tile(n, 512)
    tk = _pick_tile(kd, 1024)
    nk = kd // tk
    osz = jnp.dtype(out_dtype).itemsize
    in_specs = [pl.BlockSpec((TM, tk), lambda i, j, k: (i, k)),
                pl.BlockSpec((tk, tn), lambda i, j, k: (k, j))]
    args = [x, w]
    blk = TM * tk * 2 + tk * tn * 2 + TM * tn * (osz + 4)
    if residual is not None:
        in_specs.append(pl.BlockSpec((TM, tn), lambda i, j, k: (i, j)))
        args.append(residual)
        blk += TM * tn * 4
    return pl.pallas_call(
        functools.partial(_mm_kernel, nk=nk, has_res=residual is not None),
        grid=(m // TM, n // tn, nk),
        in_specs=in_specs,
        out_specs=pl.BlockSpec((TM, tn), lambda i, j, k: (i, j)),
        out_shape=jax.ShapeDtypeStruct((m, n), out_dtype),
        scratch_shapes=[pltpu.VMEM((TM, tn), F32)],
        compiler_params=_params(("parallel", "parallel", "arbitrary"), blk),
        name="matmul",
    )(*args)


def _ffn_up_kernel(x_ref, wg_ref, wu_ref, o_ref, accg_ref, accu_ref, *, nk):
    k = pl.program_id(2)

    @pl.when(k == 0)
    def _():
        accg_ref[...] = jnp.zeros_like(accg_ref)
        accu_ref[...] = jnp.zeros_like(accu_ref)

    x = x_ref[...]
    accg_ref[...] += _dot(x, wg_ref[...])
    accu_ref[...] += _dot(x, wu_ref[...])

    @pl.when(k == nk - 1)
    def _():
        o_ref[...] = (jax.nn.silu(accg_ref[...]) * accu_ref[...]).astype(o_ref.dtype)


def _ffn_up(x, wg, wu):
    m, kd = x.shape
    n = wg.shape[1]
    tn = _pick_tile(n, 512)
    tk = _pick_tile(kd, 1024)
    nk = kd // tk
    blk = TM * tk * 2 + 2 * tk * tn * 2 + TM * tn * (2 + 8)
    return pl.pallas_call(
        functools.partial(_ffn_up_kernel, nk=nk),
        grid=(m // TM, n // tn, nk),
        in_specs=[pl.BlockSpec((TM, tk), lambda i, j, k: (i, k)),
                  pl.BlockSpec((tk, tn), lambda i, j, k: (k, j)),
                  pl.BlockSpec((tk, tn), lambda i, j, k: (k, j))],
        out_specs=pl.BlockSpec((TM, tn), lambda i, j, k: (i, j)),
        out_shape=jax.ShapeDtypeStruct((m, n), BF16),
        scratch_shapes=[pltpu.VMEM((TM, tn), F32), pltpu.VMEM((TM, tn), F32)],
        compiler_params=_params(("parallel", "parallel", "arbitrary"), blk),
        name="ffn_up",
    )(x, wg, wu)


def _mlstm_prompt_kernel(q_ref, k_ref, v_ref, o_ref, g_ref, gh_ref, hs_ref, c_ref, n_ref, m_ref):
    L = MLSTM_L

    @pl.when(pl.program_id(2) == 0)
    def _():
        c_ref[...] = jnp.zeros_like(c_ref)
        n_ref[...] = jnp.zeros_like(n_ref)
        m_ref[...] = jnp.zeros_like(m_ref)

    ig_row = g_ref[0, 0]
    lf_row = g_ref[1, 0]
    m_prev = m_ref[0, 0]
    t_idx = lax.broadcasted_iota(jnp.int32, (L, L), 0)
    s_idx = lax.broadcasted_iota(jnp.int32, (L, L), 1)
    lf_c = _col_form(lf_row, L)
    b_row = jnp.sum(jnp.where(t_idx <= s_idx, lf_c, 0.0), axis=0, keepdims=True)
    b_r = jnp.broadcast_to(b_row, (L, L))
    b_c = b_r.T
    ig_r = jnp.broadcast_to(ig_row, (L, L))
    log_d = jnp.where(s_idx <= t_idx, b_c - b_r + ig_r, -jnp.inf)
    b_col = b_c[:, 0:1]
    inter = b_col + m_prev
    m_t = jnp.maximum(jnp.max(log_d, axis=1, keepdims=True), inter)
    d = jnp.exp(log_d - m_t)
    w_inter = jnp.exp(inter - m_t)

    q = q_ref[...] * (MLSTM_QK_DIM ** -0.5)
    k = k_ref[...]
    qb = q.astype(BF16)
    kb = k.astype(BF16)
    vb = v_ref[...].astype(BF16)
    s = _dot_nt(qb, kb) * d
    c_prev = c_ref[0, 0]
    n_prev = n_ref[0, 0]
    num = w_inter * _dot(qb, c_prev.astype(BF16)) + _dot(s.astype(BF16), vb)
    den = w_inter * jnp.sum(q * n_prev, axis=1, keepdims=True) + jnp.sum(s, axis=1, keepdims=True)
    h = num / jnp.maximum(jnp.abs(den), jnp.exp(-m_t))
    hn = h * lax.rsqrt(jnp.mean(h * h, axis=1, keepdims=True) + EPS)
    hs_ref[...] = (hn * gh_ref[...] * jax.nn.sigmoid(o_ref[...])).astype(hs_ref.dtype)

    m_new = m_t[L - 1:L, :]
    b_last = b_row[:, L - 1:L]
    ig_col = ig_r.T[:, 0:1]
    decay = jnp.exp(b_last - b_col + ig_col - m_new)
    carry = jnp.exp(b_last + m_prev - m_new)
    kd = k * decay
    c_ref[0, 0] = carry * c_prev + _dot_tn(kd.astype(BF16), vb)
    n_ref[0, 0] = carry * n_prev + jnp.sum(kd, axis=0, keepdims=True)
    m_ref[0, 0] = m_new


def _mlstm_prompt(z, gates_t, g_head):
    H, DK, DV, L = MLSTM_HEADS, MLSTM_QK_DIM, MLSTM_V_DIM, MLSTM_L
    nc = SEQ // L
    v0 = (2 * H * DK) // DV
    row = lambda b, h, c: b * nc + c
    blk = L * (2 * DK + 2 * DV) * 4 + L * DV * 2 + DK * DV * 4
    return pl.pallas_call(
        _mlstm_prompt_kernel,
        grid=(BATCH, H, nc),
        in_specs=[pl.BlockSpec((L, DK), lambda b, h, c: (row(b, h, c), h)),
                  pl.BlockSpec((L, DK), lambda b, h, c: (row(b, h, c), H + h)),
                  pl.BlockSpec((L, DV), lambda b, h, c: (row(b, h, c), v0 + h)),
                  pl.BlockSpec((L, DV), lambda b, h, c: (row(b, h, c), v0 + H + h)),
                  pl.BlockSpec((2, 1, 1, L), lambda b, h, c: (0, h, 0, row(b, h, c))),
                  pl.BlockSpec((1, DV), lambda b, h, c: (0, h))],
        out_specs=[pl.BlockSpec((L, DV), lambda b, h, c: (row(b, h, c), h)),
                   pl.BlockSpec((1, 1, DK, DV), lambda b, h, c: (b, h, 0, 0)),
                   pl.BlockSpec((1, 1, 1, DK), lambda b, h, c: (b, h, 0, 0)),
                   pl.BlockSpec((1, 1, 1, 1), lambda b, h, c: (b, h, 0, 0))],
        out_shape=[jax.ShapeDtypeStruct((M_PROMPT, H * DV), BF16),
                   jax.ShapeDtypeStruct((BATCH, H, DK, DV), F32),
                   jax.ShapeDtypeStruct((BATCH, H, 1, DK), F32),
                   jax.ShapeDtypeStruct((BATCH, H, 1, 1), F32)],
        compiler_params=_params(("parallel", "parallel", "arbitrary"), blk),
        name="mlstm_prompt",
    )(z, z, z, z, gates_t, g_head.reshape(1, H * DV))


def _mlstm_sample_kernel(q_ref, k_ref, v_ref, o_ref, ig_ref, lf_ref, m0_ref, c0_ref, n0_ref, gh_ref,
                         hs_ref, c_ref, n_ref, m_ref):
    DK = MLSTM_QK_DIM
    b = pl.program_id(0)
    q = q_ref[pl.ds(b, 1), :] * (DK ** -0.5)
    k = k_ref[pl.ds(b, 1), :]
    v = v_ref[pl.ds(b, 1), :]
    og = o_ref[pl.ds(b, 1), :]
    ig = ig_ref[0, 0]
    lf = lf_ref[0, 0]
    m0 = m0_ref[0, 0]
    m_t = jnp.maximum(ig, lf + m0)
    d = jnp.exp(ig - m_t)[:, 0:1]
    w = jnp.exp(lf + m0 - m_t)[:, 0:1]
    c0 = c0_ref[0, 0]
    n0 = n0_ref[0, 0]
    q_col = _col_form(q, DK)[:, 0:1]
    k_col = _col_form(k, DK)[:, 0:1]
    s = jnp.sum(q * k, axis=1, keepdims=True) * d
    num = w * jnp.sum(q_col * c0, axis=0, keepdims=True) + s * v
    den = w * jnp.sum(q * n0, axis=1, keepdims=True) + s
    h = num / jnp.maximum(jnp.abs(den), jnp.exp(-m_t)[:, 0:1])
    hn = h * lax.rsqrt(jnp.mean(h * h, axis=1, keepdims=True) + EPS)
    hs_ref[0] = hn * gh_ref[...] * jax.nn.sigmoid(og)
    c_ref[0, 0] = w * c0 + d * (k_col * v)
    n_ref[0, 0] = w * n0 + d * k
    m_ref[0, 0] = m_t


def _mlstm_sample(z, ig_b, lf_b, m_b, c0, n0, g_head):
    H, DK, DV = MLSTM_HEADS, MLSTM_QK_DIM, MLSTM_V_DIM
    DB = DEC_BATCH
    r0 = S0 // SUBLANES
    v0 = (2 * H * DK) // DV
    sc = pl.BlockSpec((1, 1, 1, LANES), lambda b, h: (b, h, 0, 0))
    blk = 2 * DK * DV * 4 + SUBLANES * (2 * DK + 2 * DV) * 4
    return pl.pallas_call(
        _mlstm_sample_kernel,
        grid=(DB, H),
        in_specs=[pl.BlockSpec((SUBLANES, DK), lambda b, h: (r0, h)),
                  pl.BlockSpec((SUBLANES, DK), lambda b, h: (r0, H + h)),
                  pl.BlockSpec((SUBLANES, DV), lambda b, h: (r0, v0 + h)),
                  pl.BlockSpec((SUBLANES, DV), lambda b, h: (r0, v0 + H + h)),
                  sc, sc, sc,
                  pl.BlockSpec((1, 1, DK, DV), lambda b, h: (b, h, 0, 0)),
                  pl.BlockSpec((1, 1, 1, DK), lambda b, h: (b, h, 0, 0)),
                  pl.BlockSpec((1, DV), lambda b, h: (0, h))],
        out_specs=[pl.BlockSpec((1, 1, DV), lambda b, h: (b, 0, h)),
                   pl.BlockSpec((1, 1, DK, DV), lambda b, h: (b, h, 0, 0)),
                   pl.BlockSpec((1, 1, 1, DK), lambda b, h: (b, h, 0, 0)),
                   sc],
        out_shape=[jax.ShapeDtypeStruct((DB, 1, H * DV), F32),
                   jax.ShapeDtypeStruct((DB, H, DK, DV), F32),
                   jax.ShapeDtypeStruct((DB, H, 1, DK), F32),
                   jax.ShapeDtypeStruct((DB, H, 1, LANES), F32)],
        compiler_params=_params(("parallel", "parallel"), blk),
        name="mlstm_sample",
    )(z, z, z, z, ig_b, lf_b, m_b, c0, n0, g_head.reshape(1, H * DV))


def _compress_proj_kernel(*refs, n_in, groups_per_in, lead):
    x_refs = refs[-(n_in + 3):-3]
    w_ref, p_ref, xp_scr = refs[-3], refs[-2], refs[-1]
    G, D = NSA_KV_HEADS, HEAD_DIM
    cpg = PERM_ROWS // CMP_STRIDE
    nch = n_in * groups_per_in * cpg
    i_idx = lax.broadcasted_iota(jnp.int32, (PERM_ROWS, PERM_ROWS), 0)
    j_idx = lax.broadcasted_iota(jnp.int32, (PERM_ROWS, PERM_ROWS), 1)
    src_row = (i_idx & (cpg - 1)) * CMP_STRIDE + (i_idx >> (cpg.bit_length() - 1))
    perm = (j_idx == src_row).astype(BF16)
    for i, xr in enumerate(x_refs):
        for p in range(groups_per_in):
            rows = pl.ds(p * PERM_ROWS, PERM_ROWS)
            xg = xr[0, rows, :] if lead else xr[rows, :]
            xg = _dot(perm, xg.astype(BF16))
            ch0 = (i * groups_per_in + p) * cpg
            for c in range(CMP_STRIDE):
                xp_scr[c, ch0:ch0 + cpg, :] = xg[c * cpg:(c + 1) * cpg, :]
    for s in range(2):
        acc = jnp.zeros((G * nch, CMP_R * CMP_HID), F32)
        for c in range(CMP_STRIDE):
            xs = jnp.concatenate([xp_scr[c, :, (s * G + g) * D:(s * G + g + 1) * D] for g in range(G)], axis=0)
            acc = acc + _dot(xs.astype(BF16), w_ref[s, c])
        for g in range(G):
            p_ref[0, s, g] = acc[g * nch:(g + 1) * nch]


def _compress_weights(w_cmp1):
    w = w_cmp1.reshape(2, CMP_R, CMP_STRIDE, HEAD_DIM, CMP_HID)
    return w.transpose(0, 2, 3, 1, 4).reshape(2, CMP_STRIDE, HEAD_DIM, CMP_R * CMP_HID).astype(BF16)


def _compress_proj_prompt(rows, w1r):
    G, D = NSA_KV_HEADS, HEAD_DIM
    nch = SEQ // CMP_STRIDE
    pw = CMP_R * CMP_HID
    blk = SEQ * 2 * G * D * 4 + w1r.size * 2 + 2 * G * nch * pw * 4 + SEQ * 2 * G * D * 4
    return pl.pallas_call(
        functools.partial(_compress_proj_kernel, n_in=1, groups_per_in=SEQ // PERM_ROWS, lead=False),
        grid=(BATCH,),
        in_specs=[pl.BlockSpec((SEQ, 2 * G * D), lambda b: (b, 0)),
                  pl.BlockSpec(w1r.shape, lambda b: (0, 0, 0, 0))],
        out_specs=pl.BlockSpec((1, 2, G, nch, pw), lambda b: (b, 0, 0, 0, 0)),
        out_shape=jax.ShapeDtypeStruct((BATCH, 2, G, nch, pw), F32),
        scratch_shapes=[pltpu.VMEM((CMP_STRIDE, nch, 2 * G * D), F32)],
        compiler_params=_params(("parallel",), blk),
        name="compress_proj_prompt",
    )(rows, w1r)


def _compress_proj_sample(cache2d, page_table_flat, w1r):
    G, D = NSA_KV_HEADS, HEAD_DIM
    n_pages = PAST_LEN // PAGE_SIZE
    pps = SAMPLE_PAGES_PER_STEP
    assert n_pages % pps == 0
    cpi = PAGE_SIZE // CMP_STRIDE
    nch_step = pps * cpi
    pw = CMP_R * CMP_HID

    def page_map(b, j, pt, *, i):
        return (pt[b * n_pages + j * pps + i], 0, 0)

    in_specs = [pl.BlockSpec((1, PAGE_SIZE, 2 * G * D), functools.partial(page_map, i=i)) for i in range(pps)]
    in_specs.append(pl.BlockSpec(w1r.shape, lambda b, j, pt: (0, 0, 0, 0)))
    blk = 2 * pps * PAGE_SIZE * 2 * G * D * 4 + w1r.size * 2 + 2 * G * nch_step * pw * 4
    return pl.pallas_call(
        functools.partial(_compress_proj_kernel, n_in=pps, groups_per_in=PAGE_SIZE // PERM_ROWS, lead=True),
        grid_spec=pltpu.PrefetchScalarGridSpec(
            num_scalar_prefetch=1,
            grid=(DEC_BATCH, n_pages // pps),
            in_specs=in_specs,
            out_specs=pl.BlockSpec((1, 2, G, nch_step, pw), lambda b, j, pt: (b, 0, 0, j, 0)),
            scratch_shapes=[pltpu.VMEM((CMP_STRIDE, nch_step, 2 * G * D), F32)],
        ),
        out_shape=jax.ShapeDtypeStruct((DEC_BATCH, 2, G, n_pages * cpi, pw), F32),
        compiler_params=_params(("parallel", "arbitrary"), blk),
        name="compress_proj_sample",
    )(page_table_flat, *([cache2d] * pps), w1r)


def _compress_finish_kernel(p_ref, pe_ref, w1_ref, w2_ref, o_ref):
    nch = p_ref.shape[3]
    pe = jnp.broadcast_to(pe_ref[0], (SUBLANES, pe_ref.shape[2])).astype(BF16)
    pe_term = _dot(pe, w1_ref[0].astype(BF16))[0:1]
    w2 = w2_ref[0].astype(BF16)
    for g in range(NSA_KV_HEADS):
        p = p_ref[0, 0, g]
        nxt = pltpu.roll(p[:, CMP_HID:], shift=nch - 1, axis=0)
        hidden = p[:, :CMP_HID] + nxt + pe_term
        o_ref[0, 0, g] = _dot(jax.nn.gelu(hidden).astype(BF16), w2)


def _compress_finish(p, pe_cmp, w_cmp1, w_cmp2):
    nb, _, G, nch, pw = p.shape
    pe = pe_cmp.reshape(2, 1, CMP_LEN * HEAD_DIM)
    blk = G * nch * (pw + HEAD_DIM) * 4 + CMP_LEN * HEAD_DIM * (CMP_HID + 1) * 4
    return pl.pallas_call(
        _compress_finish_kernel,
        grid=(nb, 2),
        in_specs=[pl.BlockSpec((1, 1, G, nch, pw), lambda b, s: (b, s, 0, 0, 0)),
                  pl.BlockSpec((1, 1, CMP_LEN * HEAD_DIM), lambda b, s: (s, 0, 0)),
                  pl.BlockSpec((1, CMP_LEN * HEAD_DIM, CMP_HID), lambda b, s: (s, 0, 0)),
                  pl.BlockSpec((1, CMP_HID, HEAD_DIM), lambda b, s: (s, 0, 0))],
        out_specs=pl.BlockSpec((1, 1, G, nch, HEAD_DIM), lambda b, s: (b, s, 0, 0, 0)),
        out_shape=jax.ShapeDtypeStruct((nb, 2, G, nch, HEAD_DIM), F32),
        compiler_params=_params(("parallel", "parallel"), blk),
        name="compress_finish",
    )(p, pe, w_cmp1, w_cmp2)


def _selection_overlap(n_cmp, n_sel, rows, cols):
    start = np.arange(n_cmp)[:, None] * CMP_STRIDE
    end = start + CMP_LEN - 1
    j = np.arange(n_sel)[None, :]
    ov = ((start <= (j + 1) * SEL_LEN - 1) & (end >= j * SEL_LEN)).astype(np.float32)
    out = np.zeros((rows, cols), np.float32)
    out[:n_cmp, :n_sel] = ov
    return jnp.asarray(out)


def _masked_softmax(s, valid):
    sm = jnp.where(valid, s, NEG)
    e = jnp.exp(sm - jnp.max(sm, axis=-1, keepdims=True))
    return jnp.where(valid, e / jnp.sum(e, axis=-1, keepdims=True), 0.0)


def _flash_step(carry, s, valid, vb):
    m, l, acc = carry
    sm = jnp.where(valid, s, NEG)
    m_new = jnp.maximum(m, jnp.max(sm, axis=-1, keepdims=True))
    alpha = jnp.exp(m - m_new)
    e = jnp.exp(sm - m_new)
    l = alpha * l + jnp.sum(e, axis=-1, keepdims=True)
    acc = alpha * acc + _dot(jnp.where(valid, e, 0.0).astype(BF16), vb)
    return m_new, l, acc


def _flash_one(carry, s_new, v_new):
    m, l, acc = carry
    m_new = jnp.maximum(m, s_new)
    alpha = jnp.exp(m - m_new)
    e = jnp.exp(s_new - m_new)
    return m_new, alpha * l + e, alpha * acc + e * v_new


def _flash_init(rows):
    return (jnp.full((rows, 1), -jnp.inf, F32), jnp.zeros((rows, 1), F32), jnp.zeros((rows, HEAD_DIM), F32))


def _top_rank(impm_c, impm_r, jp_idx, j_idx, axis):
    before = (impm_c > impm_r) | ((impm_c == impm_r) & (jp_idx < j_idx))
    return jnp.sum(before.astype(jnp.int32), axis=axis, keepdims=True)


def _nsa_prompt_kernel(slopes_ref, q_ref, gl_ref, kc_ref, vc_ref, ks_ref, vs_ref, kw_ref, vw_ref,
                       ov_ref, e_ref, out_ref, mask_scr):
    g = pl.program_id(1)
    qi = pl.program_id(2)
    D = HEAD_DIM
    scale = D ** -0.5
    n_sel = SEQ // SEL_LEN
    qs = qi * TQ
    tpos = qs + lax.broadcasted_iota(jnp.int32, (TQ, 1), 0)

    kcb = kc_ref[0, 0, 0].astype(BF16)
    vcb = vc_ref[0, 0, 0].astype(BF16)
    ncp = kcb.shape[0]
    cpos = lax.broadcasted_iota(jnp.int32, (1, ncp), 1) * CMP_STRIDE + (CMP_LEN - 1)
    dist_c = tpos - cpos
    valid_c = dist_c >= 0
    dist_cf = dist_c.astype(F32)
    o_cmp = []
    p_sum = jnp.zeros((TQ, ncp), F32)
    for r in range(GRP):
        qr = q_ref[:, r * D:(r + 1) * D]
        s = _dot_nt(qr, kcb) * scale - slopes_ref[g * GRP + r] * dist_cf
        p = _masked_softmax(s, valid_c)
        o_cmp.append(_dot(p.astype(BF16), vcb))
        p_sum = p_sum + p
    imp = _dot_exact(p_sum, ov_ref[...])

    j_idx = lax.broadcasted_iota(jnp.int32, (1, LANES), 1)
    cur = tpos // SEL_LEN
    forced = (j_idx == 0) | (j_idx == cur) | (j_idx == cur - 1)
    impm = jnp.where(forced, jnp.inf, jnp.where(j_idx <= cur, imp, -jnp.inf))
    rank = jnp.zeros((TQ, LANES), jnp.int32)
    for jp in range(n_sel):
        cj = impm[:, jp:jp + 1]
        rank = rank + ((cj > impm) | ((cj == impm) & (jp < j_idx))).astype(jnp.int32)
    sel = ((rank < min(SEL_TOP, n_sel)) & (j_idx <= cur)).astype(BF16)
    for kt in range(SEQ // TKS):
        mask_scr[kt] = _dot(sel, e_ref[:, kt * TKS:(kt + 1) * TKS])

    n_kt_sel = (qs + TQ + TKS - 1) // TKS
    w_lo = jnp.maximum(qs - WINDOW, 0) // TKW
    w_hi = (qs + TQ - 1) // TKW + 1
    for r in range(GRP):
        qr = q_ref[:, r * D:(r + 1) * D]
        slope = slopes_ref[g * GRP + r]

        def sel_body(kt, carry):
            k0 = pl.multiple_of(kt * TKS, TKS)
            kb = ks_ref[pl.ds(k0, TKS), :].astype(BF16)
            vb = vs_ref[pl.ds(k0, TKS), :].astype(BF16)
            dist = tpos - (k0 + lax.broadcasted_iota(jnp.int32, (1, TKS), 1))
            s = _dot_nt(qr, kb) * scale - slope * dist.astype(F32)
            valid = (mask_scr[kt] > 0.5) & (dist >= 0)
            return _flash_step(carry, s, valid, vb)

        _, l_s, acc_s = lax.fori_loop(0, n_kt_sel, sel_body, _flash_init(TQ))

        def win_body(kt, carry):
            k0 = pl.multiple_of(kt * TKW, TKW)
            kb = kw_ref[pl.ds(k0, TKW), :].astype(BF16)
            vb = vw_ref[pl.ds(k0, TKW), :].astype(BF16)
            dist = tpos - (k0 + lax.broadcasted_iota(jnp.int32, (1, TKW), 1))
            s = _dot_nt(qr, kb) * scale - slope * dist.astype(F32)
            valid = (dist >= 0) & (dist <= WINDOW)
            return _flash_step(carry, s, valid, vb)

        _, l_w, acc_w = lax.fori_loop(w_lo, w_hi, win_body, _flash_init(TQ))

        gates = jax.nn.sigmoid(gl_ref[0, :, 3 * r:3 * r + 3])
        out = gates[:, 0:1] * o_cmp[r] + gates[:, 1:2] * (acc_s / l_s) + gates[:, 2:3] * (acc_w / l_w)
        out_ref[:, r * D:(r + 1) * D] = out.astype(out_ref.dtype)


def _nsa_prompt(q, gate_logits, kvc, rows, slopes, ov, expand):
    G, D = NSA_KV_HEADS, HEAD_DIM
    nq = SEQ // TQ
    ncp = kvc.shape[3]
    row = lambda b, g, qi, sl: b * nq + qi
    kv_spec = lambda slot: pl.BlockSpec((SEQ, D), lambda b, g, qi, sl: (b, slot * G + g))
    blk = (TQ * GRP * D * 4 + 2 * ncp * D * 4 + 4 * SEQ * D * 4 + LANES * LANES * 4 + LANES * SEQ * 2
           + TQ * SEQ * 4)
    return pl.pallas_call(
        _nsa_prompt_kernel,
        grid_spec=pltpu.PrefetchScalarGridSpec(
            num_scalar_prefetch=1,
            grid=(BATCH, G, nq),
            in_specs=[pl.BlockSpec((TQ, GRP * D), lambda b, g, qi, sl: (row(b, g, qi, sl), g)),
                      pl.BlockSpec((1, TQ, 3 * GRP), lambda b, g, qi, sl: (g, row(b, g, qi, sl), 0)),
                      pl.BlockSpec((1, 1, 1, ncp, D), lambda b, g, qi, sl: (b, 0, g, 0, 0)),
                      pl.BlockSpec((1, 1, 1, ncp, D), lambda b, g, qi, sl: (b, 1, g, 0, 0)),
                      kv_spec(2), kv_spec(3), kv_spec(4), kv_spec(5),
                      pl.BlockSpec(ov.shape, lambda b, g, qi, sl: (0, 0)),
                      pl.BlockSpec(expand.shape, lambda b, g, qi, sl: (0, 0))],
            out_specs=pl.BlockSpec((TQ, GRP * D), lambda b, g, qi, sl: (row(b, g, qi, sl), g)),
            scratch_shapes=[pltpu.VMEM((SEQ // TKS, TQ, TKS), F32)],
        ),
        out_shape=jax.ShapeDtypeStruct((M_PROMPT, NSA_HEADS * D), BF16),
        compiler_params=_params(("parallel", "parallel", "arbitrary"), blk),
        name="nsa_prompt",
    )(slopes, q, gate_logits, kvc, kvc, rows, rows, rows, rows, ov, expand)


T_SAMPLE = PAST_LEN
N_SEL_S = -(-(PAST_LEN + 1) // SEL_LEN)
NB_PAST = PAST_LEN // SEL_LEN
KK_S = min(SEL_TOP, N_SEL_S)
SEL_PAD_S = _round_up(N_SEL_S, LANES)
assert PAST_LEN % SEL_LEN == 0 and PAST_LEN % CMP_STRIDE == 0


def _slope_col(slopes_ref, g):
    r_idx = lax.broadcasted_iota(jnp.int32, (GRP, 1), 0)
    col = jnp.zeros((GRP, 1), F32)
    for r in range(GRP):
        col = jnp.where(r_idx == r, slopes_ref[g * GRP + r], col)
    return col


def _nsa_sample_cmp_win_kernel(slopes_ref, q_ref, kc_ref, vc_ref, ov_ref, kw_ref, vw_ref, kwn_ref, vwn_ref,
                               oc_ref, ow_ref, idx_ref):
    g = pl.program_id(1)
    scale = HEAD_DIM ** -0.5
    q = q_ref[0, 0]
    slope = _slope_col(slopes_ref, g)

    kcb = kc_ref[0, 0, 0].astype(BF16)
    ncp = kcb.shape[0]
    cpos = lax.broadcasted_iota(jnp.int32, (1, ncp), 1) * CMP_STRIDE + (CMP_LEN - 1)
    dist_c = T_SAMPLE - cpos
    p_c = _masked_softmax(_dot_nt(q, kcb) * scale - slope * dist_c.astype(F32), dist_c >= 0)
    oc_ref[0, 0] = _dot(p_c.astype(BF16), vc_ref[0, 0, 0].astype(BF16))
    p_sum = jnp.broadcast_to(jnp.sum(p_c, axis=0, keepdims=True), (SUBLANES, ncp))
    imp = _dot_exact(p_sum, ov_ref[...])[0:1]

    NP = SEL_PAD_S
    cur = T_SAMPLE // SEL_LEN
    j_row = lax.broadcasted_iota(jnp.int32, (1, NP), 1)
    forced = (j_row == 0) | (j_row == cur) | (j_row == cur - 1)
    impm = jnp.where(forced, jnp.inf, jnp.where(j_row <= cur, imp, -jnp.inf))
    sub = lax.broadcasted_iota(jnp.int32, (NP, NP), 0)
    lane = lax.broadcasted_iota(jnp.int32, (NP, NP), 1)
    impm_r = jnp.broadcast_to(impm, (NP, NP))
    rank = _top_rank(impm_r.T, impm_r, sub, lane, 0).astype(F32)
    rank_c = jnp.broadcast_to(rank, (NP, NP)).T
    hit = (rank_c == lane.astype(F32)) & (sub <= cur)
    idx_ref[0, 0] = jnp.sum(jnp.where(hit, sub, 0), axis=0, keepdims=True)

    kwb = kw_ref[0].astype(BF16)
    wlen = kwb.shape[0]
    wpos = T_SAMPLE - wlen + lax.broadcasted_iota(jnp.int32, (1, wlen), 1)
    dist_w = T_SAMPLE - wpos
    valid_w = (dist_w >= 0) & (dist_w <= WINDOW) & (wpos >= 0)
    carry = _flash_step(_flash_init(GRP), _dot_nt(q, kwb) * scale - slope * dist_w.astype(F32), valid_w,
                        vw_ref[0].astype(BF16))
    k_new = kwn_ref[0, 0, 0].astype(BF16).astype(F32)
    v_new = vwn_ref[0, 0, 0].astype(BF16).astype(F32)
    s_new = jnp.sum(q.astype(F32) * k_new, axis=1, keepdims=True) * scale
    _, l_w, acc_w = _flash_one(carry, s_new, v_new)
    ow_ref[0, 0] = acc_w / l_w


def _nsa_sample_cmp_win(slopes, q_s, kvc, ov, win2d, rows_s):
    G, D, DB = NSA_KV_HEADS, HEAD_DIM, DEC_BATCH
    ncp = kvc.shape[3]
    wlen = win2d.shape[1]
    o_spec = pl.BlockSpec((1, 1, GRP, D), lambda b, g, sl: (b, g, 0, 0))
    blk = 2 * ncp * D * 4 + ov.size * 4 + 2 * wlen * D * 4 + 4 * SEL_PAD_S * SEL_PAD_S * 4
    return pl.pallas_call(
        _nsa_sample_cmp_win_kernel,
        grid_spec=pltpu.PrefetchScalarGridSpec(
            num_scalar_prefetch=1,
            grid=(DB, G),
            in_specs=[pl.BlockSpec((1, 1, GRP, D), lambda b, g, sl: (b, g, 0, 0)),
                      pl.BlockSpec((1, 1, 1, ncp, D), lambda b, g, sl: (b, 0, g, 0, 0)),
                      pl.BlockSpec((1, 1, 1, ncp, D), lambda b, g, sl: (b, 1, g, 0, 0)),
                      pl.BlockSpec(ov.shape, lambda b, g, sl: (0, 0)),
                      pl.BlockSpec((1, wlen, D), lambda b, g, sl: (b, 0, g)),
                      pl.BlockSpec((1, wlen, D), lambda b, g, sl: (b, 0, G + g)),
                      pl.BlockSpec((1, 1, 1, 1, D), lambda b, g, sl: (b, 4, g, 0, 0)),
                      pl.BlockSpec((1, 1, 1, 1, D), lambda b, g, sl: (b, 5, g, 0, 0))],
            out_specs=[o_spec, o_spec, pl.BlockSpec((1, 1, 1, SEL_PAD_S), lambda b, g, sl: (b, g, 0, 0))],
        ),
        out_shape=[jax.ShapeDtypeStruct((DB, G, GRP, D), F32),
                   jax.ShapeDtypeStruct((DB, G, GRP, D), F32),
                   jax.ShapeDtypeStruct((DB, G, 1, SEL_PAD_S), jnp.int32)],
        compiler_params=_params(("parallel", "parallel"), blk),
        name="nsa_sample_cmp_win",
    )(slopes, q_s, kvc, kvc, ov, win2d, win2d, rows_s, rows_s)


def _nsa_sample_sel_kernel(pt_ref, idx_ref, slopes_ref, q_ref, ks_ref, vs_ref, ksn_ref, vsn_ref,
                           oc_ref, ow_ref, gl_ref, out_ref, m_scr, l_scr, acc_scr):
    b, g, k = pl.program_id(0), pl.program_id(1), pl.program_id(2)
    scale = HEAD_DIM ** -0.5
    q = q_ref[0, 0]
    slope = _slope_col(slopes_ref, g)
    j = idx_ref[(b * NSA_KV_HEADS + g) * KK_S + k]

    @pl.when(k == 0)
    def _():
        m0, l0, a0 = _flash_init(GRP)
        m_scr[...] = m0
        l_scr[...] = l0
        acc_scr[...] = a0

    carry = (m_scr[...], l_scr[...], acc_scr[...])

    @pl.when(j < NB_PAST)
    def _():
        dist = T_SAMPLE - (j * SEL_LEN + lax.broadcasted_iota(jnp.int32, (1, SEL_LEN), 1))
        s = _dot_nt(q, ks_ref[0].astype(BF16)) * scale - slope * dist.astype(F32)
        m1, l1, a1 = _flash_step(carry, s, dist >= 0, vs_ref[0].astype(BF16))
        m_scr[...] = m1
        l_scr[...] = l1
        acc_scr[...] = a1

    @pl.when(j >= NB_PAST)
    def _():
        k_new = ksn_ref[0, 0, 0].astype(BF16).astype(F32)
        v_new = vsn_ref[0, 0, 0].astype(BF16).astype(F32)
        s_new = jnp.sum(q.astype(F32) * k_new, axis=1, keepdims=True) * scale
        m1, l1, a1 = _flash_one(carry, s_new, v_new)
        m_scr[...] = m1
        l_scr[...] = l1
        acc_scr[...] = a1

    @pl.when(k == KK_S - 1)
    def _():
        gates = jax.nn.sigmoid(gl_ref[0, 0])
        o_s = acc_scr[...] / l_scr[...]
        out_ref[0, 0] = gates[0] * oc_ref[0, 0] + gates[1] * o_s + gates[2] * ow_ref[0, 0]


def _nsa_sample_sel(page_table_flat, idx_flat, slopes, q_s, cache_half, rows_s, o_c, o_w, gate_logits_s):
    G, D, DB = NSA_KV_HEADS, HEAD_DIM, DEC_BATCH
    n_pages = PAST_LEN // PAGE_SIZE
    bpp = PAGE_SIZE // SEL_LEN

    def blk_map(b, g, k, pt, idx, sl, *, slot):
        j = jnp.minimum(idx[(b * G + g) * KK_S + k], NB_PAST - 1)
        return (pt[b * n_pages + j // bpp] * bpp + j % bpp, 0, slot * G + g)

    o_spec = pl.BlockSpec((1, 1, GRP, D), lambda b, g, k, pt, idx, sl: (b, g, 0, 0))
    blk = 2 * SEL_LEN * D * 4 + 6 * GRP * D * 4
    return pl.pallas_call(
        _nsa_sample_sel_kernel,
        grid_spec=pltpu.PrefetchScalarGridSpec(
            num_scalar_prefetch=3,
            grid=(DB, G, KK_S),
            in_specs=[o_spec,
                      pl.BlockSpec((1, SEL_LEN, D), functools.partial(blk_map, slot=2)),
                      pl.BlockSpec((1, SEL_LEN, D), functools.partial(blk_map, slot=3)),
                      pl.BlockSpec((1, 1, 1, 1, D), lambda b, g, k, pt, idx, sl: (b, 2, g, 0, 0)),
                      pl.BlockSpec((1, 1, 1, 1, D), lambda b, g, k, pt, idx, sl: (b, 3, g, 0, 0)),
                      o_spec, o_spec,
                      pl.BlockSpec((1, 1, 3, GRP, 1), lambda b, g, k, pt, idx, sl: (b, g, 0, 0, 0))],
            out_specs=o_spec,
            scratch_shapes=[pltpu.VMEM((GRP, 1), F32), pltpu.VMEM((GRP, 1), F32), pltpu.VMEM((GRP, D), F32)],
        ),
        out_shape=jax.ShapeDtypeStruct((DB, G, GRP, D), F32),
        compiler_params=_params(("parallel", "parallel", "arbitrary"), blk),
        name="nsa_sample_sel",
    )(page_table_flat, idx_flat, slopes, q_s, cache_half, cache_half, rows_s, rows_s, o_c, o_w, gate_logits_s)


def _stack_rows(prompt_rows, sample_rows):
    pad = jnp.zeros((M_PAD - M_PROMPT - M_SAMPLE, prompt_rows.shape[1]), prompt_rows.dtype)
    return jnp.concatenate([prompt_rows, sample_rows.astype(prompt_rows.dtype), pad], axis=0)


def _ffn_block(x, g, w_gate_up, w_down):
    wg = jnp.pad(w_gate_up[:, :D_FF], ((0, 0), (0, D_FF_PAD - D_FF))).astype(BF16)
    wu = jnp.pad(w_gate_up[:, D_FF:], ((0, 0), (0, D_FF_PAD - D_FF))).astype(BF16)
    wd = jnp.pad(w_down, ((0, D_FF_PAD - D_FF), (0, 0))).astype(BF16)
    h = _rmsnorm(x, g, BF16)
    return _matmul(_ffn_up(h, wg, wu), wd, F32, residual=x)


def kernel(x_prompt, x_sample, cache_nsa_kv, page_table, state_win_kv, state_mlstm_C, state_mlstm_n, state_mlstm_m,
           g_mix, g_ffn, g_kv, g_final, w_in_a, b_gate_a, g_head_a, w_out_a, w_qg_b, w_out_b, w_kv,
           pe_cmp, w_cmp1, w_cmp2, w_gate_up, w_down):
    H, DK, DV = MLSTM_HEADS, MLSTM_QK_DIM, MLSTM_V_DIM
    G, D, DB = NSA_KV_HEADS, HEAD_DIM, DEC_BATCH
    hd = NSA_HEADS * D
    n_main = 2 * H * DK + 2 * H * DV
    x = _stack_rows(x_prompt.reshape(M_PROMPT, D_MODEL), x_sample.reshape(M_SAMPLE, D_MODEL))

    c_p, n_p, m_p, c_s, n_s, m_s = [], [], [], [], [], []
    for l in range(N_A):
        h, gates = _rmsnorm_with_gates(x, g_mix[l], w_in_a[l][:, n_main:], b_gate_a[l], True)
        z = _matmul(h, w_in_a[l][:, :n_main].astype(BF16), F32)
        gates_t = gates[:M_PROMPT, :2 * H].T.reshape(2, H, 1, M_PROMPT)
        hs_p, cp, np_, mp = _mlstm_prompt(z, gates_t, g_head_a[l])
        gs = gates[S0:S0 + DB, :2 * H]
        lane_b = lambda a: jnp.broadcast_to(a[:, :, None, None], (DB, H, 1, LANES))
        hs_s, cs, ns, ms = _mlstm_sample(z, lane_b(gs[:, :H]), lane_b(gs[:, H:]), lane_b(state_mlstm_m[l]),
                                         state_mlstm_C[l], state_mlstm_n[l].reshape(DB, H, 1, DK), g_head_a[l])
        hs = _stack_rows(hs_p, hs_s.reshape(DB, H * DV))
        x = _matmul(hs, w_out_a[l].astype(BF16), F32, residual=x)
        x = _ffn_block(x, g_ffn[l], w_gate_up[l], w_down[l])
        c_p.append(cp)
        n_p.append(np_.reshape(BATCH, H, DK))
        m_p.append(mp.reshape(BATCH, H))
        c_s.append(cs)
        n_s.append(ns.reshape(DB, H, DK))
        m_s.append(ms[:, :, 0, 0])

    rows = _matmul(_rmsnorm(x, g_kv, BF16), w_kv.astype(BF16), F32)
    rows_s = rows[S0:S0 + DB].reshape(DB, 6, G, 1, D)
    w1r = _compress_weights(w_cmp1)
    kvc_p = _compress_finish(_compress_proj_prompt(rows, w1r), pe_cmp, w_cmp1, w_cmp2)
    n_pool = cache_nsa_kv.shape[0]
    pt_flat = page_table.reshape(-1).astype(jnp.int32)
    cache2d = cache_nsa_kv.reshape(n_pool, PAGE_SIZE, 4 * G * D)
    kvc_s = _compress_finish(_compress_proj_sample(cache2d, pt_flat, w1r), pe_cmp, w_cmp1, w_cmp2)
    cache_half = cache_nsa_kv.reshape(n_pool * (PAGE_SIZE // SEL_LEN), SEL_LEN, 4 * G * D)
    win2d = state_win_kv.reshape(DB, state_win_kv.shape[1], 2 * G * D)

    slopes = jnp.exp2(-8.0 * jnp.arange(1, NSA_HEADS + 1, dtype=F32) / NSA_HEADS)
    n_chunks_p = SEQ // CMP_STRIDE
    ov_p = _selection_overlap(n_chunks_p - CMP_R + 1, SEQ // SEL_LEN, n_chunks_p, LANES)
    expand = np.zeros((LANES, SEQ), np.float32)
    expand[np.arange(SEQ) // SEL_LEN, np.arange(SEQ)] = 1.0
    expand = jnp.asarray(expand, BF16)
    n_chunks_s = (PAST_LEN + M_SAMPLE // DB) // CMP_STRIDE
    ov_s = _selection_overlap(n_chunks_s - CMP_R + 1, N_SEL_S, n_chunks_s, SEL_PAD_S)

    for l in range(N_B):
        h, gate_logits = _rmsnorm_with_gates(x, g_mix[N_A + l], w_qg_b[l][:, hd:], jnp.zeros((3 * NSA_HEADS,), F32),
                                             False)
        q = _matmul(h, w_qg_b[l][:, :hd].astype(BF16), BF16)
        gl = gate_logits[:, :3 * NSA_HEADS]
        gl_p = gl.reshape(M_PAD, G, 3 * GRP).transpose(1, 0, 2)
        att_p = _nsa_prompt(q, gl_p, kvc_p, rows, slopes, ov_p, expand)
        q_s = q[S0:S0 + DB].reshape(DB, G, GRP, D)
        gl_s = gl[S0:S0 + DB].reshape(DB, G, GRP, 3).transpose(0, 1, 3, 2)[..., None]
        o_c, o_w, idx = _nsa_sample_cmp_win(slopes, q_s, kvc_s, ov_s, win2d, rows_s)
        idx_flat = idx[:, :, 0, :KK_S].reshape(-1)
        att_s = _nsa_sample_sel(pt_flat, idx_flat, slopes, q_s, cache_half, rows_s, o_c, o_w, gl_s)
        att = _stack_rows(att_p, att_s.reshape(DB, hd))
        x = _matmul(att, w_out_b[l].astype(BF16), F32, residual=x)
        x = _ffn_block(x, g_ffn[N_A + l], w_gate_up[N_A + l], w_down[N_A + l])

    y = _rmsnorm(x, g_final, F32)
    wp = min(WINDOW, SEQ)
    rows_p = rows[:M_PROMPT].reshape(BATCH, SEQ, 6, G, D)
    rows_s6 = rows[S0:S0 + DB].reshape(DB, 1, 6, G, D)
    wbuf = state_win_kv.shape[1]
    win_kv_sample = jnp.concatenate([state_win_kv, rows_s6[:, :, 4:]], axis=1)[:, -wbuf:]
    return (y[:M_PROMPT].reshape(BATCH, SEQ, D_MODEL), y[S0:S0 + DB].reshape(DB, 1, D_MODEL),
            rows_p[:, :, :4], rows_p[:, SEQ - wp:, 4:],
            jnp.stack(c_p), jnp.stack(n_p), jnp.stack(m_p),
            rows_s6[:, :, :4], win_kv_sample,
            jnp.stack(c_s), jnp.stack(n_s), jnp.stack(m_s))
```

```python
import functools

import numpy as np
import jax
import jax.numpy as jnp
from jax import lax
from jax.experimental import pallas as pl
from jax.experimental.pallas import tpu as pltpu

D_MODEL = 4096
BATCH = 4
SEQ = 2048
DEPTH = 4
DEC_BATCH = 8
DEC_SEQ = 1
PAST_LEN = 8192
PAGE_SIZE = 128

N_A = DEPTH // 2
N_B = DEPTH - N_A
MLSTM_HEADS = 8
MLSTM_QK_DIM = D_MODEL // 16
MLSTM_V_DIM = D_MODEL // MLSTM_HEADS
GATE_CAP = 15.0
HEAD_DIM = 128
NSA_HEADS = D_MODEL // HEAD_DIM
NSA_KV_HEADS = 4
GRP = NSA_HEADS // NSA_KV_HEADS
CMP_LEN = 32
CMP_STRIDE = 16
CMP_HID = HEAD_DIM
SEL_LEN = 64
SEL_TOP = 16
WINDOW = 512
D_FF = -(-8 * D_MODEL // (3 * 256)) * 256
EPS = 1e-6
NEG = -1e30

F32 = jnp.float32
BF16 = jnp.bfloat16

LANES = 128
SUBLANES = 8
VMEM_BYTES_V7X = 64 * 1024 * 1024


def _round_up(n, m):
    return -(-n // m) * m


def _pick_tile(n, cap, align=LANES):
    best = None
    for t in range(align, min(n, cap) + 1, align):
        if n % t == 0:
            best = t
    assert best is not None, (n, cap, align)
    return best


M_PROMPT = BATCH * SEQ
M_SAMPLE = DEC_BATCH * DEC_SEQ
S0 = M_PROMPT
M_PAD = _round_up(M_PROMPT + M_SAMPLE, LANES)
TM = _pick_tile(M_PAD, 2048)
TR = _pick_tile(M_PAD, 256)
TN_FULLK = 256
MLSTM_L = 128
TQ = 128
TKS = 512
N_GATE_PAD = LANES
D_FF_PAD = _round_up(D_FF, 1024)
CMP_R = CMP_LEN // CMP_STRIDE
SAMPLE_PAGES_PER_STEP = 8
PERM_ROWS = LANES

assert DEC_SEQ == 1 and S0 % 16 == 0
assert SEQ % PERM_ROWS == 0 and PAGE_SIZE % PERM_ROWS == 0
assert (PERM_ROWS // CMP_STRIDE) & (PERM_ROWS // CMP_STRIDE - 1) == 0
assert SEQ % MLSTM_L == 0 and SEQ % TQ == 0 and SEQ % TKS == 0 and TKS % TQ == 0
assert PAST_LEN % PAGE_SIZE == 0 and PAGE_SIZE % SEL_LEN == 0 and PAGE_SIZE % CMP_STRIDE == 0
assert CMP_R == 2 and SEQ // SEL_LEN <= LANES


def _vmem_limit(block_bytes, single_bytes):
    return int(min(single_bytes + 2 * block_bytes + (12 << 20), VMEM_BYTES_V7X - (8 << 20)))


def _params(sem, block_bytes, single_bytes=0):
    return pltpu.CompilerParams(dimension_semantics=sem,
                                vmem_limit_bytes=_vmem_limit(block_bytes, single_bytes))


def _dot(a, b):
    return jnp.dot(a, b, preferred_element_type=F32)


def _dot_nt(a, b):
    return lax.dot_general(a, b, (((1,), (1,)), ((), ())), preferred_element_type=F32)


def _dot_tn(a, b):
    return lax.dot_general(a, b, (((0,), (0,)), ((), ())), preferred_element_type=F32)


def _dot_exact(a, b):
    return jnp.dot(a, b, precision=lax.Precision.HIGHEST, preferred_element_type=F32)


def _log_sigmoid(x):
    return jnp.minimum(x, 0.0) - jnp.log1p(jnp.exp(-jnp.abs(x)))


def _col_form(row, n):
    return jnp.broadcast_to(row, (n, n)).T


def _rms_kernel(x_ref, g_ref, o_ref):
    x = x_ref[...]
    y = x * lax.rsqrt(jnp.mean(x * x, axis=-1, keepdims=True) + EPS) * g_ref[...]
    o_ref[...] = y.astype(o_ref.dtype)


def _rmsnorm(x, g, out_dtype):
    m, d = x.shape
    blk = TR * d * (4 + jnp.dtype(out_dtype).itemsize)
    return pl.pallas_call(
        _rms_kernel,
        grid=(m // TR,),
        in_specs=[pl.BlockSpec((TR, d), lambda i: (i, 0)), pl.BlockSpec((1, d), lambda i: (0, 0))],
        out_specs=pl.BlockSpec((TR, d), lambda i: (i, 0)),
        out_shape=jax.ShapeDtypeStruct((m, d), out_dtype),
        compiler_params=_params(("parallel",), blk),
        name="rmsnorm",
    )(x, g.reshape(1, d))


def _rms_gate_kernel(x_ref, g_ref, w_ref, b_ref, h_ref, o_ref, *, mlstm_gates):
    x = x_ref[...]
    y = x * lax.rsqrt(jnp.mean(x * x, axis=-1, keepdims=True) + EPS) * g_ref[...]
    h_ref[...] = y.astype(h_ref.dtype)
    z = _dot_exact(y, w_ref[...]) + b_ref[...]
    if mlstm_gates:
        gl = GATE_CAP * jnp.tanh(z / GATE_CAP)
        lane = lax.broadcasted_iota(jnp.int32, z.shape, 1)
        z = jnp.where(lane < MLSTM_HEADS, gl, _log_sigmoid(gl))
    o_ref[...] = z


def _rmsnorm_with_gates(x, g, w_gate, b_gate, mlstm_gates):
    m, d = x.shape
    ng = w_gate.shape[1]
    wp = jnp.pad(w_gate, ((0, 0), (0, N_GATE_PAD - ng)))
    bp = jnp.pad(b_gate, (0, N_GATE_PAD - ng)).reshape(1, N_GATE_PAD)
    blk = TR * d * 6 + d * N_GATE_PAD * 4 + TR * N_GATE_PAD * 4
    return pl.pallas_call(
        functools.partial(_rms_gate_kernel, mlstm_gates=mlstm_gates),
        grid=(m // TR,),
        in_specs=[pl.BlockSpec((TR, d), lambda i: (i, 0)),
                  pl.BlockSpec((1, d), lambda i: (0, 0)),
                  pl.BlockSpec((d, N_GATE_PAD), lambda i: (0, 0)),
                  pl.BlockSpec((1, N_GATE_PAD), lambda i: (0, 0))],
        out_specs=[pl.BlockSpec((TR, d), lambda i: (i, 0)),
                   pl.BlockSpec((TR, N_GATE_PAD), lambda i: (i, 0))],
        out_shape=[jax.ShapeDtypeStruct((m, d), BF16), jax.ShapeDtypeStruct((m, N_GATE_PAD), F32)],
        compiler_params=_params(("parallel",), blk),
        name="rmsnorm_gates",
    )(x, g.reshape(1, d), wp, bp)


def _mm_kernel(*refs, nk, has_res):
    if has_res:
        x_ref, w_ref, r_ref, o_ref, acc_ref = refs
    else:
        x_ref, w_ref, o_ref, acc_ref = refs
    k = pl.program_id(2)

    @pl.when(k == 0)
    def _():
        acc_ref[...] = jnp.zeros_like(acc_ref)

    acc_ref[...] += _dot(x_ref[...], w_ref[...])

    @pl.when(k == nk - 1)
    def _():
        r = acc_ref[...]
        if has_res:
            r = r + r_ref[...]
        o_ref[...] = r.astype(o_ref.dtype)


def _matmul_kblocked(x, w, layer, out_dtype, residual=None):
    m, kd = x.shape
    n = w.shape[2]
    tn = _pick_tile(n, 512)
    tk = _pick_tile(kd, 1024)
    nk = kd // tk
    osz = jnp.dtype(out_dtype).itemsize
    in_specs = [pl.BlockSpec((TM, tk), lambda i, j, k: (i, k)),
                pl.BlockSpec((None, tk, tn), lambda i, j, k: (layer, k, j))]
    args = [x, w]
    blk = TM * tk * 2 + tk * tn * 2 + TM * tn * (osz + 4)
    if residual is not None:
        in_specs.append(pl.BlockSpec((TM, tn), lambda i, j, k: (i, j)))
        args.append(residual)
        blk += TM * tn * 4
    return pl.pallas_call(
        functools.partial(_mm_kernel, nk=nk, has_res=residual is not None),
        grid=(m // TM, n // tn, nk),
        in_specs=in_specs,
        out_specs=pl.BlockSpec((TM, tn), lambda i, j, k: (i, j)),
        out_shape=jax.ShapeDtypeStruct((m, n), out_dtype),
        scratch_shapes=[pltpu.VMEM((TM, tn), F32)],
        compiler_params=_params(("parallel", "parallel", "arbitrary"), blk),
        name="matmul_kblocked",
    )(*args)


def _mm_fullk_kernel(*refs, has_res):
    x_ref, w_ref, o_ref = refs[0], refs[1], refs[-1]
    r = _dot(x_ref[...], w_ref[...].astype(BF16))
    if has_res:
        r = r + refs[2][...]
    o_ref[...] = r.astype(o_ref.dtype)


def _row_tile_spec(kd):
    return pl.BlockSpec((TM, kd), lambda i, j: (i, 0), pipeline_mode=pl.Buffered(1))


def _matmul_fullk(x, w, layer, n_cols, out_dtype, residual=None):
    m, kd = x.shape
    tn = TN_FULLK
    assert n_cols % tn == 0 and w.shape[1] == kd and w.dtype == F32
    osz = jnp.dtype(out_dtype).itemsize
    in_specs = [_row_tile_spec(kd), pl.BlockSpec((None, kd, tn), lambda i, j: (layer, 0, j))]
    args = [x, w]
    streamed = kd * tn * 4 + TM * tn * osz
    if residual is not None:
        in_specs.append(pl.BlockSpec((TM, tn), lambda i, j: (i, j)))
        args.append(residual)
        streamed += TM * tn * 4
    return pl.pallas_call(
        functools.partial(_mm_fullk_kernel, has_res=residual is not None),
        grid=(m // TM, n_cols // tn),
        in_specs=in_specs,
        out_specs=pl.BlockSpec((TM, tn), lambda i, j: (i, j)),
        out_shape=jax.ShapeDtypeStruct((m, n_cols), out_dtype),
        compiler_params=_params(("parallel", "arbitrary"), streamed, single_bytes=TM * kd * 2),
        name="matmul_fullk",
    )(*args)


def _ffn_up_kernel(x_ref, wg_ref, wu_ref, o_ref, *, n_real):
    j = pl.program_id(1)

    @pl.when(j < n_real)
    def _():
        x = x_ref[...]
        gate = _dot(x, wg_ref[...].astype(BF16))
        up = _dot(x, wu_ref[...].astype(BF16))
        o_ref[...] = (jax.nn.silu(gate) * up).astype(o_ref.dtype)

    @pl.when(j >= n_real)
    def _():
        o_ref[...] = jnp.zeros_like(o_ref)


def _ffn_up(x, w_gate_up, layer):
    m, kd = x.shape
    tn = TN_FULLK
    assert D_FF % tn == 0 and D_FF_PAD % tn == 0
    n_real = D_FF // tn
    col = lambda j: jnp.minimum(j, n_real - 1)
    streamed = 2 * kd * tn * 4 + TM * tn * 2
    return pl.pallas_call(
        functools.partial(_ffn_up_kernel, n_real=n_real),
        grid=(m // TM, D_FF_PAD // tn),
        in_specs=[_row_tile_spec(kd),
                  pl.BlockSpec((None, kd, tn), lambda i, j: (layer, 0, col(j))),
                  pl.BlockSpec((None, kd, tn), lambda i, j: (layer, 0, n_real + col(j)))],
        out_specs=pl.BlockSpec((TM, tn), lambda i, j: (i, j)),
        out_shape=jax.ShapeDtypeStruct((m, D_FF_PAD), BF16),
        compiler_params=_params(("parallel", "arbitrary"), streamed, single_bytes=TM * kd * 2),
        name="ffn_up",
    )(x, w_gate_up, w_gate_up)


def _mlstm_prompt_kernel(q_ref, k_ref, v_ref, o_ref, g_ref, gh_ref, hs_ref, c_ref, n_ref, m_ref):
    L = MLSTM_L

    @pl.when(pl.program_id(2) == 0)
    def _():
        c_ref[...] = jnp.zeros_like(c_ref)
        n_ref[...] = jnp.zeros_like(n_ref)
        m_ref[...] = jnp.zeros_like(m_ref)

    ig_row = g_ref[0, 0]
    lf_row = g_ref[1, 0]
    m_prev = m_ref[0, 0]
    t_idx = lax.broadcasted_iota(jnp.int32, (L, L), 0)
    s_idx = lax.broadcasted_iota(jnp.int32, (L, L), 1)
    lf_c = _col_form(lf_row, L)
    b_row = jnp.sum(jnp.where(t_idx <= s_idx, lf_c, 0.0), axis=0, keepdims=True)
    b_r = jnp.broadcast_to(b_row, (L, L))
    b_c = b_r.T
    ig_r = jnp.broadcast_to(ig_row, (L, L))
    log_d = jnp.where(s_idx <= t_idx, b_c - b_r + ig_r, -jnp.inf)
    b_col = b_c[:, 0:1]
    inter = b_col + m_prev
    m_t = jnp.maximum(jnp.max(log_d, axis=1, keepdims=True), inter)
    d = jnp.exp(log_d - m_t)
    w_inter = jnp.exp(inter - m_t)

    q = q_ref[...] * (MLSTM_QK_DIM ** -0.5)
    k = k_ref[...]
    qb = q.astype(BF16)
    kb = k.astype(BF16)
    vb = v_ref[...].astype(BF16)
    s = _dot_nt(qb, kb) * d
    c_prev = c_ref[0, 0]
    n_prev = n_ref[0, 0]
    num = w_inter * _dot(qb, c_prev.astype(BF16)) + _dot(s.astype(BF16), vb)
    den = w_inter * jnp.sum(q * n_prev, axis=1, keepdims=True) + jnp.sum(s, axis=1, keepdims=True)
    h = num / jnp.maximum(jnp.abs(den), jnp.exp(-m_t))
    hn = h * lax.rsqrt(jnp.mean(h * h, axis=1, keepdims=True) + EPS)
    hs_ref[...] = (hn * gh_ref[...] * jax.nn.sigmoid(o_ref[...])).astype(hs_ref.dtype)

    m_new = m_t[L - 1:L, :]
    b_last = b_row[:, L - 1:L]
    ig_col = ig_r.T[:, 0:1]
    decay = jnp.exp(b_last - b_col + ig_col - m_new)
    carry = jnp.exp(b_last + m_prev - m_new)
    kd = k * decay
    c_ref[0, 0] = carry * c_prev + _dot_tn(kd.astype(BF16), vb)
    n_ref[0, 0] = carry * n_prev + jnp.sum(kd, axis=0, keepdims=True)
    m_ref[0, 0] = m_new


def _mlstm_prompt(z, gates_t, g_head):
    H, DK, DV, L = MLSTM_HEADS, MLSTM_QK_DIM, MLSTM_V_DIM, MLSTM_L
    nc = SEQ // L
    v0 = (2 * H * DK) // DV
    row = lambda b, h, c: b * nc + c
    blk = L * (2 * DK + 2 * DV) * 4 + L * DV * 2 + DK * DV * 4
    return pl.pallas_call(
        _mlstm_prompt_kernel,
        grid=(BATCH, H, nc),
        in_specs=[pl.BlockSpec((L, DK), lambda b, h, c: (row(b, h, c), h)),
                  pl.BlockSpec((L, DK), lambda b, h, c: (row(b, h, c), H + h)),
                  pl.BlockSpec((L, DV), lambda b, h, c: (row(b, h, c), v0 + h)),
                  pl.BlockSpec((L, DV), lambda b, h, c: (row(b, h, c), v0 + H + h)),
                  pl.BlockSpec((2, 1, 1, L), lambda b, h, c: (0, h, 0, row(b, h, c))),
                  pl.BlockSpec((1, DV), lambda b, h, c: (0, h))],
        out_specs=[pl.BlockSpec((L, DV), lambda b, h, c: (row(b, h, c), h)),
                   pl.BlockSpec((1, 1, DK, DV), lambda b, h, c: (b, h, 0, 0)),
                   pl.BlockSpec((1, 1, 1, DK), lambda b, h, c: (b, h, 0, 0)),
                   pl.BlockSpec((1, 1, 1, 1), lambda b, h, c: (b, h, 0, 0))],
        out_shape=[jax.ShapeDtypeStruct((M_PROMPT, H * DV), BF16),
                   jax.ShapeDtypeStruct((BATCH, H, DK, DV), F32),
                   jax.ShapeDtypeStruct((BATCH, H, 1, DK), F32),
                   jax.ShapeDtypeStruct((BATCH, H, 1, 1), F32)],
        compiler_params=_params(("parallel", "parallel", "arbitrary"), blk),
        name="mlstm_prompt",
    )(z, z, z, z, gates_t, g_head.reshape(1, H * DV))


def _mlstm_sample_kernel(q_ref, k_ref, v_ref, o_ref, ig_ref, lf_ref, m0_ref, c0_ref, n0_ref, gh_ref,
                         hs_ref, c_ref, n_ref, m_ref):
    DK = MLSTM_QK_DIM
    b = pl.program_id(0)
    q = q_ref[pl.ds(b, 1), :] * (DK ** -0.5)
    k = k_ref[pl.ds(b, 1), :]
    v = v_ref[pl.ds(b, 1), :]
    og = o_ref[pl.ds(b, 1), :]
    ig = ig_ref[0, 0]
    lf = lf_ref[0, 0]
    m0 = m0_ref[0, 0]
    m_t = jnp.maximum(ig, lf + m0)
    d = jnp.exp(ig - m_t)[:, 0:1]
    w = jnp.exp(lf + m0 - m_t)[:, 0:1]
    c0 = c0_ref[0, 0]
    n0 = n0_ref[0, 0]
    q_col = _col_form(q, DK)[:, 0:1]
    k_col = _col_form(k, DK)[:, 0:1]
    s = jnp.sum(q * k, axis=1, keepdims=True) * d
    num = w * jnp.sum(q_col * c0, axis=0, keepdims=True) + s * v
    den = w * jnp.sum(q * n0, axis=1, keepdims=True) + s
    h = num / jnp.maximum(jnp.abs(den), jnp.exp(-m_t)[:, 0:1])
    hn = h * lax.rsqrt(jnp.mean(h * h, axis=1, keepdims=True) + EPS)
    hs_ref[0] = hn * gh_ref[...] * jax.nn.sigmoid(og)
    c_ref[0, 0] = w * c0 + d * (k_col * v)
    n_ref[0, 0] = w * n0 + d * k
    m_ref[0, 0] = m_t


def _mlstm_sample(z, ig_b, lf_b, m_b, c0, n0, g_head):
    H, DK, DV = MLSTM_HEADS, MLSTM_QK_DIM, MLSTM_V_DIM
    DB = DEC_BATCH
    r0 = S0 // SUBLANES
    v0 = (2 * H * DK) // DV
    sc = pl.BlockSpec((1, 1, 1, LANES), lambda b, h: (b, h, 0, 0))
    blk = 2 * DK * DV * 4 + SUBLANES * (2 * DK + 2 * DV) * 4
    return pl.pallas_call(
        _mlstm_sample_kernel,
        grid=(DB, H),
        in_specs=[pl.BlockSpec((SUBLANES, DK), lambda b, h: (r0, h)),
                  pl.BlockSpec((SUBLANES, DK), lambda b, h: (r0, H + h)),
                  pl.BlockSpec((SUBLANES, DV), lambda b, h: (r0, v0 + h)),
                  pl.BlockSpec((SUBLANES, DV), lambda b, h: (r0, v0 + H + h)),
                  sc, sc, sc,
                  pl.BlockSpec((1, 1, DK, DV), lambda b, h: (b, h, 0, 0)),
                  pl.BlockSpec((1, 1, 1, DK), lambda b, h: (b, h, 0, 0)),
                  pl.BlockSpec((1, DV), lambda b, h: (0, h))],
        out_specs=[pl.BlockSpec((1, 1, DV), lambda b, h: (b, 0, h)),
                   pl.BlockSpec((1, 1, DK, DV), lambda b, h: (b, h, 0, 0)),
                   pl.BlockSpec((1, 1, 1, DK), lambda b, h: (b, h, 0, 0)),
                   sc],
        out_shape=[jax.ShapeDtypeStruct((DB, 1, H * DV), F32),
                   jax.ShapeDtypeStruct((DB, H, DK, DV), F32),
                   jax.ShapeDtypeStruct((DB, H, 1, DK), F32),
                   jax.ShapeDtypeStruct((DB, H, 1, LANES), F32)],
        compiler_params=_params(("parallel", "parallel"), blk),
        name="mlstm_sample",
    )(z, z, z, z, ig_b, lf_b, m_b, c0, n0, g_head.reshape(1, H * DV))


def _compress_proj_kernel(*refs, n_in, groups_per_in, lead):
    x_refs = refs[-(n_in + 3):-3]
    w_ref, p_ref, xp_scr = refs[-3], refs[-2], refs[-1]
    G, D = NSA_KV_HEADS, HEAD_DIM
    cpg = PERM_ROWS // CMP_STRIDE
    nch = n_in * groups_per_in * cpg
    i_idx = lax.broadcasted_iota(jnp.int32, (PERM_ROWS, PERM_ROWS), 0)
    j_idx = lax.broadcasted_iota(jnp.int32, (PERM_ROWS, PERM_ROWS), 1)
    src_row = (i_idx & (cpg - 1)) * CMP_STRIDE + (i_idx >> (cpg.bit_length() - 1))
    perm = (j_idx == src_row).astype(BF16)
    for i, xr in enumerate(x_refs):
        for p in range(groups_per_in):
            rows = pl.ds(p * PERM_ROWS, PERM_ROWS)
            xg = xr[0, rows, :] if lead else xr[rows, :]
            xg = _dot(perm, xg.astype(BF16))
            ch0 = (i * groups_per_in + p) * cpg
            for c in range(CMP_STRIDE):
                xp_scr[c, ch0:ch0 + cpg, :] = xg[c * cpg:(c + 1) * cpg, :]
    for s in range(2):
        acc = jnp.zeros((G * nch, CMP_R * CMP_HID), F32)
        for c in range(CMP_STRIDE):
            xs = jnp.concatenate([xp_scr[c, :, (s * G + g) * D:(s * G + g + 1) * D] for g in range(G)], axis=0)
            acc = acc + _dot(xs.astype(BF16), w_ref[s, c])
        for g in range(G):
            p_ref[0, s, g] = acc[g * nch:(g + 1) * nch]


def _compress_weights(w_cmp1):
    w = w_cmp1.reshape(2, CMP_R, CMP_STRIDE, HEAD_DIM, CMP_HID)
    return w.transpose(0, 2, 3, 1, 4).reshape(2, CMP_STRIDE, HEAD_DIM, CMP_R * CMP_HID).astype(BF16)


def _compress_proj_prompt(rows, w1r):
    G, D = NSA_KV_HEADS, HEAD_DIM
    nch = SEQ // CMP_STRIDE
    pw = CMP_R * CMP_HID
    blk = SEQ * 2 * G * D * 4 + w1r.size * 2 + 2 * G * nch * pw * 4 + SEQ * 2 * G * D * 4
    return pl.pallas_call(
        functools.partial(_compress_proj_kernel, n_in=1, groups_per_in=SEQ // PERM_ROWS, lead=False),
        grid=(BATCH,),
        in_specs=[pl.BlockSpec((SEQ, 2 * G * D), lambda b: (b, 0)),
                  pl.BlockSpec(w1r.shape, lambda b: (0, 0, 0, 0))],
        out_specs=pl.BlockSpec((1, 2, G, nch, pw), lambda b: (b, 0, 0, 0, 0)),
        out_shape=jax.ShapeDtypeStruct((BATCH, 2, G, nch, pw), F32),
        scratch_shapes=[pltpu.VMEM((CMP_STRIDE, nch, 2 * G * D), F32)],
        compiler_params=_params(("parallel",), blk),
        name="compress_proj_prompt",
    )(rows, w1r)


def _compress_proj_sample(cache2d, page_table_flat, w1r):
    G, D = NSA_KV_HEADS, HEAD_DIM
    n_pages = PAST_LEN // PAGE_SIZE
    pps = SAMPLE_PAGES_PER_STEP
    assert n_pages % pps == 0
    cpi = PAGE_SIZE // CMP_STRIDE
    nch_step = pps * cpi
    pw = CMP_R * CMP_HID

    def page_map(b, j, pt, *, i):
        return (pt[b * n_pages + j * pps + i], 0, 0)

    in_specs = [pl.BlockSpec((1, PAGE_SIZE, 2 * G * D), functools.partial(page_map, i=i)) for i in range(pps)]
    in_specs.append(pl.BlockSpec(w1r.shape, lambda b, j, pt: (0, 0, 0, 0)))
    blk = 2 * pps * PAGE_SIZE * 2 * G * D * 4 + w1r.size * 2 + 2 * G * nch_step * pw * 4
    return pl.pallas_call(
        functools.partial(_compress_proj_kernel, n_in=pps, groups_per_in=PAGE_SIZE // PERM_ROWS, lead=True),
        grid_spec=pltpu.PrefetchScalarGridSpec(
            num_scalar_prefetch=1,
            grid=(DEC_BATCH, n_pages // pps),
            in_specs=in_specs,
            out_specs=pl.BlockSpec((1, 2, G, nch_step, pw), lambda b, j, pt: (b, 0, 0, j, 0)),
            scratch_shapes=[pltpu.VMEM((CMP_STRIDE, nch_step, 2 * G * D), F32)],
        ),
        out_shape=jax.ShapeDtypeStruct((DEC_BATCH, 2, G, n_pages * cpi, pw), F32),
        compiler_params=_params(("parallel", "arbitrary"), blk),
        name="compress_proj_sample",
    )(page_table_flat, *([cache2d] * pps), w1r)


def _compress_finish_kernel(p_ref, pe_ref, w1_ref, w2_ref, o_ref):
    nch = p_ref.shape[3]
    pe = jnp.broadcast_to(pe_ref[0], (SUBLANES, pe_ref.shape[2])).astype(BF16)
    pe_term = _dot(pe, w1_ref[0].astype(BF16))[0:1]
    w2 = w2_ref[0].astype(BF16)
    for g in range(NSA_KV_HEADS):
        p = p_ref[0, 0, g]
        nxt = pltpu.roll(p[:, CMP_HID:], shift=nch - 1, axis=0)
        hidden = p[:, :CMP_HID] + nxt + pe_term
        o_ref[0, 0, g] = _dot(jax.nn.gelu(hidden).astype(BF16), w2)


def _compress_finish(p, pe_cmp, w_cmp1, w_cmp2):
    nb, _, G, nch, pw = p.shape
    pe = pe_cmp.reshape(2, 1, CMP_LEN * HEAD_DIM)
    blk = G * nch * (pw + HEAD_DIM) * 4 + CMP_LEN * HEAD_DIM * (CMP_HID + 1) * 4
    return pl.pallas_call(
        _compress_finish_kernel,
        grid=(nb, 2),
        in_specs=[pl.BlockSpec((1, 1, G, nch, pw), lambda b, s: (b, s, 0, 0, 0)),
                  pl.BlockSpec((1, 1, CMP_LEN * HEAD_DIM), lambda b, s: (s, 0, 0)),
                  pl.BlockSpec((1, CMP_LEN * HEAD_DIM, CMP_HID), lambda b, s: (s, 0, 0)),
                  pl.BlockSpec((1, CMP_HID, HEAD_DIM), lambda b, s: (s, 0, 0))],
        out_specs=pl.BlockSpec((1, 1, G, nch, HEAD_DIM), lambda b, s: (b, s, 0, 0, 0)),
        out_shape=jax.ShapeDtypeStruct((nb, 2, G, nch, HEAD_DIM), F32),
        compiler_params=_params(("parallel", "parallel"), blk),
        name="compress_finish",
    )(p, pe, w_cmp1, w_cmp2)


def _selection_overlap(n_cmp, n_sel, rows, cols):
    start = np.arange(n_cmp)[:, None] * CMP_STRIDE
    end = start + CMP_LEN - 1
    j = np.arange(n_sel)[None, :]
    ov = ((start <= (j + 1) * SEL_LEN - 1) & (end >= j * SEL_LEN)).astype(np.float32)
    out = np.zeros((rows, cols), np.float32)
    out[:n_cmp, :n_sel] = ov
    return jnp.asarray(out)


def _masked_softmax(s, valid):
    sm = jnp.where(valid, s, NEG)
    e = jnp.exp(sm - jnp.max(sm, axis=-1, keepdims=True))
    return jnp.where(valid, e / jnp.sum(e, axis=-1, keepdims=True), 0.0)


def _flash_step(carry, s, valid, vb):
    m, l, acc = carry
    sm = jnp.where(valid, s, NEG)
    m_new = jnp.maximum(m, jnp.max(sm, axis=-1, keepdims=True))
    alpha = jnp.exp(m - m_new)
    e = jnp.exp(sm - m_new)
    l = alpha * l + jnp.sum(e, axis=-1, keepdims=True)
    acc = alpha * acc + _dot(jnp.where(valid, e, 0.0).astype(BF16), vb)
    return m_new, l, acc


def _flash_one(carry, s_new, v_new):
    m, l, acc = carry
    m_new = jnp.maximum(m, s_new)
    alpha = jnp.exp(m - m_new)
    e = jnp.exp(s_new - m_new)
    return m_new, alpha * l + e, alpha * acc + e * v_new


def _flash_init(rows):
    return (jnp.full((rows, 1), -jnp.inf, F32), jnp.zeros((rows, 1), F32), jnp.zeros((rows, HEAD_DIM), F32))


def _top_rank(impm_c, impm_r, jp_idx, j_idx, axis):
    before = (impm_c > impm_r) | ((impm_c == impm_r) & (jp_idx < j_idx))
    return jnp.sum(before.astype(jnp.int32), axis=axis, keepdims=True)


FEAT_SIG = 96
FEAT_SIG_FULL = FEAT_SIG + 9
MASK_BIG = 2.0 ** 100
POS_RADIX = 256
LOG2E = 1.4426950408889634
assert SEQ // SEL_LEN <= FEAT_SIG and SEQ >= WINDOW + TQ


def _key_features(pos, block):
    n = pos.shape[0]
    kf = np.zeros((n, LANES), np.float32)
    if block is not None:
        kf[np.arange(n), block] = -MASK_BIG
    kf[:, FEAT_SIG:FEAT_SIG + 3] = (pos % POS_RADIX)[:, None]
    kf[:, FEAT_SIG + 3:FEAT_SIG + 6] = (pos // POS_RADIX)[:, None]
    kf[:, FEAT_SIG + 6:FEAT_SIG + 9] = 1.0
    return jnp.asarray(kf, BF16)


def _sigma_table():
    slopes = np.exp2(-8.0 * np.arange(1, NSA_HEADS + 1, dtype=np.float32) / NSA_HEADS).astype(np.float32)
    sigma = (slopes / np.float32(HEAD_DIM ** -0.5)).astype(np.float32)
    bf = lambda a: np.asarray(np.asarray(a, dtype=BF16), dtype=np.float32)
    s1 = bf(sigma)
    s2 = bf(sigma - s1)
    s3 = bf(sigma - s1 - s2)
    tab = np.zeros((NSA_HEADS, LANES), np.float32)
    for i, s in enumerate((s1, s2, s3)):
        tab[:, FEAT_SIG + i] = s
        tab[:, FEAT_SIG + 3 + i] = s * POS_RADIX
    tab[:, FEAT_SIG_FULL] = sigma
    return jnp.asarray(tab)


def _nsa_prompt_kernel(q_ref, gl_ref, sig_ref, kc_ref, vc_ref, ks_ref, vs_ref, kw_ref, vw_ref,
                       ov_ref, kfc_ref, kfs_ref, kfw_ref, out_ref,
                       qa_scr, kca_scr, vcb_scr, ksa_scr, vsb_scr, kwa_scr, vwb_scr, m_scr, l_scr, acc_scr):
    qi = pl.program_id(2)
    D = HEAD_DIM
    c2 = (D ** -0.5) * LOG2E
    n_sel = SEQ // SEL_LEN
    qs = qi * TQ
    tpos = qs + lax.broadcasted_iota(jnp.int32, (TQ, 1), 0)
    tposf = tpos.astype(F32)
    lane = lax.broadcasted_iota(jnp.int32, (1, LANES), 1)
    heads = [slice(r * TQ, (r + 1) * TQ) for r in range(GRP)]

    @pl.when(qi == 0)
    def _():
        kca_scr[:, 0:D] = kc_ref[0, 0, 0].astype(BF16)
        kca_scr[:, D:2 * D] = kfc_ref[...]
        vcb_scr[...] = vc_ref[0, 0, 0].astype(BF16)
        ksa_scr[:, 0:D] = ks_ref[...].astype(BF16)
        ksa_scr[:, D:2 * D] = kfs_ref[...]
        vsb_scr[...] = vs_ref[...].astype(BF16)
        kwa_scr[:, 0:D] = kw_ref[...].astype(BF16)
        kwa_scr[:, D:2 * D] = kfw_ref[...]
        vwb_scr[...] = vw_ref[...].astype(BF16)

    def write_features(nsel):
        for r in range(GRP):
            sig_row = sig_ref[r:r + 1, :]
            c = -sig_row[:, FEAT_SIG_FULL:FEAT_SIG_FULL + 1] * tposf
            c_hi = c.astype(BF16).astype(F32)
            c_mid = (c - c_hi).astype(BF16).astype(F32)
            c_lo = c - c_hi - c_mid
            f = jnp.where(lane == FEAT_SIG + 6, c_hi,
                          jnp.where(lane == FEAT_SIG + 7, c_mid, jnp.where(lane == FEAT_SIG + 8, c_lo, sig_row)))
            if nsel is not None:
                f = jnp.where(lane < n_sel, nsel, f)
            qa_scr[heads[r], D:2 * D] = f.astype(BF16)

    for r in range(GRP):
        qa_scr[heads[r], 0:D] = q_ref[:, r * D:(r + 1) * D]
    write_features(None)

    ncp = kca_scr.shape[0]
    cpos = lax.broadcasted_iota(jnp.int32, (1, ncp), 1) * CMP_STRIDE + (CMP_LEN - 1)
    valid_c = cpos <= tpos
    s_c = _dot_nt(qa_scr[...], kca_scr[...])
    p_sum = jnp.zeros((TQ, ncp), F32)
    p_c = []
    for r in range(GRP):
        sm = jnp.where(valid_c, s_c[heads[r]], NEG)
        e = jnp.exp2((sm - jnp.max(sm, axis=-1, keepdims=True)) * c2)
        p = jnp.where(valid_c, e / jnp.sum(e, axis=-1, keepdims=True), 0.0)
        p_c.append(p.astype(BF16))
        p_sum = p_sum + p
    o_c = _dot(jnp.concatenate(p_c, axis=0), vcb_scr[...])
    imp = _dot_exact(p_sum, ov_ref[...])

    cur = tpos // SEL_LEN
    forced = (lane == 0) | (lane == cur) | (lane == cur - 1)
    impm = jnp.where(forced, jnp.inf, jnp.where(lane <= cur, imp, -jnp.inf))
    rank = jnp.zeros((TQ, LANES), jnp.int32)
    for jp in range(n_sel):
        cj = impm[:, jp:jp + 1]
        rank = rank + ((cj > impm) | ((cj == impm) & (jp < lane))).astype(jnp.int32)
    selected = (rank < min(SEL_TOP, n_sel)) & (lane <= cur)
    write_features(jnp.where(selected, 0.0, 1.0))

    m_scr[...] = jnp.full(m_scr.shape, -jnp.inf, F32)
    l_scr[...] = jnp.zeros(l_scr.shape, F32)
    acc_scr[...] = jnp.zeros(acc_scr.shape, F32)

    def sel_tile(kt, bias):
        k0 = pl.multiple_of(kt * TKS, TKS)
        s_all = _dot_nt(qa_scr[...], ksa_scr[pl.ds(k0, TKS), :])
        p_all = []
        for r in range(GRP):
            s = s_all[heads[r]]
            if bias is not None:
                s = s + bias
            m_old = m_scr[heads[r]]
            m_new = jnp.maximum(m_old, jnp.max(s, axis=-1, keepdims=True))
            alpha = jnp.exp2((m_old - m_new) * c2)
            e = jnp.exp2((s - m_new) * c2)
            m_scr[heads[r]] = m_new
            l_scr[heads[r]] = alpha * l_scr[heads[r]] + jnp.sum(e, axis=-1, keepdims=True)
            acc_scr[heads[r]] = alpha * acc_scr[heads[r]]
            p_all.append(e.astype(BF16))
        acc_scr[...] += _dot(jnp.concatenate(p_all, axis=0), vsb_scr[pl.ds(k0, TKS), :])

    n_past = qs // TKS

    def past_tile(kt, carry):
        sel_tile(kt, None)
        return carry

    lax.fori_loop(0, n_past, past_tile, 0)
    spos_d = n_past * TKS + lax.broadcasted_iota(jnp.int32, (1, TKS), 1)
    sel_tile(n_past, jnp.where(spos_d <= tpos, 0.0, -MASK_BIG))
    o_s = acc_scr[...] / l_scr[...]

    wk = WINDOW + TQ
    w0 = pl.multiple_of(jnp.maximum(qs - WINDOW, 0), TQ)
    dist_w = tpos - (w0 + lax.broadcasted_iota(jnp.int32, (1, wk), 1))
    bias_w = jnp.where((dist_w >= 0) & (dist_w <= WINDOW), 0.0, -MASK_BIG)
    s_w = _dot_nt(qa_scr[...], kwa_scr[pl.ds(w0, wk), :])
    p_w, l_w = [], []
    for r in range(GRP):
        s = s_w[heads[r]] + bias_w
        e = jnp.exp2((s - jnp.max(s, axis=-1, keepdims=True)) * c2)
        l_w.append(jnp.sum(e, axis=-1, keepdims=True))
        p_w.append(e.astype(BF16))
    o_w = _dot(jnp.concatenate(p_w, axis=0), vwb_scr[pl.ds(w0, wk), :]) / jnp.concatenate(l_w, axis=0)

    for r in range(GRP):
        gates = jax.nn.sigmoid(gl_ref[0, :, 3 * r:3 * r + 3])
        out = gates[:, 0:1] * o_c[heads[r]] + gates[:, 1:2] * o_s[heads[r]] + gates[:, 2:3] * o_w[heads[r]]
        out_ref[:, r * D:(r + 1) * D] = out.astype(out_ref.dtype)


def _nsa_prompt(q, gate_logits, kvc, rows, ov):
    G, D = NSA_KV_HEADS, HEAD_DIM
    nq = SEQ // TQ
    ncp = kvc.shape[3]
    R = GRP * TQ
    spos = np.arange(SEQ)
    kf_cmp = _key_features(np.arange(ncp) * CMP_STRIDE + (CMP_LEN - 1), None)
    kf_sel = _key_features(spos, spos // SEL_LEN)
    kf_win = _key_features(spos, None)
    row = lambda b, g, qi: b * nq + qi
    kv_spec = lambda slot: pl.BlockSpec((SEQ, D), lambda b, g, qi: (b, slot * G + g))
    const = lambda a: pl.BlockSpec(a.shape, lambda b, g, qi: (0, 0))
    blk = (TQ * GRP * D * 4 + 2 * ncp * D * 4 + 4 * SEQ * D * 4 + 3 * SEQ * D * 2 + SEQ * D * 6 + R * D * 16
           + R * (TKS + WINDOW + TQ) * 6)
    return pl.pallas_call(
        _nsa_prompt_kernel,
        grid=(BATCH, G, nq),
        in_specs=[pl.BlockSpec((TQ, GRP * D), lambda b, g, qi: (row(b, g, qi), g)),
                  pl.BlockSpec((1, TQ, 3 * GRP), lambda b, g, qi: (g, row(b, g, qi), 0)),
                  pl.BlockSpec((GRP, LANES), lambda b, g, qi: (g, 0)),
                  pl.BlockSpec((1, 1, 1, ncp, D), lambda b, g, qi: (b, 0, g, 0, 0)),
                  pl.BlockSpec((1, 1, 1, ncp, D), lambda b, g, qi: (b, 1, g, 0, 0)),
                  kv_spec(2), kv_spec(3), kv_spec(4), kv_spec(5),
                  const(ov), const(kf_cmp), const(kf_sel), const(kf_win)],
        out_specs=pl.BlockSpec((TQ, GRP * D), lambda b, g, qi: (row(b, g, qi), g)),
        out_shape=jax.ShapeDtypeStruct((M_PROMPT, NSA_HEADS * D), BF16),
        scratch_shapes=[pltpu.VMEM((R, 2 * D), BF16),
                        pltpu.VMEM((ncp, 2 * D), BF16), pltpu.VMEM((ncp, D), BF16),
                        pltpu.VMEM((SEQ, 2 * D), BF16), pltpu.VMEM((SEQ, D), BF16),
                        pltpu.VMEM((SEQ, 2 * D), BF16), pltpu.VMEM((SEQ, D), BF16),
                        pltpu.VMEM((R, 1), F32), pltpu.VMEM((R, 1), F32), pltpu.VMEM((R, D), F32)],
        compiler_params=_params(("parallel", "parallel", "arbitrary"), blk),
        name="nsa_prompt",
    )(q, gate_logits, _sigma_table(), kvc, kvc, rows, rows, rows, rows, ov, kf_cmp, kf_sel, kf_win)


T_SAMPLE = PAST_LEN
N_SEL_S = -(-(PAST_LEN + 1) // SEL_LEN)
NB_PAST = PAST_LEN // SEL_LEN
KK_S = min(SEL_TOP, N_SEL_S)
SEL_PAD_S = _round_up(N_SEL_S, LANES)
assert PAST_LEN % SEL_LEN == 0 and PAST_LEN % CMP_STRIDE == 0


def _slope_col(slopes_ref, g):
    r_idx = lax.broadcasted_iota(jnp.int32, (GRP, 1), 0)
    col = jnp.zeros((GRP, 1), F32)
    for r in range(GRP):
        col = jnp.where(r_idx == r, slopes_ref[g * GRP + r], col)
    return col


def _nsa_sample_cmp_win_kernel(slopes_ref, q_ref, kc_ref, vc_ref, ov_ref, kw_ref, vw_ref, kwn_ref, vwn_ref,
                               oc_ref, ow_ref, idx_ref):
    g = pl.program_id(1)
    scale = HEAD_DIM ** -0.5
    q = q_ref[0, 0]
    slope = _slope_col(slopes_ref, g)

    kcb = kc_ref[0, 0, 0].astype(BF16)
    ncp = kcb.shape[0]
    cpos = lax.broadcasted_iota(jnp.int32, (1, ncp), 1) * CMP_STRIDE + (CMP_LEN - 1)
    dist_c = T_SAMPLE - cpos
    p_c = _masked_softmax(_dot_nt(q, kcb) * scale - slope * dist_c.astype(F32), dist_c >= 0)
    oc_ref[0, 0] = _dot(p_c.astype(BF16), vc_ref[0, 0, 0].astype(BF16))
    p_sum = jnp.broadcast_to(jnp.sum(p_c, axis=0, keepdims=True), (SUBLANES, ncp))
    imp = _dot_exact(p_sum, ov_ref[...])[0:1]

    NP = SEL_PAD_S
    cur = T_SAMPLE // SEL_LEN
    j_row = lax.broadcasted_iota(jnp.int32, (1, NP), 1)
    forced = (j_row == 0) | (j_row == cur) | (j_row == cur - 1)
    impm = jnp.where(forced, jnp.inf, jnp.where(j_row <= cur, imp, -jnp.inf))
    sub = lax.broadcasted_iota(jnp.int32, (NP, NP), 0)
    lane = lax.broadcasted_iota(jnp.int32, (NP, NP), 1)
    impm_r = jnp.broadcast_to(impm, (NP, NP))
    rank = _top_rank(impm_r.T, impm_r, sub, lane, 0).astype(F32)
    rank_c = jnp.broadcast_to(rank, (NP, NP)).T
    hit = (rank_c == lane.astype(F32)) & (sub <= cur)
    idx_ref[0, 0] = jnp.sum(jnp.where(hit, sub, 0), axis=0, keepdims=True)

    kwb = kw_ref[0].astype(BF16)
    wlen = kwb.shape[0]
    wpos = T_SAMPLE - wlen + lax.broadcasted_iota(jnp.int32, (1, wlen), 1)
    dist_w = T_SAMPLE - wpos
    valid_w = (dist_w >= 0) & (dist_w <= WINDOW) & (wpos >= 0)
    carry = _flash_step(_flash_init(GRP), _dot_nt(q, kwb) * scale - slope * dist_w.astype(F32), valid_w,
                        vw_ref[0].astype(BF16))
    k_new = kwn_ref[0, 0, 0].astype(BF16).astype(F32)
    v_new = vwn_ref[0, 0, 0].astype(BF16).astype(F32)
    s_new = jnp.sum(q.astype(F32) * k_new, axis=1, keepdims=True) * scale
    _, l_w, acc_w = _flash_one(carry, s_new, v_new)
    ow_ref[0, 0] = acc_w / l_w


def _nsa_sample_cmp_win(slopes, q_s, kvc, ov, win2d, rows_s):
    G, D, DB = NSA_KV_HEADS, HEAD_DIM, DEC_BATCH
    ncp = kvc.shape[3]
    wlen = win2d.shape[1]
    o_spec = pl.BlockSpec((1, 1, GRP, D), lambda b, g, sl: (b, g, 0, 0))
    blk = 2 * ncp * D * 4 + ov.size * 4 + 2 * wlen * D * 4 + 4 * SEL_PAD_S * SEL_PAD_S * 4
    return pl.pallas_call(
        _nsa_sample_cmp_win_kernel,
        grid_spec=pltpu.PrefetchScalarGridSpec(
            num_scalar_prefetch=1,
            grid=(DB, G),
            in_specs=[pl.BlockSpec((1, 1, GRP, D), lambda b, g, sl: (b, g, 0, 0)),
                      pl.BlockSpec((1, 1, 1, ncp, D), lambda b, g, sl: (b, 0, g, 0, 0)),
                      pl.BlockSpec((1, 1, 1, ncp, D), lambda b, g, sl: (b, 1, g, 0, 0)),
                      pl.BlockSpec(ov.shape, lambda b, g, sl: (0, 0)),
                      pl.BlockSpec((1, wlen, D), lambda b, g, sl: (b, 0, g)),
                      pl.BlockSpec((1, wlen, D), lambda b, g, sl: (b, 0, G + g)),
                      pl.BlockSpec((1, 1, 1, 1, D), lambda b, g, sl: (b, 4, g, 0, 0)),
                      pl.BlockSpec((1, 1, 1, 1, D), lambda b, g, sl: (b, 5, g, 0, 0))],
            out_specs=[o_spec, o_spec, pl.BlockSpec((1, 1, 1, SEL_PAD_S), lambda b, g, sl: (b, g, 0, 0))],
        ),
        out_shape=[jax.ShapeDtypeStruct((DB, G, GRP, D), F32),
                   jax.ShapeDtypeStruct((DB, G, GRP, D), F32),
                   jax.ShapeDtypeStruct((DB, G, 1, SEL_PAD_S), jnp.int32)],
        compiler_params=_params(("parallel", "parallel"), blk),
        name="nsa_sample_cmp_win",
    )(slopes, q_s, kvc, kvc, ov, win2d, win2d, rows_s, rows_s)


def _nsa_sample_sel_kernel(pt_ref, idx_ref, slopes_ref, q_ref, ks_ref, vs_ref, ksn_ref, vsn_ref,
                           oc_ref, ow_ref, gl_ref, out_ref, m_scr, l_scr, acc_scr):
    b, g, k = pl.program_id(0), pl.program_id(1), pl.program_id(2)
    scale = HEAD_DIM ** -0.5
    q = q_ref[0, 0]
    slope = _slope_col(slopes_ref, g)
    j = idx_ref[(b * NSA_KV_HEADS + g) * KK_S + k]

    @pl.when(k == 0)
    def _():
        m0, l0, a0 = _flash_init(GRP)
        m_scr[...] = m0
        l_scr[...] = l0
        acc_scr[...] = a0

    carry = (m_scr[...], l_scr[...], acc_scr[...])

    @pl.when(j < NB_PAST)
    def _():
        dist = T_SAMPLE - (j * SEL_LEN + lax.broadcasted_iota(jnp.int32, (1, SEL_LEN), 1))
        s = _dot_nt(q, ks_ref[0].astype(BF16)) * scale - slope * dist.astype(F32)
        m1, l1, a1 = _flash_step(carry, s, dist >= 0, vs_ref[0].astype(BF16))
        m_scr[...] = m1
        l_scr[...] = l1
        acc_scr[...] = a1

    @pl.when(j >= NB_PAST)
    def _():
        k_new = ksn_ref[0, 0, 0].astype(BF16).astype(F32)
        v_new = vsn_ref[0, 0, 0].astype(BF16).astype(F32)
        s_new = jnp.sum(q.astype(F32) * k_new, axis=1, keepdims=True) * scale
        m1, l1, a1 = _flash_one(carry, s_new, v_new)
        m_scr[...] = m1
        l_scr[...] = l1
        acc_scr[...] = a1

    @pl.when(k == KK_S - 1)
    def _():
        gates = jax.nn.sigmoid(gl_ref[0, 0])
        o_s = acc_scr[...] / l_scr[...]
        out_ref[0, 0] = gates[0] * oc_ref[0, 0] + gates[1] * o_s + gates[2] * ow_ref[0, 0]


def _nsa_sample_sel(page_table_flat, idx_flat, slopes, q_s, cache_half, rows_s, o_c, o_w, gate_logits_s):
    G, D, DB = NSA_KV_HEADS, HEAD_DIM, DEC_BATCH
    n_pages = PAST_LEN // PAGE_SIZE
    bpp = PAGE_SIZE // SEL_LEN

    def blk_map(b, g, k, pt, idx, sl, *, slot):
        j = jnp.minimum(idx[(b * G + g) * KK_S + k], NB_PAST - 1)
        return (pt[b * n_pages + j // bpp] * bpp + j % bpp, 0, slot * G + g)

    o_spec = pl.BlockSpec((1, 1, GRP, D), lambda b, g, k, pt, idx, sl: (b, g, 0, 0))
    blk = 2 * SEL_LEN * D * 4 + 6 * GRP * D * 4
    return pl.pallas_call(
        _nsa_sample_sel_kernel,
        grid_spec=pltpu.PrefetchScalarGridSpec(
            num_scalar_prefetch=3,
            grid=(DB, G, KK_S),
            in_specs=[o_spec,
                      pl.BlockSpec((1, SEL_LEN, D), functools.partial(blk_map, slot=2)),
                      pl.BlockSpec((1, SEL_LEN, D), functools.partial(blk_map, slot=3)),
                      pl.BlockSpec((1, 1, 1, 1, D), lambda b, g, k, pt, idx, sl: (b, 2, g, 0, 0)),
                      pl.BlockSpec((1, 1, 1, 1, D), lambda b, g, k, pt, idx, sl: (b, 3, g, 0, 0)),
                      o_spec, o_spec,
                      pl.BlockSpec((1, 1, 3, GRP, 1), lambda b, g, k, pt, idx, sl: (b, g, 0, 0, 0))],
            out_specs=o_spec,
            scratch_shapes=[pltpu.VMEM((GRP, 1), F32), pltpu.VMEM((GRP, 1), F32), pltpu.VMEM((GRP, D), F32)],
        ),
        out_shape=jax.ShapeDtypeStruct((DB, G, GRP, D), F32),
        compiler_params=_params(("parallel", "parallel", "arbitrary"), blk),
        name="nsa_sample_sel",
    )(page_table_flat, idx_flat, slopes, q_s, cache_half, cache_half, rows_s, rows_s, o_c, o_w, gate_logits_s)


def _stack_rows(prompt_rows, sample_rows):
    pad = jnp.zeros((M_PAD - M_PROMPT - M_SAMPLE, prompt_rows.shape[1]), prompt_rows.dtype)
    return jnp.concatenate([prompt_rows, sample_rows.astype(prompt_rows.dtype), pad], axis=0)


def _ffn_block(x, g, w_gate_up, w_down_bf16, layer):
    h = _rmsnorm(x, g, BF16)
    return _matmul_kblocked(_ffn_up(h, w_gate_up, layer), w_down_bf16, layer, F32, residual=x)


def kernel(x_prompt, x_sample, cache_nsa_kv, page_table, state_win_kv, state_mlstm_C, state_mlstm_n, state_mlstm_m,
           g_mix, g_ffn, g_kv, g_final, w_in_a, b_gate_a, g_head_a, w_out_a, w_qg_b, w_out_b, w_kv,
           pe_cmp, w_cmp1, w_cmp2, w_gate_up, w_down):
    H, DK, DV = MLSTM_HEADS, MLSTM_QK_DIM, MLSTM_V_DIM
    G, D, DB = NSA_KV_HEADS, HEAD_DIM, DEC_BATCH
    hd = NSA_HEADS * D
    n_main = 2 * H * DK + 2 * H * DV
    x = _stack_rows(x_prompt.reshape(M_PROMPT, D_MODEL), x_sample.reshape(M_SAMPLE, D_MODEL))
    w_down_bf16 = jnp.pad(w_down, ((0, 0), (0, D_FF_PAD - D_FF), (0, 0))).astype(BF16)

    c_p, n_p, m_p, c_s, n_s, m_s = [], [], [], [], [], []
    for l in range(N_A):
        h, gates = _rmsnorm_with_gates(x, g_mix[l], w_in_a[l][:, n_main:], b_gate_a[l], True)
        z = _matmul_fullk(h, w_in_a, l, n_main, F32)
        gates_t = gates[:M_PROMPT, :2 * H].T.reshape(2, H, 1, M_PROMPT)
        hs_p, cp, np_, mp = _mlstm_prompt(z, gates_t, g_head_a[l])
        gs = gates[S0:S0 + DB, :2 * H]
        lane_b = lambda a: jnp.broadcast_to(a[:, :, None, None], (DB, H, 1, LANES))
        hs_s, cs, ns, ms = _mlstm_sample(z, lane_b(gs[:, :H]), lane_b(gs[:, H:]), lane_b(state_mlstm_m[l]),
                                         state_mlstm_C[l], state_mlstm_n[l].reshape(DB, H, 1, DK), g_head_a[l])
        hs = _stack_rows(hs_p, hs_s.reshape(DB, H * DV))
        x = _matmul_fullk(hs, w_out_a, l, D_MODEL, F32, residual=x)
        x = _ffn_block(x, g_ffn[l], w_gate_up, w_down_bf16, l)
        c_p.append(cp)
        n_p.append(np_.reshape(BATCH, H, DK))
        m_p.append(mp.reshape(BATCH, H))
        c_s.append(cs)
        n_s.append(ns.reshape(DB, H, DK))
        m_s.append(ms[:, :, 0, 0])

    rows = _matmul_fullk(_rmsnorm(x, g_kv, BF16), w_kv[None], 0, 6 * G * D, F32)
    rows_s = rows[S0:S0 + DB].reshape(DB, 6, G, 1, D)
    w1r = _compress_weights(w_cmp1)
    kvc_p = _compress_finish(_compress_proj_prompt(rows, w1r), pe_cmp, w_cmp1, w_cmp2)
    n_pool = cache_nsa_kv.shape[0]
    pt_flat = page_table.reshape(-1).astype(jnp.int32)
    cache2d = cache_nsa_kv.reshape(n_pool, PAGE_SIZE, 4 * G * D)
    kvc_s = _compress_finish(_compress_proj_sample(cache2d, pt_flat, w1r), pe_cmp, w_cmp1, w_cmp2)
    cache_half = cache2d.reshape(n_pool * (PAGE_SIZE // SEL_LEN), SEL_LEN, 4 * G * D)
    win2d = state_win_kv.reshape(DB, state_win_kv.shape[1], 2 * G * D)

    slopes = jnp.exp2(-8.0 * jnp.arange(1, NSA_HEADS + 1, dtype=F32) / NSA_HEADS)
    n_chunks_p = SEQ // CMP_STRIDE
    ov_p = _selection_overlap(n_chunks_p - CMP_R + 1, SEQ // SEL_LEN, n_chunks_p, LANES)
    n_chunks_s = (PAST_LEN + M_SAMPLE // DB) // CMP_STRIDE
    ov_s = _selection_overlap(n_chunks_s - CMP_R + 1, N_SEL_S, n_chunks_s, SEL_PAD_S)

    for l in range(N_B):
        h, gate_logits = _rmsnorm_with_gates(x, g_mix[N_A + l], w_qg_b[l][:, hd:], jnp.zeros((3 * NSA_HEADS,), F32),
                                             False)
        q = _matmul_fullk(h, w_qg_b, l, hd, BF16)
        gl = gate_logits[:, :3 * NSA_HEADS]
        gl_p = gl.reshape(M_PAD, G, 3 * GRP).transpose(1, 0, 2)
        att_p = _nsa_prompt(q, gl_p, kvc_p, rows, ov_p)
        q_s = q[S0:S0 + DB].reshape(DB, G, GRP, D)
        gl_s = gl[S0:S0 + DB].reshape(DB, G, GRP, 3).transpose(0, 1, 3, 2)[..., None]
        o_c, o_w, idx = _nsa_sample_cmp_win(slopes, q_s, kvc_s, ov_s, win2d, rows_s)
        idx_flat = idx[:, :, 0, :KK_S].reshape(-1)
        att_s = _nsa_sample_sel(pt_flat, idx_flat, slopes, q_s, cache_half, rows_s, o_c, o_w, gl_s)
        att = _stack_rows(att_p, att_s.reshape(DB, hd))
        x = _matmul_fullk(att, w_out_b, l, D_MODEL, F32, residual=x)
        x = _ffn_block(x, g_ffn[N_A + l], w_gate_up, w_down_bf16, N_A + l)

    y = _rmsnorm(x, g_final, F32)
    wp = min(WINDOW, SEQ)
    rows_p = rows[:M_PROMPT].reshape(BATCH, SEQ, 6, G, D)
    rows_s6 = rows[S0:S0 + DB].reshape(DB, 1, 6, G, D)
    wbuf = state_win_kv.shape[1]
    win_kv_sample = jnp.concatenate([state_win_kv, rows_s6[:, :, 4:]], axis=1)[:, -wbuf:]
    return (y[:M_PROMPT].reshape(BATCH, SEQ, D_MODEL), y[S0:S0 + DB].reshape(DB, 1, D_MODEL),
            rows_p[:, :, :4], rows_p[:, SEQ - wp:, 4:],
            jnp.stack(c_p), jnp.stack(n_p), jnp.stack(m_p),
            rows_s6[:, :, :4], win_kv_sample,
            jnp.stack(c_s), jnp.stack(n_s), jnp.stack(m_s))
```

```python
import functools

import numpy as np
import jax
import jax.numpy as jnp
from jax import lax
from jax.experimental import pallas as pl
from jax.experimental.pallas import tpu as pltpu

D_MODEL = 4096
BATCH = 4
SEQ = 2048
DEPTH = 4
DEC_BATCH = 8
DEC_SEQ = 1
PAST_LEN = 8192
PAGE_SIZE = 128

N_A = DEPTH // 2
N_B = DEPTH - N_A
MLSTM_HEADS = 8
MLSTM_QK_DIM = D_MODEL // 16
MLSTM_V_DIM = D_MODEL // MLSTM_HEADS
GATE_CAP = 15.0
HEAD_DIM = 128
NSA_HEADS = D_MODEL // HEAD_DIM
NSA_KV_HEADS = 4
GRP = NSA_HEADS // NSA_KV_HEADS
CMP_LEN = 32
CMP_STRIDE = 16
CMP_HID = HEAD_DIM
SEL_LEN = 64
SEL_TOP = 16
WINDOW = 512
D_FF = -(-8 * D_MODEL // (3 * 256)) * 256
EPS = 1e-6
NEG = -1e30

F32 = jnp.float32
BF16 = jnp.bfloat16

LANES = 128
SUBLANES = 8
VMEM_BYTES_V7X = 64 * 1024 * 1024


def _round_up(n, m):
    return -(-n // m) * m


def _pick_tile(n, cap, align=LANES):
    best = None
    for t in range(align, min(n, cap) + 1, align):
        if n % t == 0:
            best = t
    assert best is not None, (n, cap, align)
    return best


M_PROMPT = BATCH * SEQ
M_SAMPLE = DEC_BATCH * DEC_SEQ
S0 = M_PROMPT
M_PAD = _round_up(M_PROMPT + M_SAMPLE, LANES)
TM = _pick_tile(M_PAD, 2048)
TR = _pick_tile(M_PAD, 256)
TN_FULLK = 256
MLSTM_L = 128
TQ = 128
TKS = 512
N_GATE_PAD = LANES
D_FF_PAD = _round_up(D_FF, 1024)
CMP_R = CMP_LEN // CMP_STRIDE
SAMPLE_PAGES_PER_STEP = 8
PERM_ROWS = LANES

assert DEC_SEQ == 1 and S0 % 16 == 0
assert SEQ % PERM_ROWS == 0 and PAGE_SIZE % PERM_ROWS == 0
assert (PERM_ROWS // CMP_STRIDE) & (PERM_ROWS // CMP_STRIDE - 1) == 0
assert SEQ % MLSTM_L == 0 and SEQ % TQ == 0 and SEQ % TKS == 0 and TKS % TQ == 0
assert PAST_LEN % PAGE_SIZE == 0 and PAGE_SIZE % SEL_LEN == 0 and PAGE_SIZE % CMP_STRIDE == 0
assert CMP_R == 2 and SEQ // SEL_LEN <= LANES


def _vmem_limit(block_bytes, single_bytes):
    return int(min(single_bytes + 2 * block_bytes + (12 << 20), VMEM_BYTES_V7X - (8 << 20)))


def _params(sem, block_bytes, single_bytes=0):
    return pltpu.CompilerParams(dimension_semantics=sem,
                                vmem_limit_bytes=_vmem_limit(block_bytes, single_bytes))


def _dot(a, b):
    return jnp.dot(a, b, preferred_element_type=F32)


def _dot_nt(a, b):
    return lax.dot_general(a, b, (((1,), (1,)), ((), ())), preferred_element_type=F32)


def _dot_tn(a, b):
    return lax.dot_general(a, b, (((0,), (0,)), ((), ())), preferred_element_type=F32)


def _dot_exact(a, b):
    return jnp.dot(a, b, precision=lax.Precision.HIGHEST, preferred_element_type=F32)


def _log_sigmoid(x):
    return jnp.minimum(x, 0.0) - jnp.log1p(jnp.exp(-jnp.abs(x)))


def _col_form(row, n):
    return jnp.broadcast_to(row, (n, n)).T


def _rms_kernel(x_ref, g_ref, o_ref):
    x = x_ref[...]
    y = x * lax.rsqrt(jnp.mean(x * x, axis=-1, keepdims=True) + EPS) * g_ref[...]
    o_ref[...] = y.astype(o_ref.dtype)


def _rmsnorm(x, g, out_dtype):
    m, d = x.shape
    blk = TR * d * (4 + jnp.dtype(out_dtype).itemsize)
    return pl.pallas_call(
        _rms_kernel,
        grid=(m // TR,),
        in_specs=[pl.BlockSpec((TR, d), lambda i: (i, 0)), pl.BlockSpec((1, d), lambda i: (0, 0))],
        out_specs=pl.BlockSpec((TR, d), lambda i: (i, 0)),
        out_shape=jax.ShapeDtypeStruct((m, d), out_dtype),
        compiler_params=_params(("parallel",), blk),
        name="rmsnorm",
    )(x, g.reshape(1, d))


def _rms_gate_kernel(x_ref, g_ref, w_ref, b_ref, h_ref, o_ref, *, mlstm_gates):
    x = x_ref[...]
    y = x * lax.rsqrt(jnp.mean(x * x, axis=-1, keepdims=True) + EPS) * g_ref[...]
    h_ref[...] = y.astype(h_ref.dtype)
    z = _dot_exact(y, w_ref[...]) + b_ref[...]
    if mlstm_gates:
        gl = GATE_CAP * jnp.tanh(z / GATE_CAP)
        lane = lax.broadcasted_iota(jnp.int32, z.shape, 1)
        z = jnp.where(lane < MLSTM_HEADS, gl, _log_sigmoid(gl))
    o_ref[...] = z


def _rmsnorm_with_gates(x, g, w_gate, b_gate, mlstm_gates):
    m, d = x.shape
    ng = w_gate.shape[1]
    wp = jnp.pad(w_gate, ((0, 0), (0, N_GATE_PAD - ng)))
    bp = jnp.pad(b_gate, (0, N_GATE_PAD - ng)).reshape(1, N_GATE_PAD)
    blk = TR * d * 6 + d * N_GATE_PAD * 4 + TR * N_GATE_PAD * 4
    return pl.pallas_call(
        functools.partial(_rms_gate_kernel, mlstm_gates=mlstm_gates),
        grid=(m // TR,),
        in_specs=[pl.BlockSpec((TR, d), lambda i: (i, 0)),
                  pl.BlockSpec((1, d), lambda i: (0, 0)),
                  pl.BlockSpec((d, N_GATE_PAD), lambda i: (0, 0)),
                  pl.BlockSpec((1, N_GATE_PAD), lambda i: (0, 0))],
        out_specs=[pl.BlockSpec((TR, d), lambda i: (i, 0)),
                   pl.BlockSpec((TR, N_GATE_PAD), lambda i: (i, 0))],
        out_shape=[jax.ShapeDtypeStruct((m, d), BF16), jax.ShapeDtypeStruct((m, N_GATE_PAD), F32)],
        compiler_params=_params(("parallel",), blk),
        name="rmsnorm_gates",
    )(x, g.reshape(1, d), wp, bp)


def _mm_fullk_kernel(*refs, has_res, rows_valid):
    x_ref, w_ref, o_ref = refs[0], refs[1], refs[-1]
    w = w_ref[...]
    if rows_valid < w.shape[0]:
        w = jnp.where(lax.broadcasted_iota(jnp.int32, w.shape, 0) < rows_valid, w, 0.0)
    r = _dot(x_ref[...], w.astype(BF16))
    if has_res:
        r = r + refs[2][...]
    o_ref[...] = r.astype(o_ref.dtype)


def _row_tile_spec(kb, kh=0):
    return pl.BlockSpec((TM, kb), lambda i, j: (i, kh), pipeline_mode=pl.Buffered(1))


def _matmul_fullk(x, w, layer, n_cols, out_dtype, residual=None, k_split=1):
    m, kd = x.shape
    tn = TN_FULLK
    kb = kd // k_split
    k_rows = w.shape[1]
    assert n_cols % tn == 0 and kd % k_split == 0 and kb % LANES == 0 and w.dtype == F32
    assert (k_split - 1) * kb < k_rows <= kd
    out = residual
    for kh in range(k_split):
        last = kh == k_split - 1
        dt = out_dtype if last else F32
        osz = jnp.dtype(dt).itemsize
        in_specs = [_row_tile_spec(kb, kh), pl.BlockSpec((None, kb, tn), lambda i, j, kh=kh: (layer, kh, j))]
        args = [x, w]
        streamed = kb * tn * 4 + TM * tn * osz
        if out is not None:
            in_specs.append(pl.BlockSpec((TM, tn), lambda i, j: (i, j)))
            args.append(out)
            streamed += TM * tn * 4
        out = pl.pallas_call(
            functools.partial(_mm_fullk_kernel, has_res=out is not None, rows_valid=min(kb, k_rows - kh * kb)),
            grid=(m // TM, n_cols // tn),
            in_specs=in_specs,
            out_specs=pl.BlockSpec((TM, tn), lambda i, j: (i, j)),
            out_shape=jax.ShapeDtypeStruct((m, n_cols), dt),
            compiler_params=_params(("parallel", "arbitrary"), streamed, single_bytes=TM * kb * 2),
            name="matmul_fullk",
        )(*args)
    return out


def _ffn_up_kernel(x_ref, wg_ref, wu_ref, o_ref, *, n_real):
    j = pl.program_id(1)

    @pl.when(j < n_real)
    def _():
        x = x_ref[...]
        gate = _dot(x, wg_ref[...].astype(BF16))
        up = _dot(x, wu_ref[...].astype(BF16))
        o_ref[...] = (jax.nn.silu(gate) * up).astype(o_ref.dtype)

    @pl.when(j >= n_real)
    def _():
        o_ref[...] = jnp.zeros_like(o_ref)


def _ffn_up(x, w_gate_up, layer):
    m, kd = x.shape
    tn = TN_FULLK
    assert D_FF % tn == 0 and D_FF_PAD % tn == 0
    n_real = D_FF // tn
    col = lambda j: jnp.minimum(j, n_real - 1)
    streamed = 2 * kd * tn * 4 + TM * tn * 2
    return pl.pallas_call(
        functools.partial(_ffn_up_kernel, n_real=n_real),
        grid=(m // TM, D_FF_PAD // tn),
        in_specs=[_row_tile_spec(kd),
                  pl.BlockSpec((None, kd, tn), lambda i, j: (layer, 0, col(j))),
                  pl.BlockSpec((None, kd, tn), lambda i, j: (layer, 0, n_real + col(j)))],
        out_specs=pl.BlockSpec((TM, tn), lambda i, j: (i, j)),
        out_shape=jax.ShapeDtypeStruct((m, D_FF_PAD), BF16),
        compiler_params=_params(("parallel", "arbitrary"), streamed, single_bytes=TM * kd * 2),
        name="ffn_up",
    )(x, w_gate_up, w_gate_up)


def _mlstm_prompt_kernel(q_ref, k_ref, v_ref, o_ref, g_ref, gh_ref, hs_ref, c_ref, n_ref, m_ref):
    L, DK, DV = MLSTM_L, MLSTM_QK_DIM, MLSTM_V_DIM

    @pl.when(pl.program_id(1) == 0)
    def _():
        c_ref[...] = jnp.zeros_like(c_ref)
        n_ref[...] = jnp.zeros_like(n_ref)
        m_ref[...] = jnp.zeros_like(m_ref)

    t_idx = lax.broadcasted_iota(jnp.int32, (L, L), 0)
    s_idx = lax.broadcasted_iota(jnp.int32, (L, L), 1)
    for hd in range(MLSTM_HEADS):
        ig_row = g_ref[0, hd]
        lf_row = g_ref[1, hd]
        m_prev = m_ref[0, hd]
        lf_c = _col_form(lf_row, L)
        b_row = jnp.sum(jnp.where(t_idx <= s_idx, lf_c, 0.0), axis=0, keepdims=True)
        b_r = jnp.broadcast_to(b_row, (L, L))
        b_c = b_r.T
        ig_r = jnp.broadcast_to(ig_row, (L, L))
        log_d = jnp.where(s_idx <= t_idx, b_c - b_r + ig_r, -jnp.inf)
        b_col = b_c[:, 0:1]
        inter = b_col + m_prev
        m_t = jnp.maximum(jnp.max(log_d, axis=1, keepdims=True), inter)
        d = jnp.exp(log_d - m_t)
        w_inter = jnp.exp(inter - m_t)

        q = q_ref[:, hd * DK:(hd + 1) * DK] * (DK ** -0.5)
        k = k_ref[:, hd * DK:(hd + 1) * DK]
        qb = q.astype(BF16)
        kb = k.astype(BF16)
        vb = v_ref[:, hd * DV:(hd + 1) * DV].astype(BF16)
        s = _dot_nt(qb, kb) * d
        c_prev = c_ref[0, hd]
        n_prev = n_ref[0, hd]
        num = w_inter * _dot(qb, c_prev.astype(BF16)) + _dot(s.astype(BF16), vb)
        den = w_inter * jnp.sum(q * n_prev, axis=1, keepdims=True) + jnp.sum(s, axis=1, keepdims=True)
        h = num / jnp.maximum(jnp.abs(den), jnp.exp(-m_t))
        hn = h * lax.rsqrt(jnp.mean(h * h, axis=1, keepdims=True) + EPS)
        cols = slice(hd * DV, (hd + 1) * DV)
        hs_ref[:, cols] = (hn * gh_ref[:, cols] * jax.nn.sigmoid(o_ref[:, cols])).astype(hs_ref.dtype)

        m_new = m_t[L - 1:L, :]
        b_last = b_row[:, L - 1:L]
        ig_col = ig_r.T[:, 0:1]
        decay = jnp.exp(b_last - b_col + ig_col - m_new)
        carry = jnp.exp(b_last + m_prev - m_new)
        kd = k * decay
        c_ref[0, hd] = carry * c_prev + _dot_tn(kd.astype(BF16), vb)
        n_ref[0, hd] = carry * n_prev + jnp.sum(kd, axis=0, keepdims=True)
        m_ref[0, hd] = m_new


def _mlstm_prompt(z, gates_t, g_head):
    H, DK, DV, L = MLSTM_HEADS, MLSTM_QK_DIM, MLSTM_V_DIM, MLSTM_L
    nc = SEQ // L
    assert (H * DV) % (H * DK) == 0
    v0 = (2 * H * DK) // (H * DV)
    row = lambda b, c: b * nc + c
    blk = L * (2 * H * DK + 2 * H * DV) * 4 + L * H * DV * 2 + H * DK * DV * 4
    return pl.pallas_call(
        _mlstm_prompt_kernel,
        grid=(BATCH, nc),
        in_specs=[pl.BlockSpec((L, H * DK), lambda b, c: (row(b, c), 0)),
                  pl.BlockSpec((L, H * DK), lambda b, c: (row(b, c), 1)),
                  pl.BlockSpec((L, H * DV), lambda b, c: (row(b, c), v0)),
                  pl.BlockSpec((L, H * DV), lambda b, c: (row(b, c), v0 + 1)),
                  pl.BlockSpec((2, H, 1, L), lambda b, c: (0, 0, 0, row(b, c))),
                  pl.BlockSpec((1, H * DV), lambda b, c: (0, 0))],
        out_specs=[pl.BlockSpec((L, H * DV), lambda b, c: (row(b, c), 0)),
                   pl.BlockSpec((1, H, DK, DV), lambda b, c: (b, 0, 0, 0)),
                   pl.BlockSpec((1, H, 1, DK), lambda b, c: (b, 0, 0, 0)),
                   pl.BlockSpec((1, H, 1, 1), lambda b, c: (b, 0, 0, 0))],
        out_shape=[jax.ShapeDtypeStruct((M_PROMPT, H * DV), BF16),
                   jax.ShapeDtypeStruct((BATCH, H, DK, DV), F32),
                   jax.ShapeDtypeStruct((BATCH, H, 1, DK), F32),
                   jax.ShapeDtypeStruct((BATCH, H, 1, 1), F32)],
        compiler_params=_params(("parallel", "arbitrary"), blk),
        name="mlstm_prompt",
    )(z, z, z, z, gates_t, g_head.reshape(1, H * DV))


def _mlstm_sample_kernel(q_ref, k_ref, v_ref, o_ref, ig_ref, lf_ref, m0_ref, c0_ref, n0_ref, gh_ref,
                         hs_ref, c_ref, n_ref, m_ref):
    DK = MLSTM_QK_DIM
    b = pl.program_id(0)
    q = q_ref[pl.ds(b, 1), :] * (DK ** -0.5)
    k = k_ref[pl.ds(b, 1), :]
    v = v_ref[pl.ds(b, 1), :]
    og = o_ref[pl.ds(b, 1), :]
    ig = ig_ref[0, 0]
    lf = lf_ref[0, 0]
    m0 = m0_ref[0, 0]
    m_t = jnp.maximum(ig, lf + m0)
    d = jnp.exp(ig - m_t)[:, 0:1]
    w = jnp.exp(lf + m0 - m_t)[:, 0:1]
    c0 = c0_ref[0, 0]
    n0 = n0_ref[0, 0]
    q_col = _col_form(q, DK)[:, 0:1]
    k_col = _col_form(k, DK)[:, 0:1]
    s = jnp.sum(q * k, axis=1, keepdims=True) * d
    num = w * jnp.sum(q_col * c0, axis=0, keepdims=True) + s * v
    den = w * jnp.sum(q * n0, axis=1, keepdims=True) + s
    h = num / jnp.maximum(jnp.abs(den), jnp.exp(-m_t)[:, 0:1])
    hn = h * lax.rsqrt(jnp.mean(h * h, axis=1, keepdims=True) + EPS)
    hs_ref[0] = hn * gh_ref[...] * jax.nn.sigmoid(og)
    c_ref[0, 0] = w * c0 + d * (k_col * v)
    n_ref[0, 0] = w * n0 + d * k
    m_ref[0, 0] = m_t


def _mlstm_sample(z, ig_b, lf_b, m_b, c0, n0, g_head):
    H, DK, DV = MLSTM_HEADS, MLSTM_QK_DIM, MLSTM_V_DIM
    DB = DEC_BATCH
    r0 = S0 // SUBLANES
    v0 = (2 * H * DK) // DV
    sc = pl.BlockSpec((1, 1, 1, LANES), lambda b, h: (b, h, 0, 0))
    blk = 2 * DK * DV * 4 + SUBLANES * (2 * DK + 2 * DV) * 4
    return pl.pallas_call(
        _mlstm_sample_kernel,
        grid=(DB, H),
        in_specs=[pl.BlockSpec((SUBLANES, DK), lambda b, h: (r0, h)),
                  pl.BlockSpec((SUBLANES, DK), lambda b, h: (r0, H + h)),
                  pl.BlockSpec((SUBLANES, DV), lambda b, h: (r0, v0 + h)),
                  pl.BlockSpec((SUBLANES, DV), lambda b, h: (r0, v0 + H + h)),
                  sc, sc, sc,
                  pl.BlockSpec((1, 1, DK, DV), lambda b, h: (b, h, 0, 0)),
                  pl.BlockSpec((1, 1, 1, DK), lambda b, h: (b, h, 0, 0)),
                  pl.BlockSpec((1, DV), lambda b, h: (0, h))],
        out_specs=[pl.BlockSpec((1, 1, DV), lambda b, h: (b, 0, h)),
                   pl.BlockSpec((1, 1, DK, DV), lambda b, h: (b, h, 0, 0)),
                   pl.BlockSpec((1, 1, 1, DK), lambda b, h: (b, h, 0, 0)),
                   sc],
        out_shape=[jax.ShapeDtypeStruct((DB, 1, H * DV), F32),
                   jax.ShapeDtypeStruct((DB, H, DK, DV), F32),
                   jax.ShapeDtypeStruct((DB, H, 1, DK), F32),
                   jax.ShapeDtypeStruct((DB, H, 1, LANES), F32)],
        compiler_params=_params(("parallel", "parallel"), blk),
        name="mlstm_sample",
    )(z, z, z, z, ig_b, lf_b, m_b, c0, n0, g_head.reshape(1, H * DV))


def _compress_proj_kernel(*refs, n_in, groups_per_in, lead):
    x_refs = refs[-(n_in + 3):-3]
    w_ref, p_ref, xp_scr = refs[-3], refs[-2], refs[-1]
    G, D = NSA_KV_HEADS, HEAD_DIM
    cpg = PERM_ROWS // CMP_STRIDE
    nch = n_in * groups_per_in * cpg
    i_idx = lax.broadcasted_iota(jnp.int32, (PERM_ROWS, PERM_ROWS), 0)
    j_idx = lax.broadcasted_iota(jnp.int32, (PERM_ROWS, PERM_ROWS), 1)
    src_row = (i_idx & (cpg - 1)) * CMP_STRIDE + (i_idx >> (cpg.bit_length() - 1))
    perm = (j_idx == src_row).astype(BF16)
    for i, xr in enumerate(x_refs):
        for p in range(groups_per_in):
            rows = pl.ds(p * PERM_ROWS, PERM_ROWS)
            xg = xr[0, rows, :] if lead else xr[rows, :]
            xg = _dot(perm, xg.astype(BF16))
            ch0 = (i * groups_per_in + p) * cpg
            for c in range(CMP_STRIDE):
                xp_scr[c, ch0:ch0 + cpg, :] = xg[c * cpg:(c + 1) * cpg, :]
    for s in range(2):
        acc = jnp.zeros((G * nch, CMP_R * CMP_HID), F32)
        for c in range(CMP_STRIDE):
            xs = jnp.concatenate([xp_scr[c, :, (s * G + g) * D:(s * G + g + 1) * D] for g in range(G)], axis=0)
            acc = acc + _dot(xs.astype(BF16), w_ref[s, c])
        for g in range(G):
            p_ref[0, s, g] = acc[g * nch:(g + 1) * nch]


def _compress_weights(w_cmp1):
    w = w_cmp1.reshape(2, CMP_R, CMP_STRIDE, HEAD_DIM, CMP_HID)
    return w.transpose(0, 2, 3, 1, 4).reshape(2, CMP_STRIDE, HEAD_DIM, CMP_R * CMP_HID).astype(BF16)


def _compress_proj_prompt(rows, w1r):
    G, D = NSA_KV_HEADS, HEAD_DIM
    nch = SEQ // CMP_STRIDE
    pw = CMP_R * CMP_HID
    blk = SEQ * 2 * G * D * 4 + w1r.size * 2 + 2 * G * nch * pw * 4 + SEQ * 2 * G * D * 4
    return pl.pallas_call(
        functools.partial(_compress_proj_kernel, n_in=1, groups_per_in=SEQ // PERM_ROWS, lead=False),
        grid=(BATCH,),
        in_specs=[pl.BlockSpec((SEQ, 2 * G * D), lambda b: (b, 0)),
                  pl.BlockSpec(w1r.shape, lambda b: (0, 0, 0, 0))],
        out_specs=pl.BlockSpec((1, 2, G, nch, pw), lambda b: (b, 0, 0, 0, 0)),
        out_shape=jax.ShapeDtypeStruct((BATCH, 2, G, nch, pw), F32),
        scratch_shapes=[pltpu.VMEM((CMP_STRIDE, nch, 2 * G * D), F32)],
        compiler_params=_params(("parallel",), blk),
        name="compress_proj_prompt",
    )(rows, w1r)


def _compress_proj_sample(cache2d, page_table_flat, w1r):
    G, D = NSA_KV_HEADS, HEAD_DIM
    n_pages = PAST_LEN // PAGE_SIZE
    pps = SAMPLE_PAGES_PER_STEP
    assert n_pages % pps == 0
    cpi = PAGE_SIZE // CMP_STRIDE
    nch_step = pps * cpi
    pw = CMP_R * CMP_HID

    def page_map(b, j, pt, *, i):
        return (pt[b * n_pages + j * pps + i], 0, 0)

    in_specs = [pl.BlockSpec((1, PAGE_SIZE, 2 * G * D), functools.partial(page_map, i=i)) for i in range(pps)]
    in_specs.append(pl.BlockSpec(w1r.shape, lambda b, j, pt: (0, 0, 0, 0)))
    blk = 2 * pps * PAGE_SIZE * 2 * G * D * 4 + w1r.size * 2 + 2 * G * nch_step * pw * 4
    return pl.pallas_call(
        functools.partial(_compress_proj_kernel, n_in=pps, groups_per_in=PAGE_SIZE // PERM_ROWS, lead=True),
        grid_spec=pltpu.PrefetchScalarGridSpec(
            num_scalar_prefetch=1,
            grid=(DEC_BATCH, n_pages // pps),
            in_specs=in_specs,
            out_specs=pl.BlockSpec((1, 2, G, nch_step, pw), lambda b, j, pt: (b, 0, 0, j, 0)),
            scratch_shapes=[pltpu.VMEM((CMP_STRIDE, nch_step, 2 * G * D), F32)],
        ),
        out_shape=jax.ShapeDtypeStruct((DEC_BATCH, 2, G, n_pages * cpi, pw), F32),
        compiler_params=_params(("parallel", "arbitrary"), blk),
        name="compress_proj_sample",
    )(page_table_flat, *([cache2d] * pps), w1r)


def _compress_finish_kernel(p_ref, pe_ref, w1_ref, w2_ref, o_ref):
    nch = p_ref.shape[3]
    pe = jnp.broadcast_to(pe_ref[0], (SUBLANES, pe_ref.shape[2])).astype(BF16)
    pe_term = _dot(pe, w1_ref[0].astype(BF16))[0:1]
    w2 = w2_ref[0].astype(BF16)
    for g in range(NSA_KV_HEADS):
        p = p_ref[0, 0, g]
        nxt = pltpu.roll(p[:, CMP_HID:], shift=nch - 1, axis=0)
        hidden = p[:, :CMP_HID] + nxt + pe_term
        o_ref[0, 0, g] = _dot(jax.nn.gelu(hidden).astype(BF16), w2)


def _compress_finish(p, pe_cmp, w_cmp1, w_cmp2):
    nb, _, G, nch, pw = p.shape
    pe = pe_cmp.reshape(2, 1, CMP_LEN * HEAD_DIM)
    blk = G * nch * (pw + HEAD_DIM) * 4 + CMP_LEN * HEAD_DIM * (CMP_HID + 1) * 4
    return pl.pallas_call(
        _compress_finish_kernel,
        grid=(nb, 2),
        in_specs=[pl.BlockSpec((1, 1, G, nch, pw), lambda b, s: (b, s, 0, 0, 0)),
                  pl.BlockSpec((1, 1, CMP_LEN * HEAD_DIM), lambda b, s: (s, 0, 0)),
                  pl.BlockSpec((1, CMP_LEN * HEAD_DIM, CMP_HID), lambda b, s: (s, 0, 0)),
                  pl.BlockSpec((1, CMP_HID, HEAD_DIM), lambda b, s: (s, 0, 0))],
        out_specs=pl.BlockSpec((1, 1, G, nch, HEAD_DIM), lambda b, s: (b, s, 0, 0, 0)),
        out_shape=jax.ShapeDtypeStruct((nb, 2, G, nch, HEAD_DIM), F32),
        compiler_params=_params(("parallel", "parallel"), blk),
        name="compress_finish",
    )(p, pe, w_cmp1, w_cmp2)


def _selection_overlap(n_cmp, n_sel, rows, cols):
    start = np.arange(n_cmp)[:, None] * CMP_STRIDE
    end = start + CMP_LEN - 1
    j = np.arange(n_sel)[None, :]
    ov = ((start <= (j + 1) * SEL_LEN - 1) & (end >= j * SEL_LEN)).astype(np.float32)
    out = np.zeros((rows, cols), np.float32)
    out[:n_cmp, :n_sel] = ov
    return jnp.asarray(out)


def _masked_softmax(s, valid):
    sm = jnp.where(valid, s, NEG)
    e = jnp.exp(sm - jnp.max(sm, axis=-1, keepdims=True))
    return jnp.where(valid, e / jnp.sum(e, axis=-1, keepdims=True), 0.0)


def _flash_step(carry, s, valid, vb):
    m, l, acc = carry
    sm = jnp.where(valid, s, NEG)
    m_new = jnp.maximum(m, jnp.max(sm, axis=-1, keepdims=True))
    alpha = jnp.exp(m - m_new)
    e = jnp.exp(sm - m_new)
    l = alpha * l + jnp.sum(e, axis=-1, keepdims=True)
    acc = alpha * acc + _dot(jnp.where(valid, e, 0.0).astype(BF16), vb)
    return m_new, l, acc


def _flash_one(carry, s_new, v_new):
    m, l, acc = carry
    m_new = jnp.maximum(m, s_new)
    alpha = jnp.exp(m - m_new)
    e = jnp.exp(s_new - m_new)
    return m_new, alpha * l + e, alpha * acc + e * v_new


def _flash_init(rows):
    return (jnp.full((rows, 1), -jnp.inf, F32), jnp.zeros((rows, 1), F32), jnp.zeros((rows, HEAD_DIM), F32))


def _top_rank(impm_c, impm_r, jp_idx, j_idx, axis):
    before = (impm_c > impm_r) | ((impm_c == impm_r) & (jp_idx < j_idx))
    return jnp.sum(before.astype(jnp.int32), axis=axis, keepdims=True)


FEAT_SIG = 96
FEAT_SIG_FULL = FEAT_SIG + 9
MASK_BIG = 2.0 ** 100
POS_RADIX = 256
LOG2E = 1.4426950408889634
assert SEQ // SEL_LEN <= FEAT_SIG and SEQ >= WINDOW + TQ


def _key_features(pos, block):
    n = pos.shape[0]
    kf = np.zeros((n, LANES), np.float32)
    if block is not None:
        kf[np.arange(n), block] = -MASK_BIG
    kf[:, FEAT_SIG:FEAT_SIG + 3] = (pos % POS_RADIX)[:, None]
    kf[:, FEAT_SIG + 3:FEAT_SIG + 6] = (pos // POS_RADIX)[:, None]
    kf[:, FEAT_SIG + 6:FEAT_SIG + 9] = 1.0
    return jnp.asarray(kf, BF16)


def _sigma_table():
    slopes = np.exp2(-8.0 * np.arange(1, NSA_HEADS + 1, dtype=np.float32) / NSA_HEADS).astype(np.float32)
    sigma = (slopes / np.float32(HEAD_DIM ** -0.5)).astype(np.float32)
    bf = lambda a: np.asarray(np.asarray(a, dtype=BF16), dtype=np.float32)
    s1 = bf(sigma)
    s2 = bf(sigma - s1)
    s3 = bf(sigma - s1 - s2)
    tab = np.zeros((NSA_HEADS, LANES), np.float32)
    for i, s in enumerate((s1, s2, s3)):
        tab[:, FEAT_SIG + i] = s
        tab[:, FEAT_SIG + 3 + i] = s * POS_RADIX
    tab[:, FEAT_SIG_FULL] = sigma
    return jnp.asarray(tab)


def _nsa_prompt_kernel(q_ref, gl_ref, sig_ref, kc_ref, vc_ref, ks_ref, vs_ref, kw_ref, vw_ref,
                       ov_ref, kfc_ref, kfs_ref, kfw_ref, out_ref,
                       qa_scr, kca_scr, vcb_scr, ksa_scr, vsb_scr, kwa_scr, vwb_scr, m_scr, l_scr, acc_scr):
    qi = pl.program_id(2)
    D = HEAD_DIM
    c2 = (D ** -0.5) * LOG2E
    n_sel = SEQ // SEL_LEN
    qs = qi * TQ
    tpos = qs + lax.broadcasted_iota(jnp.int32, (TQ, 1), 0)
    tposf = tpos.astype(F32)
    lane = lax.broadcasted_iota(jnp.int32, (1, LANES), 1)
    heads = [slice(r * TQ, (r + 1) * TQ) for r in range(GRP)]

    @pl.when(qi == 0)
    def _():
        kca_scr[:, 0:D] = kc_ref[0, 0, 0].astype(BF16)
        kca_scr[:, D:2 * D] = kfc_ref[...]
        vcb_scr[...] = vc_ref[0, 0, 0].astype(BF16)
        ksa_scr[:, 0:D] = ks_ref[...].astype(BF16)
        ksa_scr[:, D:2 * D] = kfs_ref[...]
        vsb_scr[...] = vs_ref[...].astype(BF16)
        kwa_scr[:, 0:D] = kw_ref[...].astype(BF16)
        kwa_scr[:, D:2 * D] = kfw_ref[...]
        vwb_scr[...] = vw_ref[...].astype(BF16)

    def write_features(nsel):
        for r in range(GRP):
            sig_row = sig_ref[r:r + 1, :]
            c = -sig_row[:, FEAT_SIG_FULL:FEAT_SIG_FULL + 1] * tposf
            c_hi = c.astype(BF16).astype(F32)
            c_mid = (c - c_hi).astype(BF16).astype(F32)
            c_lo = c - c_hi - c_mid
            f = jnp.where(lane == FEAT_SIG + 6, c_hi,
                          jnp.where(lane == FEAT_SIG + 7, c_mid, jnp.where(lane == FEAT_SIG + 8, c_lo, sig_row)))
            if nsel is not None:
                f = jnp.where(lane < n_sel, nsel, f)
            qa_scr[heads[r], D:2 * D] = f.astype(BF16)

    for r in range(GRP):
        qa_scr[heads[r], 0:D] = q_ref[:, r * D:(r + 1) * D]
    write_features(None)

    ncp = kca_scr.shape[0]
    cpos = lax.broadcasted_iota(jnp.int32, (1, ncp), 1) * CMP_STRIDE + (CMP_LEN - 1)
    valid_c = cpos <= tpos
    s_c = _dot_nt(qa_scr[...], kca_scr[...])
    p_sum = jnp.zeros((TQ, ncp), F32)
    p_c = []
    for r in range(GRP):
        sm = jnp.where(valid_c, s_c[heads[r]], NEG)
        e = jnp.exp2((sm - jnp.max(sm, axis=-1, keepdims=True)) * c2)
        p = jnp.where(valid_c, e / jnp.sum(e, axis=-1, keepdims=True), 0.0)
        p_c.append(p.astype(BF16))
        p_sum = p_sum + p
    o_c = _dot(jnp.concatenate(p_c, axis=0), vcb_scr[...])
    imp = _dot_exact(p_sum, ov_ref[...])

    cur = tpos // SEL_LEN
    forced = (lane == 0) | (lane == cur) | (lane == cur - 1)
    impm = jnp.where(forced, jnp.inf, jnp.where(lane <= cur, imp, -jnp.inf))
    rank = jnp.zeros((TQ, LANES), jnp.int32)
    for jp in range(n_sel):
        cj = impm[:, jp:jp + 1]
        rank = rank + ((cj > impm) | ((cj == impm) & (jp < lane))).astype(jnp.int32)
    selected = (rank < min(SEL_TOP, n_sel)) & (lane <= cur)
    write_features(jnp.where(selected, 0.0, 1.0))

    m_scr[...] = jnp.full(m_scr.shape, -jnp.inf, F32)
    l_scr[...] = jnp.zeros(l_scr.shape, F32)
    acc_scr[...] = jnp.zeros(acc_scr.shape, F32)

    def sel_tile(kt, bias):
        k0 = pl.multiple_of(kt * TKS, TKS)
        s_all = _dot_nt(qa_scr[...], ksa_scr[pl.ds(k0, TKS), :])
        p_all = []
        for r in range(GRP):
            s = s_all[heads[r]]
            if bias is not None:
                s = s + bias
            m_old = m_scr[heads[r]]
            m_new = jnp.maximum(m_old, jnp.max(s, axis=-1, keepdims=True))
            alpha = jnp.exp2((m_old - m_new) * c2)
            e = jnp.exp2((s - m_new) * c2)
            m_scr[heads[r]] = m_new
            l_scr[heads[r]] = alpha * l_scr[heads[r]] + jnp.sum(e, axis=-1, keepdims=True)
            acc_scr[heads[r]] = alpha * acc_scr[heads[r]]
            p_all.append(e.astype(BF16))
        acc_scr[...] += _dot(jnp.concatenate(p_all, axis=0), vsb_scr[pl.ds(k0, TKS), :])

    n_past = qs // TKS

    def past_tile(kt, carry):
        sel_tile(kt, None)
        return carry

    lax.fori_loop(0, n_past, past_tile, 0)
    spos_d = n_past * TKS + lax.broadcasted_iota(jnp.int32, (1, TKS), 1)
    sel_tile(n_past, jnp.where(spos_d <= tpos, 0.0, -MASK_BIG))
    o_s = acc_scr[...] / l_scr[...]

    wk = WINDOW + TQ
    w0 = pl.multiple_of(jnp.maximum(qs - WINDOW, 0), TQ)
    dist_w = tpos - (w0 + lax.broadcasted_iota(jnp.int32, (1, wk), 1))
    bias_w = jnp.where((dist_w >= 0) & (dist_w <= WINDOW), 0.0, -MASK_BIG)
    s_w = _dot_nt(qa_scr[...], kwa_scr[pl.ds(w0, wk), :])
    p_w, l_w = [], []
    for r in range(GRP):
        s = s_w[heads[r]] + bias_w
        e = jnp.exp2((s - jnp.max(s, axis=-1, keepdims=True)) * c2)
        l_w.append(jnp.sum(e, axis=-1, keepdims=True))
        p_w.append(e.astype(BF16))
    o_w = _dot(jnp.concatenate(p_w, axis=0), vwb_scr[pl.ds(w0, wk), :]) / jnp.concatenate(l_w, axis=0)

    for r in range(GRP):
        gates = jax.nn.sigmoid(gl_ref[0, :, 3 * r:3 * r + 3])
        out = gates[:, 0:1] * o_c[heads[r]] + gates[:, 1:2] * o_s[heads[r]] + gates[:, 2:3] * o_w[heads[r]]
        out_ref[:, r * D:(r + 1) * D] = out.astype(out_ref.dtype)


def _nsa_prompt(q, gate_logits, kvc, rows, ov):
    G, D = NSA_KV_HEADS, HEAD_DIM
    nq = SEQ // TQ
    ncp = kvc.shape[3]
    R = GRP * TQ
    spos = np.arange(SEQ)
    kf_cmp = _key_features(np.arange(ncp) * CMP_STRIDE + (CMP_LEN - 1), None)
    kf_sel = _key_features(spos, spos // SEL_LEN)
    kf_win = _key_features(spos, None)
    row = lambda b, g, qi: b * nq + qi
    kv_spec = lambda slot: pl.BlockSpec((SEQ, D), lambda b, g, qi: (b, slot * G + g))
    const = lambda a: pl.BlockSpec(a.shape, lambda b, g, qi: (0, 0))
    blk = (TQ * GRP * D * 4 + 2 * ncp * D * 4 + 4 * SEQ * D * 4 + 3 * SEQ * D * 2 + SEQ * D * 6 + R * D * 16
           + R * (TKS + WINDOW + TQ) * 6)
    return pl.pallas_call(
        _nsa_prompt_kernel,
        grid=(BATCH, G, nq),
        in_specs=[pl.BlockSpec((TQ, GRP * D), lambda b, g, qi: (row(b, g, qi), g)),
                  pl.BlockSpec((1, TQ, 3 * GRP), lambda b, g, qi: (g, row(b, g, qi), 0)),
                  pl.BlockSpec((GRP, LANES), lambda b, g, qi: (g, 0)),
                  pl.BlockSpec((1, 1, 1, ncp, D), lambda b, g, qi: (b, 0, g, 0, 0)),
                  pl.BlockSpec((1, 1, 1, ncp, D), lambda b, g, qi: (b, 1, g, 0, 0)),
                  kv_spec(2), kv_spec(3), kv_spec(4), kv_spec(5),
                  const(ov), const(kf_cmp), const(kf_sel), const(kf_win)],
        out_specs=pl.BlockSpec((TQ, GRP * D), lambda b, g, qi: (row(b, g, qi), g)),
        out_shape=jax.ShapeDtypeStruct((M_PROMPT, NSA_HEADS * D), BF16),
        scratch_shapes=[pltpu.VMEM((R, 2 * D), BF16),
                        pltpu.VMEM((ncp, 2 * D), BF16), pltpu.VMEM((ncp, D), BF16),
                        pltpu.VMEM((SEQ, 2 * D), BF16), pltpu.VMEM((SEQ, D), BF16),
                        pltpu.VMEM((SEQ, 2 * D), BF16), pltpu.VMEM((SEQ, D), BF16),
                        pltpu.VMEM((R, 1), F32), pltpu.VMEM((R, 1), F32), pltpu.VMEM((R, D), F32)],
        compiler_params=_params(("parallel", "parallel", "arbitrary"), blk),
        name="nsa_prompt",
    )(q, gate_logits, _sigma_table(), kvc, kvc, rows, rows, rows, rows, ov, kf_cmp, kf_sel, kf_win)


T_SAMPLE = PAST_LEN
N_SEL_S = -(-(PAST_LEN + 1) // SEL_LEN)
NB_PAST = PAST_LEN // SEL_LEN
KK_S = min(SEL_TOP, N_SEL_S)
SEL_PAD_S = _round_up(N_SEL_S, LANES)
assert PAST_LEN % SEL_LEN == 0 and PAST_LEN % CMP_STRIDE == 0


def _slope_col(slopes_ref, g):
    r_idx = lax.broadcasted_iota(jnp.int32, (GRP, 1), 0)
    col = jnp.zeros((GRP, 1), F32)
    for r in range(GRP):
        col = jnp.where(r_idx == r, slopes_ref[g * GRP + r], col)
    return col


def _nsa_sample_cmp_win_kernel(slopes_ref, q_ref, kc_ref, vc_ref, ov_ref, kw_ref, vw_ref, kwn_ref, vwn_ref,
                               oc_ref, ow_ref, idx_ref):
    g = pl.program_id(1)
    scale = HEAD_DIM ** -0.5
    q = q_ref[0, 0]
    slope = _slope_col(slopes_ref, g)

    kcb = kc_ref[0, 0, 0].astype(BF16)
    ncp = kcb.shape[0]
    cpos = lax.broadcasted_iota(jnp.int32, (1, ncp), 1) * CMP_STRIDE + (CMP_LEN - 1)
    dist_c = T_SAMPLE - cpos
    p_c = _masked_softmax(_dot_nt(q, kcb) * scale - slope * dist_c.astype(F32), dist_c >= 0)
    oc_ref[0, 0] = _dot(p_c.astype(BF16), vc_ref[0, 0, 0].astype(BF16))
    p_sum = jnp.broadcast_to(jnp.sum(p_c, axis=0, keepdims=True), (SUBLANES, ncp))
    imp = _dot_exact(p_sum, ov_ref[...])[0:1]

    NP = SEL_PAD_S
    cur = T_SAMPLE // SEL_LEN
    j_row = lax.broadcasted_iota(jnp.int32, (1, NP), 1)
    forced = (j_row == 0) | (j_row == cur) | (j_row == cur - 1)
    impm = jnp.where(forced, jnp.inf, jnp.where(j_row <= cur, imp, -jnp.inf))
    sub = lax.broadcasted_iota(jnp.int32, (NP, NP), 0)
    lane = lax.broadcasted_iota(jnp.int32, (NP, NP), 1)
    impm_r = jnp.broadcast_to(impm, (NP, NP))
    rank = _top_rank(impm_r.T, impm_r, sub, lane, 0).astype(F32)
    rank_c = jnp.broadcast_to(rank, (NP, NP)).T
    hit = (rank_c == lane.astype(F32)) & (sub <= cur)
    idx_ref[0, 0] = jnp.sum(jnp.where(hit, sub, 0), axis=0, keepdims=True)

    kwb = kw_ref[0].astype(BF16)
    wlen = kwb.shape[0]
    wpos = T_SAMPLE - wlen + lax.broadcasted_iota(jnp.int32, (1, wlen), 1)
    dist_w = T_SAMPLE - wpos
    valid_w = (dist_w >= 0) & (dist_w <= WINDOW) & (wpos >= 0)
    carry = _flash_step(_flash_init(GRP), _dot_nt(q, kwb) * scale - slope * dist_w.astype(F32), valid_w,
                        vw_ref[0].astype(BF16))
    k_new = kwn_ref[0, 0, 0].astype(BF16).astype(F32)
    v_new = vwn_ref[0, 0, 0].astype(BF16).astype(F32)
    s_new = jnp.sum(q.astype(F32) * k_new, axis=1, keepdims=True) * scale
    _, l_w, acc_w = _flash_one(carry, s_new, v_new)
    ow_ref[0, 0] = acc_w / l_w


def _nsa_sample_cmp_win(slopes, q_s, kvc, ov, win2d, rows_s):
    G, D, DB = NSA_KV_HEADS, HEAD_DIM, DEC_BATCH
    ncp = kvc.shape[3]
    wlen = win2d.shape[1]
    o_spec = pl.BlockSpec((1, 1, GRP, D), lambda b, g, sl: (b, g, 0, 0))
    blk = 2 * ncp * D * 4 + ov.size * 4 + 2 * wlen * D * 4 + 4 * SEL_PAD_S * SEL_PAD_S * 4
    return pl.pallas_call(
        _nsa_sample_cmp_win_kernel,
        grid_spec=pltpu.PrefetchScalarGridSpec(
            num_scalar_prefetch=1,
            grid=(DB, G),
            in_specs=[pl.BlockSpec((1, 1, GRP, D), lambda b, g, sl: (b, g, 0, 0)),
                      pl.BlockSpec((1, 1, 1, ncp, D), lambda b, g, sl: (b, 0, g, 0, 0)),
                      pl.BlockSpec((1, 1, 1, ncp, D), lambda b, g, sl: (b, 1, g, 0, 0)),
                      pl.BlockSpec(ov.shape, lambda b, g, sl: (0, 0)),
                      pl.BlockSpec((1, wlen, D), lambda b, g, sl: (b, 0, g)),
                      pl.BlockSpec((1, wlen, D), lambda b, g, sl: (b, 0, G + g)),
                      pl.BlockSpec((1, 1, 1, 1, D), lambda b, g, sl: (b, 4, g, 0, 0)),
                      pl.BlockSpec((1, 1, 1, 1, D), lambda b, g, sl: (b, 5, g, 0, 0))],
            out_specs=[o_spec, o_spec, pl.BlockSpec((1, 1, 1, SEL_PAD_S), lambda b, g, sl: (b, g, 0, 0))],
        ),
        out_shape=[jax.ShapeDtypeStruct((DB, G, GRP, D), F32),
                   jax.ShapeDtypeStruct((DB, G, GRP, D), F32),
                   jax.ShapeDtypeStruct((DB, G, 1, SEL_PAD_S), jnp.int32)],
        compiler_params=_params(("parallel", "parallel"), blk),
        name="nsa_sample_cmp_win",
    )(slopes, q_s, kvc, kvc, ov, win2d, win2d, rows_s, rows_s)


def _nsa_sample_sel_kernel(pt_ref, idx_ref, slopes_ref, q_ref, ks_ref, vs_ref, ksn_ref, vsn_ref,
                           oc_ref, ow_ref, gl_ref, out_ref, m_scr, l_scr, acc_scr):
    b, g, k = pl.program_id(0), pl.program_id(1), pl.program_id(2)
    scale = HEAD_DIM ** -0.5
    q = q_ref[0, 0]
    slope = _slope_col(slopes_ref, g)
    j = idx_ref[(b * NSA_KV_HEADS + g) * KK_S + k]

    @pl.when(k == 0)
    def _():
        m0, l0, a0 = _flash_init(GRP)
        m_scr[...] = m0
        l_scr[...] = l0
        acc_scr[...] = a0

    carry = (m_scr[...], l_scr[...], acc_scr[...])

    @pl.when(j < NB_PAST)
    def _():
        dist = T_SAMPLE - (j * SEL_LEN + lax.broadcasted_iota(jnp.int32, (1, SEL_LEN), 1))
        s = _dot_nt(q, ks_ref[0].astype(BF16)) * scale - slope * dist.astype(F32)
        m1, l1, a1 = _flash_step(carry, s, dist >= 0, vs_ref[0].astype(BF16))
        m_scr[...] = m1
        l_scr[...] = l1
        acc_scr[...] = a1

    @pl.when(j >= NB_PAST)
    def _():
        k_new = ksn_ref[0, 0, 0].astype(BF16).astype(F32)
        v_new = vsn_ref[0, 0, 0].astype(BF16).astype(F32)
        s_new = jnp.sum(q.astype(F32) * k_new, axis=1, keepdims=True) * scale
        m1, l1, a1 = _flash_one(carry, s_new, v_new)
        m_scr[...] = m1
        l_scr[...] = l1
        acc_scr[...] = a1

    @pl.when(k == KK_S - 1)
    def _():
        gates = jax.nn.sigmoid(gl_ref[0, 0])
        o_s = acc_scr[...] / l_scr[...]
        out_ref[0, 0] = gates[0] * oc_ref[0, 0] + gates[1] * o_s + gates[2] * ow_ref[0, 0]


def _nsa_sample_sel(page_table_flat, idx_flat, slopes, q_s, cache2d, rows_s, o_c, o_w, gate_logits_s):
    G, D, DB = NSA_KV_HEADS, HEAD_DIM, DEC_BATCH
    n_pages = PAST_LEN // PAGE_SIZE
    bpp = PAGE_SIZE // SEL_LEN

    def blk_map(b, g, k, pt, idx, sl, *, slot):
        j = jnp.minimum(idx[(b * G + g) * KK_S + k], NB_PAST - 1)
        return (pt[b * n_pages + j // bpp], j % bpp, slot * G + g)

    o_spec = pl.BlockSpec((1, 1, GRP, D), lambda b, g, k, pt, idx, sl: (b, g, 0, 0))
    blk = 2 * SEL_LEN * D * 4 + 6 * GRP * D * 4
    return pl.pallas_call(
        _nsa_sample_sel_kernel,
        grid_spec=pltpu.PrefetchScalarGridSpec(
            num_scalar_prefetch=3,
            grid=(DB, G, KK_S),
            in_specs=[o_spec,
                      pl.BlockSpec((1, SEL_LEN, D), functools.partial(blk_map, slot=2)),
                      pl.BlockSpec((1, SEL_LEN, D), functools.partial(blk_map, slot=3)),
                      pl.BlockSpec((1, 1, 1, 1, D), lambda b, g, k, pt, idx, sl: (b, 2, g, 0, 0)),
                      pl.BlockSpec((1, 1, 1, 1, D), lambda b, g, k, pt, idx, sl: (b, 3, g, 0, 0)),
                      o_spec, o_spec,
                      pl.BlockSpec((1, 1, 3, GRP, 1), lambda b, g, k, pt, idx, sl: (b, g, 0, 0, 0))],
            out_specs=o_spec,
            scratch_shapes=[pltpu.VMEM((GRP, 1), F32), pltpu.VMEM((GRP, 1), F32), pltpu.VMEM((GRP, D), F32)],
        ),
        out_shape=jax.ShapeDtypeStruct((DB, G, GRP, D), F32),
        compiler_params=_params(("parallel", "parallel", "arbitrary"), blk),
        name="nsa_sample_sel",
    )(page_table_flat, idx_flat, slopes, q_s, cache2d, cache2d, rows_s, rows_s, o_c, o_w, gate_logits_s)


def _stack_rows(prompt_rows, sample_rows):
    pad = jnp.zeros((M_PAD - M_PROMPT - M_SAMPLE, prompt_rows.shape[1]), prompt_rows.dtype)
    return jnp.concatenate([prompt_rows, sample_rows.astype(prompt_rows.dtype), pad], axis=0)


def _ffn_block(x, g, w_gate_up, w_down, layer):
    h = _rmsnorm(x, g, BF16)
    return _matmul_fullk(_ffn_up(h, w_gate_up, layer), w_down, layer, D_MODEL, F32, residual=x, k_split=2)


def kernel(x_prompt, x_sample, cache_nsa_kv, page_table, state_win_kv, state_mlstm_C, state_mlstm_n, state_mlstm_m,
           g_mix, g_ffn, g_kv, g_final, w_in_a, b_gate_a, g_head_a, w_out_a, w_qg_b, w_out_b, w_kv,
           pe_cmp, w_cmp1, w_cmp2, w_gate_up, w_down):
    H, DK, DV = MLSTM_HEADS, MLSTM_QK_DIM, MLSTM_V_DIM
    G, D, DB = NSA_KV_HEADS, HEAD_DIM, DEC_BATCH
    hd = NSA_HEADS * D
    n_main = 2 * H * DK + 2 * H * DV
    x = _stack_rows(x_prompt.reshape(M_PROMPT, D_MODEL), x_sample.reshape(M_SAMPLE, D_MODEL))
    w_gate_a = w_in_a[:, :, n_main:]
    w_gate_b = w_qg_b[:, :, hd:]

    c_p, n_p, m_p, c_s, n_s, m_s = [], [], [], [], [], []
    for l in range(N_A):
        h, gates = _rmsnorm_with_gates(x, g_mix[l], w_gate_a[l], b_gate_a[l], True)
        z = _matmul_fullk(h, w_in_a, l, n_main, F32)
        gates_t = gates[:M_PROMPT, :2 * H].T.reshape(2, H, 1, M_PROMPT)
        hs_p, cp, np_, mp = _mlstm_prompt(z, gates_t, g_head_a[l])
        gs = gates[S0:S0 + DB, :2 * H]
        lane_b = lambda a: jnp.broadcast_to(a[:, :, None, None], (DB, H, 1, LANES))
        hs_s, cs, ns, ms = _mlstm_sample(z, lane_b(gs[:, :H]), lane_b(gs[:, H:]), lane_b(state_mlstm_m[l]),
                                         state_mlstm_C[l], state_mlstm_n[l].reshape(DB, H, 1, DK), g_head_a[l])
        hs = _stack_rows(hs_p, hs_s.reshape(DB, H * DV))
        x = _matmul_fullk(hs, w_out_a, l, D_MODEL, F32, residual=x)
        x = _ffn_block(x, g_ffn[l], w_gate_up, w_down, l)
        c_p.append(cp)
        n_p.append(np_.reshape(BATCH, H, DK))
        m_p.append(mp.reshape(BATCH, H))
        c_s.append(cs)
        n_s.append(ns.reshape(DB, H, DK))
        m_s.append(ms[:, :, 0, 0])

    rows = _matmul_fullk(_rmsnorm(x, g_kv, BF16), w_kv[None], 0, 6 * G * D, F32)
    rows_s = rows[S0:S0 + DB].reshape(DB, 6, G, 1, D)
    w1r = _compress_weights(w_cmp1)
    kvc_p = _compress_finish(_compress_proj_prompt(rows, w1r), pe_cmp, w_cmp1, w_cmp2)
    n_pool = cache_nsa_kv.shape[0]
    pt_flat = page_table.reshape(-1).astype(jnp.int32)
    cache2d = cache_nsa_kv.reshape(n_pool, PAGE_SIZE, 4 * G * D)
    kvc_s = _compress_finish(_compress_proj_sample(cache2d, pt_flat, w1r), pe_cmp, w_cmp1, w_cmp2)
    win2d = state_win_kv.reshape(DB, state_win_kv.shape[1], 2 * G * D)

    slopes = jnp.exp2(-8.0 * jnp.arange(1, NSA_HEADS + 1, dtype=F32) / NSA_HEADS)
    n_chunks_p = SEQ // CMP_STRIDE
    ov_p = _selection_overlap(n_chunks_p - CMP_R + 1, SEQ // SEL_LEN, n_chunks_p, LANES)
    n_chunks_s = (PAST_LEN + M_SAMPLE // DB) // CMP_STRIDE
    ov_s = _selection_overlap(n_chunks_s - CMP_R + 1, N_SEL_S, n_chunks_s, SEL_PAD_S)

    for l in range(N_B):
        h, gate_logits = _rmsnorm_with_gates(x, g_mix[N_A + l], w_gate_b[l], jnp.zeros((3 * NSA_HEADS,), F32),
                                             False)
        q = _matmul_fullk(h, w_qg_b, l, hd, BF16)
        gl = gate_logits[:, :3 * NSA_HEADS]
        gl_p = gl.reshape(M_PAD, G, 3 * GRP).transpose(1, 0, 2)
        att_p = _nsa_prompt(q, gl_p, kvc_p, rows, ov_p)
        q_s = q[S0:S0 + DB].reshape(DB, G, GRP, D)
        gl_s = gl[S0:S0 + DB].reshape(DB, G, GRP, 3).transpose(0, 1, 3, 2)[..., None]
        o_c, o_w, idx = _nsa_sample_cmp_win(slopes, q_s, kvc_s, ov_s, win2d, rows_s)
        idx_flat = idx[:, :, 0, :KK_S].reshape(-1)
        att_s = _nsa_sample_sel(pt_flat, idx_flat, slopes, q_s, cache2d, rows_s, o_c, o_w, gl_s)
        att = _stack_rows(att_p, att_s.reshape(DB, hd))
        x = _matmul_fullk(att, w_out_b, l, D_MODEL, F32, residual=x)
        x = _ffn_block(x, g_ffn[N_A + l], w_gate_up, w_down, N_A + l)

    y = _rmsnorm(x, g_final, F32)
    wp = min(WINDOW, SEQ)
    rows_p = rows[:M_PROMPT].reshape(BATCH, SEQ, 6, G, D)
    rows_s6 = rows[S0:S0 + DB].reshape(DB, 1, 6, G, D)
    wbuf = state_win_kv.shape[1]
    win_kv_sample = jnp.concatenate([state_win_kv, rows_s6[:, :, 4:]], axis=1)[:, -wbuf:]
    return (y[:M_PROMPT].reshape(BATCH, SEQ, D_MODEL), y[S0:S0 + DB].reshape(DB, 1, D_MODEL),
            rows_p[:, :, :4], rows_p[:, SEQ - wp:, 4:],
            jnp.stack(c_p), jnp.stack(n_p), jnp.stack(m_p),
            rows_s6[:, :, :4], win_kv_sample,
            jnp.stack(c_s), jnp.stack(n_s), jnp.stack(m_s))
```

```python
import functools

import numpy as np
import jax
import jax.numpy as jnp
from jax import lax
from jax.experimental import pallas as pl
from jax.experimental.pallas import tpu as pltpu

D_MODEL = 4096
BATCH = 4
SEQ = 2048
DEPTH = 4
DEC_BATCH = 8
DEC_SEQ = 1
PAST_LEN = 8192
PAGE_SIZE = 128

N_A = DEPTH // 2
N_B = DEPTH - N_A
MLSTM_HEADS = 8
MLSTM_QK_DIM = D_MODEL // 16
MLSTM_V_DIM = D_MODEL // MLSTM_HEADS
GATE_CAP = 15.0
HEAD_DIM = 128
NSA_HEADS = D_MODEL // HEAD_DIM
NSA_KV_HEADS = 4
GRP = NSA_HEADS // NSA_KV_HEADS
CMP_LEN = 32
CMP_STRIDE = 16
CMP_HID = HEAD_DIM
SEL_LEN = 64
SEL_TOP = 16
WINDOW = 512
D_FF = -(-8 * D_MODEL // (3 * 256)) * 256
EPS = 1e-6
NEG = -1e30

F32 = jnp.float32
BF16 = jnp.bfloat16

LANES = 128
SUBLANES = 8
VMEM_BYTES_V7X = 64 * 1024 * 1024


def _round_up(n, m):
    return -(-n // m) * m


def _pick_tile(n, cap, align=LANES):
    best = None
    for t in range(align, min(n, cap) + 1, align):
        if n % t == 0:
            best = t
    assert best is not None, (n, cap, align)
    return best


M_PROMPT = BATCH * SEQ
M_SAMPLE = DEC_BATCH * DEC_SEQ
S0 = M_PROMPT
M_PAD = _round_up(M_PROMPT + M_SAMPLE, LANES)
TM = _pick_tile(M_PAD, 2048)
TR = _pick_tile(M_PAD, 256)
TN_FULLK = 256
MLSTM_L = 128
TQ = 128
TKS = 1024
N_GATE_PAD = LANES
D_FF_PAD = _round_up(D_FF, 1024)
CMP_R = CMP_LEN // CMP_STRIDE
SAMPLE_PAGES_PER_STEP = 8
PERM_ROWS = LANES

assert DEC_SEQ == 1 and S0 % 16 == 0
assert SEQ % PERM_ROWS == 0 and PAGE_SIZE % PERM_ROWS == 0
assert (PERM_ROWS // CMP_STRIDE) & (PERM_ROWS // CMP_STRIDE - 1) == 0
assert SEQ % MLSTM_L == 0 and SEQ % TQ == 0 and SEQ % TKS == 0 and TKS % TQ == 0
assert PAST_LEN % PAGE_SIZE == 0 and PAGE_SIZE % SEL_LEN == 0 and PAGE_SIZE % CMP_STRIDE == 0
assert CMP_R == 2 and SEQ // SEL_LEN <= LANES


def _vmem_limit(block_bytes, single_bytes):
    return int(min(single_bytes + 2 * block_bytes + (12 << 20), VMEM_BYTES_V7X - (8 << 20)))


def _params(sem, block_bytes, single_bytes=0):
    return pltpu.CompilerParams(dimension_semantics=sem,
                                vmem_limit_bytes=_vmem_limit(block_bytes, single_bytes))


def _dot(a, b):
    return jnp.dot(a, b, preferred_element_type=F32)


def _dot_nt(a, b):
    return lax.dot_general(a, b, (((1,), (1,)), ((), ())), preferred_element_type=F32)


def _dot_tn(a, b):
    return lax.dot_general(a, b, (((0,), (0,)), ((), ())), preferred_element_type=F32)


def _dot_exact(a, b):
    return jnp.dot(a, b, precision=lax.Precision.HIGHEST, preferred_element_type=F32)


def _log_sigmoid(x):
    return jnp.minimum(x, 0.0) - jnp.log1p(jnp.exp(-jnp.abs(x)))


def _col_form(row, n):
    return jnp.broadcast_to(row, (n, n)).T


def _rms_kernel(x_ref, g_ref, o_ref):
    x = x_ref[...]
    y = x * lax.rsqrt(jnp.mean(x * x, axis=-1, keepdims=True) + EPS) * g_ref[...]
    o_ref[...] = y.astype(o_ref.dtype)


def _rmsnorm(x, g, out_dtype):
    m, d = x.shape
    blk = TR * d * (4 + jnp.dtype(out_dtype).itemsize)
    return pl.pallas_call(
        _rms_kernel,
        grid=(m // TR,),
        in_specs=[pl.BlockSpec((TR, d), lambda i: (i, 0)), pl.BlockSpec((1, d), lambda i: (0, 0))],
        out_specs=pl.BlockSpec((TR, d), lambda i: (i, 0)),
        out_shape=jax.ShapeDtypeStruct((m, d), out_dtype),
        compiler_params=_params(("parallel",), blk),
        name="rmsnorm",
    )(x, g.reshape(1, d))


def _rms_gate_kernel(x_ref, g_ref, w_ref, b_ref, h_ref, o_ref, *, mlstm_gates):
    x = x_ref[...]
    y = x * lax.rsqrt(jnp.mean(x * x, axis=-1, keepdims=True) + EPS) * g_ref[...]
    h_ref[...] = y.astype(h_ref.dtype)
    z = _dot_exact(y, w_ref[...]) + b_ref[...]
    if mlstm_gates:
        gl = GATE_CAP * jnp.tanh(z / GATE_CAP)
        lane = lax.broadcasted_iota(jnp.int32, z.shape, 1)
        z = jnp.where(lane < MLSTM_HEADS, gl, _log_sigmoid(gl))
    o_ref[...] = z


def _rmsnorm_with_gates(x, g, w_gate, b_gate, mlstm_gates):
    m, d = x.shape
    ng = w_gate.shape[1]
    wp = jnp.pad(w_gate, ((0, 0), (0, N_GATE_PAD - ng)))
    bp = jnp.pad(b_gate, (0, N_GATE_PAD - ng)).reshape(1, N_GATE_PAD)
    blk = TR * d * 6 + d * N_GATE_PAD * 4 + TR * N_GATE_PAD * 4
    return pl.pallas_call(
        functools.partial(_rms_gate_kernel, mlstm_gates=mlstm_gates),
        grid=(m // TR,),
        in_specs=[pl.BlockSpec((TR, d), lambda i: (i, 0)),
                  pl.BlockSpec((1, d), lambda i: (0, 0)),
                  pl.BlockSpec((d, N_GATE_PAD), lambda i: (0, 0)),
                  pl.BlockSpec((1, N_GATE_PAD), lambda i: (0, 0))],
        out_specs=[pl.BlockSpec((TR, d), lambda i: (i, 0)),
                   pl.BlockSpec((TR, N_GATE_PAD), lambda i: (i, 0))],
        out_shape=[jax.ShapeDtypeStruct((m, d), BF16), jax.ShapeDtypeStruct((m, N_GATE_PAD), F32)],
        compiler_params=_params(("parallel",), blk),
        name="rmsnorm_gates",
    )(x, g.reshape(1, d), wp, bp)


def _mm_fullk_kernel(*refs, has_res, rows_valid):
    x_ref, w_ref, o_ref = refs[0], refs[1], refs[-1]
    w = w_ref[...]
    if rows_valid < w.shape[0]:
        w = jnp.where(lax.broadcasted_iota(jnp.int32, w.shape, 0) < rows_valid, w, 0.0)
    r = _dot(x_ref[...], w.astype(BF16))
    if has_res:
        r = r + refs[2][...]
    o_ref[...] = r.astype(o_ref.dtype)


def _row_tile_spec(kb, kh=0):
    return pl.BlockSpec((TM, kb), lambda i, j: (i, kh), pipeline_mode=pl.Buffered(1))


def _matmul_fullk(x, w, layer, n_cols, out_dtype, residual=None, k_split=1):
    m, kd = x.shape
    tn = TN_FULLK
    kb = kd // k_split
    k_rows = w.shape[1]
    assert n_cols % tn == 0 and kd % k_split == 0 and kb % LANES == 0 and w.dtype == F32
    assert (k_split - 1) * kb < k_rows <= kd
    out = residual
    for kh in range(k_split):
        last = kh == k_split - 1
        dt = out_dtype if last else F32
        osz = jnp.dtype(dt).itemsize
        in_specs = [_row_tile_spec(kb, kh), pl.BlockSpec((None, kb, tn), lambda i, j, kh=kh: (layer, kh, j))]
        args = [x, w]
        streamed = kb * tn * 4 + TM * tn * osz
        if out is not None:
            in_specs.append(pl.BlockSpec((TM, tn), lambda i, j: (i, j)))
            args.append(out)
            streamed += TM * tn * 4
        out = pl.pallas_call(
            functools.partial(_mm_fullk_kernel, has_res=out is not None, rows_valid=min(kb, k_rows - kh * kb)),
            grid=(m // TM, n_cols // tn),
            in_specs=in_specs,
            out_specs=pl.BlockSpec((TM, tn), lambda i, j: (i, j)),
            out_shape=jax.ShapeDtypeStruct((m, n_cols), dt),
            compiler_params=_params(("parallel", "arbitrary"), streamed, single_bytes=TM * kb * 2),
            name="matmul_fullk",
        )(*args)
    return out


def _mm_fullk_nt_kernel(x_ref, wt_ref, o_ref):
    o_ref[...] = _dot_nt(x_ref[...], wt_ref[...].astype(BF16)).astype(o_ref.dtype)


def _matmul_fullk_nt(x, wt, layer, n_cols, out_dtype):
    m, kd = x.shape
    tn = TN_FULLK
    assert n_cols % tn == 0 and wt.shape[2] == kd and wt.dtype == F32
    streamed = kd * tn * 4 + TM * tn * jnp.dtype(out_dtype).itemsize
    return pl.pallas_call(
        _mm_fullk_nt_kernel,
        grid=(m // TM, n_cols // tn),
        in_specs=[_row_tile_spec(kd), pl.BlockSpec((None, tn, kd), lambda i, j: (layer, j, 0))],
        out_specs=pl.BlockSpec((TM, tn), lambda i, j: (i, j)),
        out_shape=jax.ShapeDtypeStruct((m, n_cols), out_dtype),
        compiler_params=_params(("parallel", "arbitrary"), streamed, single_bytes=TM * kd * 2),
        name="matmul_fullk_nt",
    )(x, wt)


def _ffn_up_kernel(x_ref, wg_ref, wu_ref, o_ref, *, n_real):
    j = pl.program_id(1)

    @pl.when(j < n_real)
    def _():
        x = x_ref[...]
        gate = _dot(x, wg_ref[...].astype(BF16))
        up = _dot(x, wu_ref[...].astype(BF16))
        o_ref[...] = (jax.nn.silu(gate) * up).astype(o_ref.dtype)

    @pl.when(j >= n_real)
    def _():
        o_ref[...] = jnp.zeros_like(o_ref)


def _ffn_up(x, w_gate_up, layer):
    m, kd = x.shape
    tn = TN_FULLK
    assert D_FF % tn == 0 and D_FF_PAD % tn == 0
    n_real = D_FF // tn
    col = lambda j: jnp.minimum(j, n_real - 1)
    streamed = 2 * kd * tn * 4 + TM * tn * 2
    return pl.pallas_call(
        functools.partial(_ffn_up_kernel, n_real=n_real),
        grid=(m // TM, D_FF_PAD // tn),
        in_specs=[_row_tile_spec(kd),
                  pl.BlockSpec((None, kd, tn), lambda i, j: (layer, 0, col(j))),
                  pl.BlockSpec((None, kd, tn), lambda i, j: (layer, 0, n_real + col(j)))],
        out_specs=pl.BlockSpec((TM, tn), lambda i, j: (i, j)),
        out_shape=jax.ShapeDtypeStruct((m, D_FF_PAD), BF16),
        compiler_params=_params(("parallel", "arbitrary"), streamed, single_bytes=TM * kd * 2),
        name="ffn_up",
    )(x, w_gate_up, w_gate_up)


def _mlstm_prompt_kernel(q_ref, k_ref, v_ref, o_ref, g_ref, gh_ref, hs_ref, c_ref, n_ref, m_ref):
    L, DK, DV = MLSTM_L, MLSTM_QK_DIM, MLSTM_V_DIM

    @pl.when(pl.program_id(1) == 0)
    def _():
        c_ref[...] = jnp.zeros_like(c_ref)
        n_ref[...] = jnp.zeros_like(n_ref)
        m_ref[...] = jnp.zeros_like(m_ref)

    t_idx = lax.broadcasted_iota(jnp.int32, (L, L), 0)
    s_idx = lax.broadcasted_iota(jnp.int32, (L, L), 1)
    for hd in range(MLSTM_HEADS):
        ig_row = g_ref[0, hd]
        lf_row = g_ref[1, hd]
        m_prev = m_ref[0, hd]
        lf_c = _col_form(lf_row, L)
        b_row = jnp.sum(jnp.where(t_idx <= s_idx, lf_c, 0.0), axis=0, keepdims=True)
        b_r = jnp.broadcast_to(b_row, (L, L))
        b_c = b_r.T
        ig_r = jnp.broadcast_to(ig_row, (L, L))
        log_d = jnp.where(s_idx <= t_idx, b_c - b_r + ig_r, -jnp.inf)
        b_col = b_c[:, 0:1]
        inter = b_col + m_prev
        m_t = jnp.maximum(jnp.max(log_d, axis=1, keepdims=True), inter)
        d = jnp.exp(log_d - m_t)
        w_inter = jnp.exp(inter - m_t)

        q = q_ref[:, hd * DK:(hd + 1) * DK] * (DK ** -0.5)
        k = k_ref[:, hd * DK:(hd + 1) * DK]
        qb = q.astype(BF16)
        kb = k.astype(BF16)
        vb = v_ref[:, hd * DV:(hd + 1) * DV].astype(BF16)
        s = _dot_nt(qb, kb) * d
        c_prev = c_ref[0, hd]
        n_prev = n_ref[0, hd]
        num = w_inter * _dot(qb, c_prev.astype(BF16)) + _dot(s.astype(BF16), vb)
        den = w_inter * jnp.sum(q * n_prev, axis=1, keepdims=True) + jnp.sum(s, axis=1, keepdims=True)
        h = num / jnp.maximum(jnp.abs(den), jnp.exp(-m_t))
        hn = h * lax.rsqrt(jnp.mean(h * h, axis=1, keepdims=True) + EPS)
        cols = slice(hd * DV, (hd + 1) * DV)
        hs_ref[:, cols] = (hn * gh_ref[:, cols] * jax.nn.sigmoid(o_ref[:, cols])).astype(hs_ref.dtype)

        m_new = m_t[L - 1:L, :]
        b_last = b_row[:, L - 1:L]
        ig_col = ig_r.T[:, 0:1]
        decay = jnp.exp(b_last - b_col + ig_col - m_new)
        carry = jnp.exp(b_last + m_prev - m_new)
        kd = k * decay
        c_ref[0, hd] = carry * c_prev + _dot_tn(kd.astype(BF16), vb)
        n_ref[0, hd] = carry * n_prev + jnp.sum(kd, axis=0, keepdims=True)
        m_ref[0, hd] = m_new


def _mlstm_prompt(z, gates_t, g_head):
    H, DK, DV, L = MLSTM_HEADS, MLSTM_QK_DIM, MLSTM_V_DIM, MLSTM_L
    nc = SEQ // L
    assert (H * DV) % (H * DK) == 0
    v0 = (2 * H * DK) // (H * DV)
    row = lambda b, c: b * nc + c
    blk = L * (2 * H * DK + 2 * H * DV) * 4 + L * H * DV * 2 + H * DK * DV * 4
    return pl.pallas_call(
        _mlstm_prompt_kernel,
        grid=(BATCH, nc),
        in_specs=[pl.BlockSpec((L, H * DK), lambda b, c: (row(b, c), 0)),
                  pl.BlockSpec((L, H * DK), lambda b, c: (row(b, c), 1)),
                  pl.BlockSpec((L, H * DV), lambda b, c: (row(b, c), v0)),
                  pl.BlockSpec((L, H * DV), lambda b, c: (row(b, c), v0 + 1)),
                  pl.BlockSpec((2, H, 1, L), lambda b, c: (0, 0, 0, row(b, c))),
                  pl.BlockSpec((1, H * DV), lambda b, c: (0, 0))],
        out_specs=[pl.BlockSpec((L, H * DV), lambda b, c: (row(b, c), 0)),
                   pl.BlockSpec((1, H, DK, DV), lambda b, c: (b, 0, 0, 0)),
                   pl.BlockSpec((1, H, 1, DK), lambda b, c: (b, 0, 0, 0)),
                   pl.BlockSpec((1, H, 1, 1), lambda b, c: (b, 0, 0, 0))],
        out_shape=[jax.ShapeDtypeStruct((M_PROMPT, H * DV), BF16),
                   jax.ShapeDtypeStruct((BATCH, H, DK, DV), F32),
                   jax.ShapeDtypeStruct((BATCH, H, 1, DK), F32),
                   jax.ShapeDtypeStruct((BATCH, H, 1, 1), F32)],
        compiler_params=_params(("parallel", "arbitrary"), blk),
        name="mlstm_prompt",
    )(z, z, z, z, gates_t, g_head.reshape(1, H * DV))


def _mlstm_sample_kernel(q_ref, k_ref, v_ref, o_ref, ig_ref, lf_ref, m0_ref, c0_ref, n0_ref, gh_ref,
                         hs_ref, c_ref, n_ref, m_ref):
    DK = MLSTM_QK_DIM
    b = pl.program_id(0)
    q = q_ref[pl.ds(b, 1), :] * (DK ** -0.5)
    k = k_ref[pl.ds(b, 1), :]
    v = v_ref[pl.ds(b, 1), :]
    og = o_ref[pl.ds(b, 1), :]
    ig = ig_ref[0, 0]
    lf = lf_ref[0, 0]
    m0 = m0_ref[0, 0]
    m_t = jnp.maximum(ig, lf + m0)
    d = jnp.exp(ig - m_t)[:, 0:1]
    w = jnp.exp(lf + m0 - m_t)[:, 0:1]
    c0 = c0_ref[0, 0]
    n0 = n0_ref[0, 0]
    q_col = _col_form(q, DK)[:, 0:1]
    k_col = _col_form(k, DK)[:, 0:1]
    s = jnp.sum(q * k, axis=1, keepdims=True) * d
    num = w * jnp.sum(q_col * c0, axis=0, keepdims=True) + s * v
    den = w * jnp.sum(q * n0, axis=1, keepdims=True) + s
    h = num / jnp.maximum(jnp.abs(den), jnp.exp(-m_t)[:, 0:1])
    hn = h * lax.rsqrt(jnp.mean(h * h, axis=1, keepdims=True) + EPS)
    hs_ref[0] = hn * gh_ref[...] * jax.nn.sigmoid(og)
    c_ref[0, 0] = w * c0 + d * (k_col * v)
    n_ref[0, 0] = w * n0 + d * k
    m_ref[0, 0] = m_t


def _mlstm_sample(z, ig_b, lf_b, m_b, c0, n0, g_head):
    H, DK, DV = MLSTM_HEADS, MLSTM_QK_DIM, MLSTM_V_DIM
    DB = DEC_BATCH
    r0 = S0 // SUBLANES
    v0 = (2 * H * DK) // DV
    sc = pl.BlockSpec((1, 1, 1, LANES), lambda b, h: (b, h, 0, 0))
    blk = 2 * DK * DV * 4 + SUBLANES * (2 * DK + 2 * DV) * 4
    return pl.pallas_call(
        _mlstm_sample_kernel,
        grid=(DB, H),
        in_specs=[pl.BlockSpec((SUBLANES, DK), lambda b, h: (r0, h)),
                  pl.BlockSpec((SUBLANES, DK), lambda b, h: (r0, H + h)),
                  pl.BlockSpec((SUBLANES, DV), lambda b, h: (r0, v0 + h)),
                  pl.BlockSpec((SUBLANES, DV), lambda b, h: (r0, v0 + H + h)),
                  sc, sc, sc,
                  pl.BlockSpec((1, 1, DK, DV), lambda b, h: (b, h, 0, 0)),
                  pl.BlockSpec((1, 1, 1, DK), lambda b, h: (b, h, 0, 0)),
                  pl.BlockSpec((1, DV), lambda b, h: (0, h))],
        out_specs=[pl.BlockSpec((1, 1, DV), lambda b, h: (b, 0, h)),
                   pl.BlockSpec((1, 1, DK, DV), lambda b, h: (b, h, 0, 0)),
                   pl.BlockSpec((1, 1, 1, DK), lambda b, h: (b, h, 0, 0)),
                   sc],
        out_shape=[jax.ShapeDtypeStruct((DB, 1, H * DV), F32),
                   jax.ShapeDtypeStruct((DB, H, DK, DV), F32),
                   jax.ShapeDtypeStruct((DB, H, 1, DK), F32),
                   jax.ShapeDtypeStruct((DB, H, 1, LANES), F32)],
        compiler_params=_params(("parallel", "parallel"), blk),
        name="mlstm_sample",
    )(z, z, z, z, ig_b, lf_b, m_b, c0, n0, g_head.reshape(1, H * DV))


def _compress_proj_kernel(*refs, n_in, groups_per_in, lead):
    x_refs = refs[-(n_in + 3):-3]
    w_ref, p_ref, xp_scr = refs[-3], refs[-2], refs[-1]
    G, D = NSA_KV_HEADS, HEAD_DIM
    cpg = PERM_ROWS // CMP_STRIDE
    nch = n_in * groups_per_in * cpg
    i_idx = lax.broadcasted_iota(jnp.int32, (PERM_ROWS, PERM_ROWS), 0)
    j_idx = lax.broadcasted_iota(jnp.int32, (PERM_ROWS, PERM_ROWS), 1)
    src_row = (i_idx & (cpg - 1)) * CMP_STRIDE + (i_idx >> (cpg.bit_length() - 1))
    perm = (j_idx == src_row).astype(BF16)
    for i, xr in enumerate(x_refs):
        for p in range(groups_per_in):
            rows = pl.ds(p * PERM_ROWS, PERM_ROWS)
            xg = xr[0, rows, :] if lead else xr[rows, :]
            xg = _dot(perm, xg.astype(BF16))
            ch0 = (i * groups_per_in + p) * cpg
            for c in range(CMP_STRIDE):
                xp_scr[c, ch0:ch0 + cpg, :] = xg[c * cpg:(c + 1) * cpg, :]
    for s in range(2):
        acc = jnp.zeros((G * nch, CMP_R * CMP_HID), F32)
        for c in range(CMP_STRIDE):
            xs = jnp.concatenate([xp_scr[c, :, (s * G + g) * D:(s * G + g + 1) * D] for g in range(G)], axis=0)
            acc = acc + _dot(xs.astype(BF16), w_ref[s, c])
        for g in range(G):
            p_ref[0, s, g] = acc[g * nch:(g + 1) * nch]


def _compress_weights(w_cmp1):
    w = w_cmp1.reshape(2, CMP_R, CMP_STRIDE, HEAD_DIM, CMP_HID)
    return w.transpose(0, 2, 3, 1, 4).reshape(2, CMP_STRIDE, HEAD_DIM, CMP_R * CMP_HID).astype(BF16)


def _compress_proj_prompt(rows, w1r):
    G, D = NSA_KV_HEADS, HEAD_DIM
    nch = SEQ // CMP_STRIDE
    pw = CMP_R * CMP_HID
    blk = SEQ * 2 * G * D * 4 + w1r.size * 2 + 2 * G * nch * pw * 4 + SEQ * 2 * G * D * 4
    return pl.pallas_call(
        functools.partial(_compress_proj_kernel, n_in=1, groups_per_in=SEQ // PERM_ROWS, lead=False),
        grid=(BATCH,),
        in_specs=[pl.BlockSpec((SEQ, 2 * G * D), lambda b: (b, 0)),
                  pl.BlockSpec(w1r.shape, lambda b: (0, 0, 0, 0))],
        out_specs=pl.BlockSpec((1, 2, G, nch, pw), lambda b: (b, 0, 0, 0, 0)),
        out_shape=jax.ShapeDtypeStruct((BATCH, 2, G, nch, pw), F32),
        scratch_shapes=[pltpu.VMEM((CMP_STRIDE, nch, 2 * G * D), F32)],
        compiler_params=_params(("parallel",), blk),
        name="compress_proj_prompt",
    )(rows, w1r)


def _compress_proj_sample(cache2d, page_table_flat, w1r):
    G, D = NSA_KV_HEADS, HEAD_DIM
    n_pages = PAST_LEN // PAGE_SIZE
    pps = SAMPLE_PAGES_PER_STEP
    assert n_pages % pps == 0
    cpi = PAGE_SIZE // CMP_STRIDE
    nch_step = pps * cpi
    pw = CMP_R * CMP_HID

    def page_map(b, j, pt, *, i):
        return (pt[b * n_pages + j * pps + i], 0, 0)

    in_specs = [pl.BlockSpec((1, PAGE_SIZE, 2 * G * D), functools.partial(page_map, i=i)) for i in range(pps)]
    in_specs.append(pl.BlockSpec(w1r.shape, lambda b, j, pt: (0, 0, 0, 0)))
    blk = 2 * pps * PAGE_SIZE * 2 * G * D * 4 + w1r.size * 2 + 2 * G * nch_step * pw * 4
    return pl.pallas_call(
        functools.partial(_compress_proj_kernel, n_in=pps, groups_per_in=PAGE_SIZE // PERM_ROWS, lead=True),
        grid_spec=pltpu.PrefetchScalarGridSpec(
            num_scalar_prefetch=1,
            grid=(DEC_BATCH, n_pages // pps),
            in_specs=in_specs,
            out_specs=pl.BlockSpec((1, 2, G, nch_step, pw), lambda b, j, pt: (b, 0, 0, j, 0)),
            scratch_shapes=[pltpu.VMEM((CMP_STRIDE, nch_step, 2 * G * D), F32)],
        ),
        out_shape=jax.ShapeDtypeStruct((DEC_BATCH, 2, G, n_pages * cpi, pw), F32),
        compiler_params=_params(("parallel", "arbitrary"), blk),
        name="compress_proj_sample",
    )(page_table_flat, *([cache2d] * pps), w1r)


def _compress_finish_kernel(p_ref, pe_ref, w1_ref, w2_ref, o_ref):
    nch = p_ref.shape[3]
    pe = jnp.broadcast_to(pe_ref[0], (SUBLANES, pe_ref.shape[2])).astype(BF16)
    pe_term = _dot(pe, w1_ref[0].astype(BF16))[0:1]
    w2 = w2_ref[0].astype(BF16)
    for g in range(NSA_KV_HEADS):
        p = p_ref[0, 0, g]
        nxt = pltpu.roll(p[:, CMP_HID:], shift=nch - 1, axis=0)
        hidden = p[:, :CMP_HID] + nxt + pe_term
        o_ref[0, 0, g] = _dot(jax.nn.gelu(hidden).astype(BF16), w2)


def _compress_finish(p, pe_cmp, w_cmp1, w_cmp2):
    nb, _, G, nch, pw = p.shape
    pe = pe_cmp.reshape(2, 1, CMP_LEN * HEAD_DIM)
    blk = G * nch * (pw + HEAD_DIM) * 4 + CMP_LEN * HEAD_DIM * (CMP_HID + 1) * 4
    return pl.pallas_call(
        _compress_finish_kernel,
        grid=(nb, 2),
        in_specs=[pl.BlockSpec((1, 1, G, nch, pw), lambda b, s: (b, s, 0, 0, 0)),
                  pl.BlockSpec((1, 1, CMP_LEN * HEAD_DIM), lambda b, s: (s, 0, 0)),
                  pl.BlockSpec((1, CMP_LEN * HEAD_DIM, CMP_HID), lambda b, s: (s, 0, 0)),
                  pl.BlockSpec((1, CMP_HID, HEAD_DIM), lambda b, s: (s, 0, 0))],
        out_specs=pl.BlockSpec((1, 1, G, nch, HEAD_DIM), lambda b, s: (b, s, 0, 0, 0)),
        out_shape=jax.ShapeDtypeStruct((nb, 2, G, nch, HEAD_DIM), F32),
        compiler_params=_params(("parallel", "parallel"), blk),
        name="compress_finish",
    )(p, pe, w_cmp1, w_cmp2)


def _selection_overlap(n_cmp, n_sel, rows, cols):
    start = np.arange(n_cmp)[:, None] * CMP_STRIDE
    end = start + CMP_LEN - 1
    j = np.arange(n_sel)[None, :]
    ov = ((start <= (j + 1) * SEL_LEN - 1) & (end >= j * SEL_LEN)).astype(np.float32)
    out = np.zeros((rows, cols), np.float32)
    out[:n_cmp, :n_sel] = ov
    return jnp.asarray(out)


def _masked_softmax(s, valid):
    sm = jnp.where(valid, s, NEG)
    e = jnp.exp(sm - jnp.max(sm, axis=-1, keepdims=True))
    return jnp.where(valid, e / jnp.sum(e, axis=-1, keepdims=True), 0.0)


def _flash_step(carry, s, valid, vb):
    m, l, acc = carry
    sm = jnp.where(valid, s, NEG)
    m_new = jnp.maximum(m, jnp.max(sm, axis=-1, keepdims=True))
    alpha = jnp.exp(m - m_new)
    e = jnp.exp(sm - m_new)
    l = alpha * l + jnp.sum(e, axis=-1, keepdims=True)
    acc = alpha * acc + _dot(jnp.where(valid, e, 0.0).astype(BF16), vb)
    return m_new, l, acc


def _flash_one(carry, s_new, v_new):
    m, l, acc = carry
    m_new = jnp.maximum(m, s_new)
    alpha = jnp.exp(m - m_new)
    e = jnp.exp(s_new - m_new)
    return m_new, alpha * l + e, alpha * acc + e * v_new


def _flash_init(rows):
    return (jnp.full((rows, 1), -jnp.inf, F32), jnp.zeros((rows, 1), F32), jnp.zeros((rows, HEAD_DIM), F32))


def _top_rank(impm_c, impm_r, jp_idx, j_idx, axis):
    before = (impm_c > impm_r) | ((impm_c == impm_r) & (jp_idx < j_idx))
    return jnp.sum(before.astype(jnp.int32), axis=axis, keepdims=True)


FEAT_SIG = 96
FEAT_SIG_FULL = FEAT_SIG + 9
MASK_BIG = 2.0 ** 100
POS_RADIX = 256
LOG2E = 1.4426950408889634
assert SEQ // SEL_LEN <= FEAT_SIG and SEQ >= WINDOW + TQ


def _key_features(pos, block):
    n = pos.shape[0]
    kf = np.zeros((n, LANES), np.float32)
    if block is not None:
        kf[np.arange(n), block] = -MASK_BIG
    kf[:, FEAT_SIG:FEAT_SIG + 3] = (pos % POS_RADIX)[:, None]
    kf[:, FEAT_SIG + 3:FEAT_SIG + 6] = (pos // POS_RADIX)[:, None]
    kf[:, FEAT_SIG + 6:FEAT_SIG + 9] = 1.0
    return jnp.asarray(kf, BF16)


def _sigma_table():
    slopes = np.exp2(-8.0 * np.arange(1, NSA_HEADS + 1, dtype=np.float32) / NSA_HEADS).astype(np.float32)
    sigma = (slopes / np.float32(HEAD_DIM ** -0.5)).astype(np.float32)
    bf = lambda a: np.asarray(np.asarray(a, dtype=BF16), dtype=np.float32)
    s1 = bf(sigma)
    s2 = bf(sigma - s1)
    s3 = bf(sigma - s1 - s2)
    tab = np.zeros((NSA_HEADS, LANES), np.float32)
    for i, s in enumerate((s1, s2, s3)):
        tab[:, FEAT_SIG + i] = s
        tab[:, FEAT_SIG + 3 + i] = s * POS_RADIX
    tab[:, FEAT_SIG_FULL] = sigma
    return jnp.asarray(tab)


def _nsa_prompt_kernel(q_ref, gl_ref, sig_ref, kc_ref, vc_ref, ks_ref, vs_ref, kw_ref, vw_ref,
                       ov_ref, kfc_ref, kfs_ref, kfw_ref, out_ref,
                       qa_scr, kca_scr, vcb_scr, ksa_scr, vsb_scr, kwa_scr, vwb_scr, m_scr, acc_scr):
    qi = pl.program_id(2)
    D = HEAD_DIM
    c2 = (D ** -0.5) * LOG2E
    n_sel = SEQ // SEL_LEN
    qs = qi * TQ
    tpos = qs + lax.broadcasted_iota(jnp.int32, (TQ, 1), 0)
    tposf = tpos.astype(F32)
    lane = lax.broadcasted_iota(jnp.int32, (1, LANES), 1)
    heads = [slice(r * TQ, (r + 1) * TQ) for r in range(GRP)]

    @pl.when(qi == 0)
    def _():
        kca_scr[:, 0:D] = kc_ref[0, 0, 0].astype(BF16)
        kca_scr[:, D:2 * D] = kfc_ref[...]
        vcb_scr[...] = vc_ref[0, 0, 0].astype(BF16)
        ksa_scr[:, 0:D] = ks_ref[...].astype(BF16)
        ksa_scr[:, D:2 * D] = kfs_ref[...]
        kwa_scr[:, 0:D] = kw_ref[...].astype(BF16)
        kwa_scr[:, D:2 * D] = kfw_ref[...]
        ones_col = jnp.where(lax.broadcasted_iota(jnp.int32, (SEQ, D), 1) == 0, 1.0, 0.0).astype(BF16)
        vsb_scr[:, 0:D] = vs_ref[...].astype(BF16)
        vsb_scr[:, D:2 * D] = ones_col
        vwb_scr[:, 0:D] = vw_ref[...].astype(BF16)
        vwb_scr[:, D:2 * D] = ones_col

    def write_features(nsel):
        for r in range(GRP):
            sig_row = sig_ref[r:r + 1, :]
            c = -sig_row[:, FEAT_SIG_FULL:FEAT_SIG_FULL + 1] * tposf
            c_hi = c.astype(BF16).astype(F32)
            c_mid = (c - c_hi).astype(BF16).astype(F32)
            c_lo = c - c_hi - c_mid
            f = jnp.where(lane == FEAT_SIG + 6, c_hi,
                          jnp.where(lane == FEAT_SIG + 7, c_mid, jnp.where(lane == FEAT_SIG + 8, c_lo, sig_row)))
            if nsel is not None:
                f = jnp.where(lane < n_sel, nsel, f)
            qa_scr[heads[r], D:2 * D] = f.astype(BF16)

    for r in range(GRP):
        qa_scr[heads[r], 0:D] = q_ref[:, r * D:(r + 1) * D]
    write_features(None)

    ncp = kca_scr.shape[0]
    cpos = lax.broadcasted_iota(jnp.int32, (1, ncp), 1) * CMP_STRIDE + (CMP_LEN - 1)
    valid_c = cpos <= tpos
    s_c = _dot_nt(qa_scr[...], kca_scr[...])
    p_sum = jnp.zeros((TQ, ncp), F32)
    p_c = []
    for r in range(GRP):
        sm = jnp.where(valid_c, s_c[heads[r]], NEG)
        e = jnp.exp2((sm - jnp.max(sm, axis=-1, keepdims=True)) * c2)
        p = jnp.where(valid_c, e / jnp.sum(e, axis=-1, keepdims=True), 0.0)
        p_c.append(p.astype(BF16))
        p_sum = p_sum + p
    o_c = _dot(jnp.concatenate(p_c, axis=0), vcb_scr[...])
    imp = _dot_exact(p_sum, ov_ref[...])

    cur = tpos // SEL_LEN
    forced = (lane == 0) | (lane == cur) | (lane == cur - 1)
    impm = jnp.where(forced, jnp.inf, jnp.where(lane <= cur, imp, -jnp.inf))
    rank = jnp.zeros((TQ, LANES), jnp.int32)
    for jp in range(n_sel):
        cj = impm[:, jp:jp + 1]
        rank = rank + ((cj > impm) | ((cj == impm) & (jp < lane))).astype(jnp.int32)
    selected = (rank < min(SEL_TOP, n_sel)) & (lane <= cur)
    write_features(jnp.where(selected, 0.0, 1.0))

    m_scr[...] = jnp.full(m_scr.shape, -jnp.inf, F32)
    acc_scr[...] = jnp.zeros(acc_scr.shape, F32)

    def sel_tile(kt, bias):
        k0 = pl.multiple_of(kt * TKS, TKS)
        s_all = _dot_nt(qa_scr[...], ksa_scr[pl.ds(k0, TKS), :])
        p_all = []
        for r in range(GRP):
            s = s_all[heads[r]]
            if bias is not None:
                s = s + bias
            m_old = m_scr[heads[r]]
            m_new = jnp.maximum(m_old, jnp.max(s, axis=-1, keepdims=True))
            alpha = jnp.exp2((m_old - m_new) * c2)
            e = jnp.exp2((s - m_new) * c2)
            m_scr[heads[r]] = m_new
            acc_scr[heads[r]] = alpha * acc_scr[heads[r]]
            p_all.append(e.astype(BF16))
        acc_scr[...] += _dot(jnp.concatenate(p_all, axis=0), vsb_scr[pl.ds(k0, TKS), :])

    n_past = qs // TKS

    def past_tile(kt, carry):
        sel_tile(kt, None)
        return carry

    lax.fori_loop(0, n_past, past_tile, 0)
    spos_d = n_past * TKS + lax.broadcasted_iota(jnp.int32, (1, TKS), 1)
    sel_tile(n_past, jnp.where(spos_d <= tpos, 0.0, -MASK_BIG))
    o_s = acc_scr[:, 0:D] / acc_scr[:, D:D + 1]

    wk = WINDOW + TQ
    w0 = pl.multiple_of(jnp.maximum(qs - WINDOW, 0), TQ)
    dist_w = tpos - (w0 + lax.broadcasted_iota(jnp.int32, (1, wk), 1))
    bias_w = jnp.where((dist_w >= 0) & (dist_w <= WINDOW), 0.0, -MASK_BIG)
    s_w = _dot_nt(qa_scr[...], kwa_scr[pl.ds(w0, wk), :])
    p_w = []
    for r in range(GRP):
        s = s_w[heads[r]] + bias_w
        p_w.append(jnp.exp2((s - jnp.max(s, axis=-1, keepdims=True)) * c2).astype(BF16))
    pv_w = _dot(jnp.concatenate(p_w, axis=0), vwb_scr[pl.ds(w0, wk), :])
    o_w = pv_w[:, 0:D] / pv_w[:, D:D + 1]

    for r in range(GRP):
        gates = jax.nn.sigmoid(gl_ref[0, :, 3 * r:3 * r + 3])
        out = gates[:, 0:1] * o_c[heads[r]] + gates[:, 1:2] * o_s[heads[r]] + gates[:, 2:3] * o_w[heads[r]]
        out_ref[:, r * D:(r + 1) * D] = out.astype(out_ref.dtype)


def _nsa_prompt(q, gate_logits, kvc, rows, ov):
    G, D = NSA_KV_HEADS, HEAD_DIM
    nq = SEQ // TQ
    ncp = kvc.shape[3]
    R = GRP * TQ
    spos = np.arange(SEQ)
    kf_cmp = _key_features(np.arange(ncp) * CMP_STRIDE + (CMP_LEN - 1), None)
    kf_sel = _key_features(spos, spos // SEL_LEN)
    kf_win = _key_features(spos, None)
    row = lambda b, g, qi: b * nq + qi
    kv_spec = lambda slot: pl.BlockSpec((SEQ, D), lambda b, g, qi: (b, slot * G + g))
    const = lambda a: pl.BlockSpec(a.shape, lambda b, g, qi: (0, 0))
    blk = (TQ * GRP * D * 4 + 2 * ncp * D * 4 + 4 * SEQ * D * 4 + 3 * SEQ * D * 2 + SEQ * D * 6 + R * D * 16
           + R * (TKS + WINDOW + TQ) * 6)
    return pl.pallas_call(
        _nsa_prompt_kernel,
        grid=(BATCH, G, nq),
        in_specs=[pl.BlockSpec((TQ, GRP * D), lambda b, g, qi: (row(b, g, qi), g)),
                  pl.BlockSpec((1, TQ, 3 * GRP), lambda b, g, qi: (g, row(b, g, qi), 0)),
                  pl.BlockSpec((GRP, LANES), lambda b, g, qi: (g, 0)),
                  pl.BlockSpec((1, 1, 1, ncp, D), lambda b, g, qi: (b, 0, g, 0, 0)),
                  pl.BlockSpec((1, 1, 1, ncp, D), lambda b, g, qi: (b, 1, g, 0, 0)),
                  kv_spec(2), kv_spec(3), kv_spec(4), kv_spec(5),
                  const(ov), const(kf_cmp), const(kf_sel), const(kf_win)],
        out_specs=pl.BlockSpec((TQ, GRP * D), lambda b, g, qi: (row(b, g, qi), g)),
        out_shape=jax.ShapeDtypeStruct((M_PROMPT, NSA_HEADS * D), BF16),
        scratch_shapes=[pltpu.VMEM((R, 2 * D), BF16),
                        pltpu.VMEM((ncp, 2 * D), BF16), pltpu.VMEM((ncp, D), BF16),
                        pltpu.VMEM((SEQ, 2 * D), BF16), pltpu.VMEM((SEQ, 2 * D), BF16),
                        pltpu.VMEM((SEQ, 2 * D), BF16), pltpu.VMEM((SEQ, 2 * D), BF16),
                        pltpu.VMEM((R, 1), F32), pltpu.VMEM((R, 2 * D), F32)],
        compiler_params=_params(("parallel", "parallel", "arbitrary"), blk),
        name="nsa_prompt",
    )(q, gate_logits, _sigma_table(), kvc, kvc, rows, rows, rows, rows, ov, kf_cmp, kf_sel, kf_win)


T_SAMPLE = PAST_LEN
N_SEL_S = -(-(PAST_LEN + 1) // SEL_LEN)
NB_PAST = PAST_LEN // SEL_LEN
KK_S = min(SEL_TOP, N_SEL_S)
SEL_PAD_S = _round_up(N_SEL_S, LANES)
assert PAST_LEN % SEL_LEN == 0 and PAST_LEN % CMP_STRIDE == 0


def _slope_col(slopes_ref, g):
    r_idx = lax.broadcasted_iota(jnp.int32, (GRP, 1), 0)
    col = jnp.zeros((GRP, 1), F32)
    for r in range(GRP):
        col = jnp.where(r_idx == r, slopes_ref[g * GRP + r], col)
    return col


def _nsa_sample_cmp_win_kernel(slopes_ref, q_ref, kc_ref, vc_ref, ov_ref, kw_ref, vw_ref, kwn_ref, vwn_ref,
                               oc_ref, ow_ref, idx_ref):
    g = pl.program_id(1)
    scale = HEAD_DIM ** -0.5
    q = q_ref[0, 0]
    slope = _slope_col(slopes_ref, g)

    kcb = kc_ref[0, 0, 0].astype(BF16)
    ncp = kcb.shape[0]
    cpos = lax.broadcasted_iota(jnp.int32, (1, ncp), 1) * CMP_STRIDE + (CMP_LEN - 1)
    dist_c = T_SAMPLE - cpos
    p_c = _masked_softmax(_dot_nt(q, kcb) * scale - slope * dist_c.astype(F32), dist_c >= 0)
    oc_ref[0, 0] = _dot(p_c.astype(BF16), vc_ref[0, 0, 0].astype(BF16))
    p_sum = jnp.broadcast_to(jnp.sum(p_c, axis=0, keepdims=True), (SUBLANES, ncp))
    imp = _dot_exact(p_sum, ov_ref[...])[0:1]

    NP = SEL_PAD_S
    cur = T_SAMPLE // SEL_LEN
    j_row = lax.broadcasted_iota(jnp.int32, (1, NP), 1)
    forced = (j_row == 0) | (j_row == cur) | (j_row == cur - 1)
    impm = jnp.where(forced, jnp.inf, jnp.where(j_row <= cur, imp, -jnp.inf))
    sub = lax.broadcasted_iota(jnp.int32, (NP, NP), 0)
    lane = lax.broadcasted_iota(jnp.int32, (NP, NP), 1)
    impm_r = jnp.broadcast_to(impm, (NP, NP))
    rank = _top_rank(impm_r.T, impm_r, sub, lane, 0).astype(F32)
    rank_c = jnp.broadcast_to(rank, (NP, NP)).T
    hit = (rank_c == lane.astype(F32)) & (sub <= cur)
    idx_ref[0, 0] = jnp.sum(jnp.where(hit, sub, 0), axis=0, keepdims=True)

    kwb = kw_ref[0].astype(BF16)
    wlen = kwb.shape[0]
    wpos = T_SAMPLE - wlen + lax.broadcasted_iota(jnp.int32, (1, wlen), 1)
    dist_w = T_SAMPLE - wpos
    valid_w = (dist_w >= 0) & (dist_w <= WINDOW) & (wpos >= 0)
    carry = _flash_step(_flash_init(GRP), _dot_nt(q, kwb) * scale - slope * dist_w.astype(F32), valid_w,
                        vw_ref[0].astype(BF16))
    k_new = kwn_ref[0, 0, 0].astype(BF16).astype(F32)
    v_new = vwn_ref[0, 0, 0].astype(BF16).astype(F32)
    s_new = jnp.sum(q.astype(F32) * k_new, axis=1, keepdims=True) * scale
    _, l_w, acc_w = _flash_one(carry, s_new, v_new)
    ow_ref[0, 0] = acc_w / l_w


def _nsa_sample_cmp_win(slopes, q_s, kvc, ov, win2d, rows_s):
    G, D, DB = NSA_KV_HEADS, HEAD_DIM, DEC_BATCH
    ncp = kvc.shape[3]
    wlen = win2d.shape[1]
    o_spec = pl.BlockSpec((1, 1, GRP, D), lambda b, g, sl: (b, g, 0, 0))
    blk = 2 * ncp * D * 4 + ov.size * 4 + 2 * wlen * D * 4 + 4 * SEL_PAD_S * SEL_PAD_S * 4
    return pl.pallas_call(
        _nsa_sample_cmp_win_kernel,
        grid_spec=pltpu.PrefetchScalarGridSpec(
            num_scalar_prefetch=1,
            grid=(DB, G),
            in_specs=[pl.BlockSpec((1, 1, GRP, D), lambda b, g, sl: (b, g, 0, 0)),
                      pl.BlockSpec((1, 1, 1, ncp, D), lambda b, g, sl: (b, 0, g, 0, 0)),
                      pl.BlockSpec((1, 1, 1, ncp, D), lambda b, g, sl: (b, 1, g, 0, 0)),
                      pl.BlockSpec(ov.shape, lambda b, g, sl: (0, 0)),
                      pl.BlockSpec((1, wlen, D), lambda b, g, sl: (b, 0, g)),
                      pl.BlockSpec((1, wlen, D), lambda b, g, sl: (b, 0, G + g)),
                      pl.BlockSpec((1, 1, 1, 1, D), lambda b, g, sl: (b, 4, g, 0, 0)),
                      pl.BlockSpec((1, 1, 1, 1, D), lambda b, g, sl: (b, 5, g, 0, 0))],
            out_specs=[o_spec, o_spec, pl.BlockSpec((1, 1, 1, SEL_PAD_S), lambda b, g, sl: (b, g, 0, 0))],
        ),
        out_shape=[jax.ShapeDtypeStruct((DB, G, GRP, D), F32),
                   jax.ShapeDtypeStruct((DB, G, GRP, D), F32),
                   jax.ShapeDtypeStruct((DB, G, 1, SEL_PAD_S), jnp.int32)],
        compiler_params=_params(("parallel", "parallel"), blk),
        name="nsa_sample_cmp_win",
    )(slopes, q_s, kvc, kvc, ov, win2d, win2d, rows_s, rows_s)


def _nsa_sample_sel_kernel(pt_ref, idx_ref, slopes_ref, q_ref, *refs):
    ks_refs, vs_refs = refs[:KK_S], refs[KK_S:2 * KK_S]
    ksn_ref, vsn_ref, oc_ref, ow_ref, gl_ref, out_ref = refs[2 * KK_S:]
    b, g = pl.program_id(0), pl.program_id(1)
    scale = HEAD_DIM ** -0.5
    q = q_ref[0, 0]
    slope = _slope_col(slopes_ref, g)
    off = lax.broadcasted_iota(jnp.int32, (1, SEL_LEN), 1)

    scores, valids = [], []
    has_new = False
    for kk in range(KK_S):
        j = idx_ref[(b * NSA_KV_HEADS + g) * KK_S + kk]
        in_cache = j < NB_PAST
        has_new = jnp.logical_or(has_new, jnp.logical_not(in_cache))
        dist = T_SAMPLE - (jnp.minimum(j, NB_PAST - 1) * SEL_LEN + off)
        valid = (dist >= 0) & in_cache
        s = _dot_nt(q, ks_refs[kk][0].astype(BF16)) * scale - slope * dist.astype(F32)
        scores.append(jnp.where(valid, s, NEG))
        valids.append(valid)
    k_new = ksn_ref[0, 0, 0].astype(BF16).astype(F32)
    v_new = vsn_ref[0, 0, 0].astype(BF16).astype(F32)
    s_new = jnp.sum(q.astype(F32) * k_new, axis=1, keepdims=True) * scale
    s_new = jnp.where(has_new, s_new, NEG)

    m = s_new
    for s in scores:
        m = jnp.maximum(m, jnp.max(s, axis=-1, keepdims=True))
    e_new = jnp.exp(s_new - m)
    l = e_new
    acc = jnp.where(has_new, e_new, 0.0) * v_new
    for kk in range(KK_S):
        e = jnp.exp(scores[kk] - m)
        l = l + jnp.sum(e, axis=-1, keepdims=True)
        acc = acc + _dot(jnp.where(valids[kk], e, 0.0).astype(BF16), vs_refs[kk][0].astype(BF16))
    gates = jax.nn.sigmoid(gl_ref[0, 0])
    out_ref[0, 0] = gates[0] * oc_ref[0, 0] + gates[1] * (acc / l) + gates[2] * ow_ref[0, 0]


def _nsa_sample_sel(page_table_flat, idx_flat, slopes, q_s, cache2d, rows_s, o_c, o_w, gate_logits_s):
    G, D, DB = NSA_KV_HEADS, HEAD_DIM, DEC_BATCH
    n_pages = PAST_LEN // PAGE_SIZE
    bpp = PAGE_SIZE // SEL_LEN

    def blk_map(b, g, pt, idx, sl, *, kk, slot):
        j = jnp.minimum(idx[(b * G + g) * KK_S + kk], NB_PAST - 1)
        return (pt[b * n_pages + j // bpp], j % bpp, slot * G + g)

    o_spec = pl.BlockSpec((1, 1, GRP, D), lambda b, g, pt, idx, sl: (b, g, 0, 0))
    blk_specs = [pl.BlockSpec((1, SEL_LEN, D), functools.partial(blk_map, kk=kk, slot=slot))
                 for slot in (2, 3) for kk in range(KK_S)]
    blk = 2 * KK_S * SEL_LEN * D * 4 + 6 * GRP * D * 4
    return pl.pallas_call(
        _nsa_sample_sel_kernel,
        grid_spec=pltpu.PrefetchScalarGridSpec(
            num_scalar_prefetch=3,
            grid=(DB, G),
            in_specs=[o_spec, *blk_specs,
                      pl.BlockSpec((1, 1, 1, 1, D), lambda b, g, pt, idx, sl: (b, 2, g, 0, 0)),
                      pl.BlockSpec((1, 1, 1, 1, D), lambda b, g, pt, idx, sl: (b, 3, g, 0, 0)),
                      o_spec, o_spec,
                      pl.BlockSpec((1, 1, 3, GRP, 1), lambda b, g, pt, idx, sl: (b, g, 0, 0, 0))],
            out_specs=o_spec,
        ),
        out_shape=jax.ShapeDtypeStruct((DB, G, GRP, D), F32),
        compiler_params=_params(("parallel", "parallel"), blk),
        name="nsa_sample_sel",
    )(page_table_flat, idx_flat, slopes, q_s, *([cache2d] * (2 * KK_S)), rows_s, rows_s, o_c, o_w, gate_logits_s)


def _stack_rows(prompt_rows, sample_rows):
    pad = jnp.zeros((M_PAD - M_PROMPT - M_SAMPLE, prompt_rows.shape[1]), prompt_rows.dtype)
    return jnp.concatenate([prompt_rows, sample_rows.astype(prompt_rows.dtype), pad], axis=0)


def _ffn_block(x, g, w_gate_up, w_down, layer):
    h = _rmsnorm(x, g, BF16)
    return _matmul_fullk(_ffn_up(h, w_gate_up, layer), w_down, layer, D_MODEL, F32, residual=x, k_split=2)


def kernel(x_prompt, x_sample, cache_nsa_kv, page_table, state_win_kv, state_mlstm_C, state_mlstm_n, state_mlstm_m,
           g_mix, g_ffn, g_kv, g_final, w_in_a, b_gate_a, g_head_a, w_out_a, w_qg_b, w_out_b, w_kv,
           pe_cmp, w_cmp1, w_cmp2, w_gate_up, w_down):
    H, DK, DV = MLSTM_HEADS, MLSTM_QK_DIM, MLSTM_V_DIM
    G, D, DB = NSA_KV_HEADS, HEAD_DIM, DEC_BATCH
    hd = NSA_HEADS * D
    n_main = 2 * H * DK + 2 * H * DV
    x = _stack_rows(x_prompt.reshape(M_PROMPT, D_MODEL), x_sample.reshape(M_SAMPLE, D_MODEL))
    w_in_t = jnp.swapaxes(w_in_a, 1, 2)
    w_qg_t = jnp.swapaxes(w_qg_b, 1, 2)
    w_gate_a = jnp.swapaxes(w_in_t[:, n_main:, :], 1, 2)
    w_gate_b = jnp.swapaxes(w_qg_t[:, hd:, :], 1, 2)

    c_p, n_p, m_p, c_s, n_s, m_s = [], [], [], [], [], []
    for l in range(N_A):
        h, gates = _rmsnorm_with_gates(x, g_mix[l], w_gate_a[l], b_gate_a[l], True)
        z = _matmul_fullk_nt(h, w_in_t, l, n_main, F32)
        gates_t = gates[:M_PROMPT, :2 * H].T.reshape(2, H, 1, M_PROMPT)
        hs_p, cp, np_, mp = _mlstm_prompt(z, gates_t, g_head_a[l])
        gs = gates[S0:S0 + DB, :2 * H]
        lane_b = lambda a: jnp.broadcast_to(a[:, :, None, None], (DB, H, 1, LANES))
        hs_s, cs, ns, ms = _mlstm_sample(z, lane_b(gs[:, :H]), lane_b(gs[:, H:]), lane_b(state_mlstm_m[l]),
                                         state_mlstm_C[l], state_mlstm_n[l].reshape(DB, H, 1, DK), g_head_a[l])
        hs = _stack_rows(hs_p, hs_s.reshape(DB, H * DV))
        x = _matmul_fullk(hs, w_out_a, l, D_MODEL, F32, residual=x)
        x = _ffn_block(x, g_ffn[l], w_gate_up, w_down, l)
        c_p.append(cp)
        n_p.append(np_.reshape(BATCH, H, DK))
        m_p.append(mp.reshape(BATCH, H))
        c_s.append(cs)
        n_s.append(ns.reshape(DB, H, DK))
        m_s.append(ms[:, :, 0, 0])

    rows = _matmul_fullk(_rmsnorm(x, g_kv, BF16), w_kv[None], 0, 6 * G * D, F32)
    rows_s = rows[S0:S0 + DB].reshape(DB, 6, G, 1, D)
    w1r = _compress_weights(w_cmp1)
    kvc_p = _compress_finish(_compress_proj_prompt(rows, w1r), pe_cmp, w_cmp1, w_cmp2)
    n_pool = cache_nsa_kv.shape[0]
    pt_flat = page_table.reshape(-1).astype(jnp.int32)
    cache2d = cache_nsa_kv.reshape(n_pool, PAGE_SIZE, 4 * G * D)
    kvc_s = _compress_finish(_compress_proj_sample(cache2d, pt_flat, w1r), pe_cmp, w_cmp1, w_cmp2)
    win2d = state_win_kv.reshape(DB, state_win_kv.shape[1], 2 * G * D)

    slopes = jnp.exp2(-8.0 * jnp.arange(1, NSA_HEADS + 1, dtype=F32) / NSA_HEADS)
    n_chunks_p = SEQ // CMP_STRIDE
    ov_p = _selection_overlap(n_chunks_p - CMP_R + 1, SEQ // SEL_LEN, n_chunks_p, LANES)
    n_chunks_s = (PAST_LEN + M_SAMPLE // DB) // CMP_STRIDE
    ov_s = _selection_overlap(n_chunks_s - CMP_R + 1, N_SEL_S, n_chunks_s, SEL_PAD_S)

    for l in range(N_B):
        h, gate_logits = _rmsnorm_with_gates(x, g_mix[N_A + l], w_gate_b[l], jnp.zeros((3 * NSA_HEADS,), F32),
                                             False)
        q = _matmul_fullk_nt(h, w_qg_t, l, hd, BF16)
        gl = gate_logits[:, :3 * NSA_HEADS]
        gl_p = gl.reshape(M_PAD, G, 3 * GRP).transpose(1, 0, 2)
        att_p = _nsa_prompt(q, gl_p, kvc_p, rows, ov_p)
        q_s = q[S0:S0 + DB].reshape(DB, G, GRP, D)
        gl_s = gl[S0:S0 + DB].reshape(DB, G, GRP, 3).transpose(0, 1, 3, 2)[..., None]
        o_c, o_w, idx = _nsa_sample_cmp_win(slopes, q_s, kvc_s, ov_s, win2d, rows_s)
        idx_flat = idx[:, :, 0, :KK_S].reshape(-1)
        att_s = _nsa_sample_sel(pt_flat, idx_flat, slopes, q_s, cache2d, rows_s, o_c, o_w, gl_s)
        att = _stack_rows(att_p, att_s.reshape(DB, hd))
        x = _matmul_fullk(att, w_out_b, l, D_MODEL, F32, residual=x)
        x = _ffn_block(x, g_ffn[N_A + l], w_gate_up, w_down, N_A + l)

    y = _rmsnorm(x, g_final, F32)
    wp = min(WINDOW, SEQ)
    rows_p = rows[:M_PROMPT].reshape(BATCH, SEQ, 6, G, D)
    rows_s6 = rows[S0:S0 + DB].reshape(DB, 1, 6, G, D)
    wbuf = state_win_kv.shape[1]
    win_kv_sample = jnp.concatenate([state_win_kv, rows_s6[:, :, 4:]], axis=1)[:, -wbuf:]
    return (y[:M_PROMPT].reshape(BATCH, SEQ, D_MODEL), y[S0:S0 + DB].reshape(DB, 1, D_MODEL),
            rows_p[:, :, :4], rows_p[:, SEQ - wp:, 4:],
            jnp.stack(c_p), jnp.stack(n_p), jnp.stack(m_p),
            rows_s6[:, :, :4], win_kv_sample,
            jnp.stack(c_s), jnp.stack(n_s), jnp.stack(m_s))
```

```python
import functools

import numpy as np
import jax
import jax.numpy as jnp
from jax import lax
from jax.experimental import pallas as pl
from jax.experimental.pallas import tpu as pltpu

D_MODEL = 4096
BATCH = 4
SEQ = 2048
DEPTH = 4
DEC_BATCH = 8
DEC_SEQ = 1
PAST_LEN = 8192
PAGE_SIZE = 128

N_A = DEPTH // 2
N_B = DEPTH - N_A
MLSTM_HEADS = 8
MLSTM_QK_DIM = D_MODEL // 16
MLSTM_V_DIM = D_MODEL // MLSTM_HEADS
GATE_CAP = 15.0
HEAD_DIM = 128
NSA_HEADS = D_MODEL // HEAD_DIM
NSA_KV_HEADS = 4
GRP = NSA_HEADS // NSA_KV_HEADS
CMP_LEN = 32
CMP_STRIDE = 16
CMP_HID = HEAD_DIM
SEL_LEN = 64
SEL_TOP = 16
WINDOW = 512
D_FF = -(-8 * D_MODEL // (3 * 256)) * 256
EPS = 1e-6
NEG = -1e30

F32 = jnp.float32
BF16 = jnp.bfloat16

LANES = 128
SUBLANES = 8
VMEM_BYTES_V7X = 64 * 1024 * 1024


def _round_up(n, m):
    return -(-n // m) * m


def _pick_tile(n, cap, align=LANES):
    best = None
    for t in range(align, min(n, cap) + 1, align):
        if n % t == 0:
            best = t
    assert best is not None, (n, cap, align)
    return best


M_PROMPT = BATCH * SEQ
M_SAMPLE = DEC_BATCH * DEC_SEQ
S0 = M_PROMPT
M_PAD = _round_up(M_PROMPT + M_SAMPLE, LANES)
TM = _pick_tile(M_PAD, 2048)
TR = _pick_tile(M_PAD, 256)
TN_FULLK = 256
MLSTM_L = 256
TQ = 128
TKS = 1024
N_GATE_PAD = LANES
D_FF_PAD = _round_up(D_FF, 1024)
CMP_R = CMP_LEN // CMP_STRIDE
SAMPLE_PAGES_PER_STEP = 8
PERM_ROWS = LANES

assert DEC_SEQ == 1 and S0 % 16 == 0
assert SEQ % PERM_ROWS == 0 and PAGE_SIZE % PERM_ROWS == 0
assert (PERM_ROWS // CMP_STRIDE) & (PERM_ROWS // CMP_STRIDE - 1) == 0
assert SEQ % MLSTM_L == 0 and SEQ % TQ == 0 and SEQ % TKS == 0 and TKS % TQ == 0
assert PAST_LEN % PAGE_SIZE == 0 and PAGE_SIZE % SEL_LEN == 0 and PAGE_SIZE % CMP_STRIDE == 0
assert CMP_R == 2 and SEQ // SEL_LEN <= LANES


def _vmem_limit(block_bytes, single_bytes):
    return int(min(single_bytes + 2 * block_bytes + (12 << 20), VMEM_BYTES_V7X - (8 << 20)))


def _params(sem, block_bytes, single_bytes=0):
    return pltpu.CompilerParams(dimension_semantics=sem,
                                vmem_limit_bytes=_vmem_limit(block_bytes, single_bytes))


def _dot(a, b):
    return jnp.dot(a, b, preferred_element_type=F32)


def _dot_nt(a, b):
    return lax.dot_general(a, b, (((1,), (1,)), ((), ())), preferred_element_type=F32)


def _dot_tn(a, b):
    return lax.dot_general(a, b, (((0,), (0,)), ((), ())), preferred_element_type=F32)


def _dot_exact(a, b):
    return jnp.dot(a, b, precision=lax.Precision.HIGHEST, preferred_element_type=F32)


def _log_sigmoid(x):
    return jnp.minimum(x, 0.0) - jnp.log1p(jnp.exp(-jnp.abs(x)))


def _col_form(row, n):
    return jnp.broadcast_to(row, (n, n)).T


def _rms_kernel(x_ref, g_ref, o_ref):
    x = x_ref[...]
    y = x * lax.rsqrt(jnp.mean(x * x, axis=-1, keepdims=True) + EPS) * g_ref[...]
    o_ref[...] = y.astype(o_ref.dtype)


def _rmsnorm(x, g, out_dtype):
    m, d = x.shape
    blk = TR * d * (4 + jnp.dtype(out_dtype).itemsize)
    return pl.pallas_call(
        _rms_kernel,
        grid=(m // TR,),
        in_specs=[pl.BlockSpec((TR, d), lambda i: (i, 0)), pl.BlockSpec((1, d), lambda i: (0, 0))],
        out_specs=pl.BlockSpec((TR, d), lambda i: (i, 0)),
        out_shape=jax.ShapeDtypeStruct((m, d), out_dtype),
        compiler_params=_params(("parallel",), blk),
        name="rmsnorm",
    )(x, g.reshape(1, d))


def _rms_gate_kernel(x_ref, g_ref, w_ref, b_ref, h_ref, o_ref, *, mlstm_gates):
    x = x_ref[...]
    y = x * lax.rsqrt(jnp.mean(x * x, axis=-1, keepdims=True) + EPS) * g_ref[...]
    h_ref[...] = y.astype(h_ref.dtype)
    z = _dot_exact(y, w_ref[...]) + b_ref[...]
    if mlstm_gates:
        gl = GATE_CAP * jnp.tanh(z / GATE_CAP)
        lane = lax.broadcasted_iota(jnp.int32, z.shape, 1)
        z = jnp.where(lane < MLSTM_HEADS, gl, _log_sigmoid(gl))
    o_ref[...] = z


def _rmsnorm_with_gates(x, g, w_gate, b_gate, mlstm_gates):
    m, d = x.shape
    ng = w_gate.shape[1]
    wp = jnp.pad(w_gate, ((0, 0), (0, N_GATE_PAD - ng)))
    bp = jnp.pad(b_gate, (0, N_GATE_PAD - ng)).reshape(1, N_GATE_PAD)
    blk = TR * d * 6 + d * N_GATE_PAD * 4 + TR * N_GATE_PAD * 4
    return pl.pallas_call(
        functools.partial(_rms_gate_kernel, mlstm_gates=mlstm_gates),
        grid=(m // TR,),
        in_specs=[pl.BlockSpec((TR, d), lambda i: (i, 0)),
                  pl.BlockSpec((1, d), lambda i: (0, 0)),
                  pl.BlockSpec((d, N_GATE_PAD), lambda i: (0, 0)),
                  pl.BlockSpec((1, N_GATE_PAD), lambda i: (0, 0))],
        out_specs=[pl.BlockSpec((TR, d), lambda i: (i, 0)),
                   pl.BlockSpec((TR, N_GATE_PAD), lambda i: (i, 0))],
        out_shape=[jax.ShapeDtypeStruct((m, d), BF16), jax.ShapeDtypeStruct((m, N_GATE_PAD), F32)],
        compiler_params=_params(("parallel",), blk),
        name="rmsnorm_gates",
    )(x, g.reshape(1, d), wp, bp)


def _mm_fullk_kernel(*refs, has_res, rows_valid):
    x_ref, w_ref, o_ref = refs[0], refs[1], refs[-1]
    w = w_ref[...]
    if rows_valid < w.shape[0]:
        w = jnp.where(lax.broadcasted_iota(jnp.int32, w.shape, 0) < rows_valid, w, 0.0)
    r = _dot(x_ref[...], w.astype(BF16))
    if has_res:
        r = r + refs[2][...]
    o_ref[...] = r.astype(o_ref.dtype)


def _row_tile_spec(kb, kh=0):
    return pl.BlockSpec((TM, kb), lambda i, j: (i, kh), pipeline_mode=pl.Buffered(1))


def _matmul_fullk(x, w, layer, n_cols, out_dtype, residual=None, k_split=1):
    m, kd = x.shape
    tn = TN_FULLK
    kb = kd // k_split
    k_rows = w.shape[1]
    assert n_cols % tn == 0 and kd % k_split == 0 and kb % LANES == 0 and w.dtype == F32
    assert (k_split - 1) * kb < k_rows <= kd
    out = residual
    for kh in range(k_split):
        last = kh == k_split - 1
        dt = out_dtype if last else F32
        osz = jnp.dtype(dt).itemsize
        in_specs = [_row_tile_spec(kb, kh), pl.BlockSpec((None, kb, tn), lambda i, j, kh=kh: (layer, kh, j))]
        args = [x, w]
        streamed = kb * tn * 4 + TM * tn * osz
        if out is not None:
            in_specs.append(pl.BlockSpec((TM, tn), lambda i, j: (i, j)))
            args.append(out)
            streamed += TM * tn * 4
        out = pl.pallas_call(
            functools.partial(_mm_fullk_kernel, has_res=out is not None, rows_valid=min(kb, k_rows - kh * kb)),
            grid=(m // TM, n_cols // tn),
            in_specs=in_specs,
            out_specs=pl.BlockSpec((TM, tn), lambda i, j: (i, j)),
            out_shape=jax.ShapeDtypeStruct((m, n_cols), dt),
            compiler_params=_params(("parallel", "arbitrary"), streamed, single_bytes=TM * kb * 2),
            name="matmul_fullk",
        )(*args)
    return out


def _mm_fullk_nt_kernel(x_ref, wt_ref, o_ref):
    o_ref[...] = _dot_nt(x_ref[...], wt_ref[...].astype(BF16)).astype(o_ref.dtype)


def _matmul_fullk_nt(x, wt, layer, n_cols, out_dtype):
    m, kd = x.shape
    tn = 2 * TN_FULLK
    assert n_cols % tn == 0 and wt.shape[2] == kd and wt.dtype == F32
    streamed = kd * tn * 4 + TM * tn * jnp.dtype(out_dtype).itemsize
    return pl.pallas_call(
        _mm_fullk_nt_kernel,
        grid=(m // TM, n_cols // tn),
        in_specs=[_row_tile_spec(kd), pl.BlockSpec((None, tn, kd), lambda i, j: (layer, j, 0))],
        out_specs=pl.BlockSpec((TM, tn), lambda i, j: (i, j)),
        out_shape=jax.ShapeDtypeStruct((m, n_cols), out_dtype),
        compiler_params=_params(("parallel", "arbitrary"), streamed, single_bytes=TM * kd * 2),
        name="matmul_fullk_nt",
    )(x, wt)


def _ffn_up_kernel(x_ref, wg_ref, wu_ref, o_ref, *, n_real):
    j = pl.program_id(1)

    @pl.when(j < n_real)
    def _():
        x = x_ref[...]
        gate = _dot(x, wg_ref[...].astype(BF16))
        up = _dot(x, wu_ref[...].astype(BF16))
        o_ref[...] = (jax.nn.silu(gate) * up).astype(o_ref.dtype)

    @pl.when(j >= n_real)
    def _():
        o_ref[...] = jnp.zeros_like(o_ref)


def _ffn_up(x, w_gate_up, layer):
    m, kd = x.shape
    tn = TN_FULLK
    assert D_FF % tn == 0 and D_FF_PAD % tn == 0
    n_real = D_FF // tn
    col = lambda j: jnp.minimum(j, n_real - 1)
    streamed = 2 * kd * tn * 4 + TM * tn * 2
    return pl.pallas_call(
        functools.partial(_ffn_up_kernel, n_real=n_real),
        grid=(m // TM, D_FF_PAD // tn),
        in_specs=[_row_tile_spec(kd),
                  pl.BlockSpec((None, kd, tn), lambda i, j: (layer, 0, col(j))),
                  pl.BlockSpec((None, kd, tn), lambda i, j: (layer, 0, n_real + col(j)))],
        out_specs=pl.BlockSpec((TM, tn), lambda i, j: (i, j)),
        out_shape=jax.ShapeDtypeStruct((m, D_FF_PAD), BF16),
        compiler_params=_params(("parallel", "arbitrary"), streamed, single_bytes=TM * kd * 2),
        name="ffn_up",
    )(x, w_gate_up, w_gate_up)


def _mlstm_prompt_kernel(q_ref, k_ref, v_ref, o_ref, g_ref, gh_ref, hs_ref, c_ref, n_ref, m_ref):
    L, DK, DV = MLSTM_L, MLSTM_QK_DIM, MLSTM_V_DIM

    @pl.when(pl.program_id(1) == 0)
    def _():
        c_ref[...] = jnp.zeros_like(c_ref)
        n_ref[...] = jnp.zeros_like(n_ref)
        m_ref[...] = jnp.zeros_like(m_ref)

    t_idx = lax.broadcasted_iota(jnp.int32, (L, L), 0)
    s_idx = lax.broadcasted_iota(jnp.int32, (L, L), 1)
    for hd in range(MLSTM_HEADS):
        ig_row = g_ref[0, hd]
        lf_row = g_ref[1, hd]
        m_prev = m_ref[0, hd]
        lf_c = _col_form(lf_row, L)
        b_row = jnp.sum(jnp.where(t_idx <= s_idx, lf_c, 0.0), axis=0, keepdims=True)
        b_r = jnp.broadcast_to(b_row, (L, L))
        b_c = b_r.T
        ig_r = jnp.broadcast_to(ig_row, (L, L))
        log_d = jnp.where(s_idx <= t_idx, b_c - b_r + ig_r, -jnp.inf)
        b_col = b_c[:, 0:1]
        inter = b_col + m_prev
        m_t = jnp.maximum(jnp.max(log_d, axis=1, keepdims=True), inter)
        d = jnp.exp(log_d - m_t)
        w_inter = jnp.exp(inter - m_t)

        q = q_ref[:, hd * DK:(hd + 1) * DK] * (DK ** -0.5)
        k = k_ref[:, hd * DK:(hd + 1) * DK]
        qb = q.astype(BF16)
        kb = k.astype(BF16)
        vb = v_ref[:, hd * DV:(hd + 1) * DV].astype(BF16)
        s = _dot_nt(qb, kb) * d
        c_prev = c_ref[0, hd]
        n_prev = n_ref[0, hd]
        num = w_inter * _dot(qb, c_prev.astype(BF16)) + _dot(s.astype(BF16), vb)
        den = w_inter * jnp.sum(q * n_prev, axis=1, keepdims=True) + jnp.sum(s, axis=1, keepdims=True)
        h = num / jnp.maximum(jnp.abs(den), jnp.exp(-m_t))
        hn = h * lax.rsqrt(jnp.mean(h * h, axis=1, keepdims=True) + EPS)
        cols = slice(hd * DV, (hd + 1) * DV)
        hs_ref[:, cols] = (hn * gh_ref[:, cols] * jax.nn.sigmoid(o_ref[:, cols])).astype(hs_ref.dtype)

        m_new = m_t[L - 1:L, :]
        b_last = b_row[:, L - 1:L]
        ig_col = ig_r.T[:, 0:1]
        decay = jnp.exp(b_last - b_col + ig_col - m_new)
        carry = jnp.exp(b_last + m_prev - m_new)
        kd = k * decay
        c_ref[0, hd] = carry * c_prev + _dot_tn(kd.astype(BF16), vb)
        n_ref[0, hd] = carry * n_prev + jnp.sum(kd, axis=0, keepdims=True)
        m_ref[0, hd] = m_new


def _mlstm_prompt(z, gates_t, g_head):
    H, DK, DV, L = MLSTM_HEADS, MLSTM_QK_DIM, MLSTM_V_DIM, MLSTM_L
    nc = SEQ // L
    assert (H * DV) % (H * DK) == 0
    v0 = (2 * H * DK) // (H * DV)
    row = lambda b, c: b * nc + c
    blk = L * (2 * H * DK + 2 * H * DV) * 4 + L * H * DV * 2 + H * DK * DV * 4
    return pl.pallas_call(
        _mlstm_prompt_kernel,
        grid=(BATCH, nc),
        in_specs=[pl.BlockSpec((L, H * DK), lambda b, c: (row(b, c), 0)),
                  pl.BlockSpec((L, H * DK), lambda b, c: (row(b, c), 1)),
                  pl.BlockSpec((L, H * DV), lambda b, c: (row(b, c), v0)),
                  pl.BlockSpec((L, H * DV), lambda b, c: (row(b, c), v0 + 1)),
                  pl.BlockSpec((2, H, 1, L), lambda b, c: (0, 0, 0, row(b, c))),
                  pl.BlockSpec((1, H * DV), lambda b, c: (0, 0))],
        out_specs=[pl.BlockSpec((L, H * DV), lambda b, c: (row(b, c), 0)),
                   pl.BlockSpec((1, H, DK, DV), lambda b, c: (b, 0, 0, 0)),
                   pl.BlockSpec((1, H, 1, DK), lambda b, c: (b, 0, 0, 0)),
                   pl.BlockSpec((1, H, 1, 1), lambda b, c: (b, 0, 0, 0))],
        out_shape=[jax.ShapeDtypeStruct((M_PAD, H * DV), BF16),
                   jax.ShapeDtypeStruct((BATCH, H, DK, DV), F32),
                   jax.ShapeDtypeStruct((BATCH, H, 1, DK), F32),
                   jax.ShapeDtypeStruct((BATCH, H, 1, 1), F32)],
        compiler_params=_params(("parallel", "arbitrary"), blk),
        name="mlstm_prompt",
    )(z, z, z, z, gates_t, g_head.reshape(1, H * DV))


def _mlstm_sample_kernel(q_ref, k_ref, v_ref, o_ref, ig_ref, lf_ref, m0_ref, c0_ref, n0_ref, gh_ref,
                         hs_ref, c_ref, n_ref, m_ref):
    DK = MLSTM_QK_DIM
    b = pl.program_id(0)
    q = q_ref[pl.ds(b, 1), :] * (DK ** -0.5)
    k = k_ref[pl.ds(b, 1), :]
    v = v_ref[pl.ds(b, 1), :]
    og = o_ref[pl.ds(b, 1), :]
    ig = ig_ref[0, 0]
    lf = lf_ref[0, 0]
    m0 = m0_ref[0, 0]
    m_t = jnp.maximum(ig, lf + m0)
    d = jnp.exp(ig - m_t)[:, 0:1]
    w = jnp.exp(lf + m0 - m_t)[:, 0:1]
    c0 = c0_ref[0, 0]
    n0 = n0_ref[0, 0]
    q_col = _col_form(q, DK)[:, 0:1]
    k_col = _col_form(k, DK)[:, 0:1]
    s = jnp.sum(q * k, axis=1, keepdims=True) * d
    num = w * jnp.sum(q_col * c0, axis=0, keepdims=True) + s * v
    den = w * jnp.sum(q * n0, axis=1, keepdims=True) + s
    h = num / jnp.maximum(jnp.abs(den), jnp.exp(-m_t)[:, 0:1])
    hn = h * lax.rsqrt(jnp.mean(h * h, axis=1, keepdims=True) + EPS)
    hs_ref[0] = hn * gh_ref[...] * jax.nn.sigmoid(og)
    c_ref[0, 0] = w * c0 + d * (k_col * v)
    n_ref[0, 0] = w * n0 + d * k
    m_ref[0, 0] = m_t


def _mlstm_sample(z, ig_b, lf_b, m_b, c0, n0, g_head):
    H, DK, DV = MLSTM_HEADS, MLSTM_QK_DIM, MLSTM_V_DIM
    DB = DEC_BATCH
    r0 = S0 // SUBLANES
    v0 = (2 * H * DK) // DV
    sc = pl.BlockSpec((1, 1, 1, LANES), lambda b, h: (b, h, 0, 0))
    blk = 2 * DK * DV * 4 + SUBLANES * (2 * DK + 2 * DV) * 4
    return pl.pallas_call(
        _mlstm_sample_kernel,
        grid=(DB, H),
        in_specs=[pl.BlockSpec((SUBLANES, DK), lambda b, h: (r0, h)),
                  pl.BlockSpec((SUBLANES, DK), lambda b, h: (r0, H + h)),
                  pl.BlockSpec((SUBLANES, DV), lambda b, h: (r0, v0 + h)),
                  pl.BlockSpec((SUBLANES, DV), lambda b, h: (r0, v0 + H + h)),
                  sc, sc, sc,
                  pl.BlockSpec((1, 1, DK, DV), lambda b, h: (b, h, 0, 0)),
                  pl.BlockSpec((1, 1, 1, DK), lambda b, h: (b, h, 0, 0)),
                  pl.BlockSpec((1, DV), lambda b, h: (0, h))],
        out_specs=[pl.BlockSpec((1, 1, DV), lambda b, h: (b, 0, h)),
                   pl.BlockSpec((1, 1, DK, DV), lambda b, h: (b, h, 0, 0)),
                   pl.BlockSpec((1, 1, 1, DK), lambda b, h: (b, h, 0, 0)),
                   sc],
        out_shape=[jax.ShapeDtypeStruct((DB, 1, H * DV), F32),
                   jax.ShapeDtypeStruct((DB, H, DK, DV), F32),
                   jax.ShapeDtypeStruct((DB, H, 1, DK), F32),
                   jax.ShapeDtypeStruct((DB, H, 1, LANES), F32)],
        compiler_params=_params(("parallel", "parallel"), blk),
        name="mlstm_sample",
    )(z, z, z, z, ig_b, lf_b, m_b, c0, n0, g_head.reshape(1, H * DV))


def _compress_proj_kernel(*refs, n_in, groups_per_in, lead):
    x_refs = refs[-(n_in + 3):-3]
    w_ref, p_ref, xp_scr = refs[-3], refs[-2], refs[-1]
    G, D = NSA_KV_HEADS, HEAD_DIM
    cpg = PERM_ROWS // CMP_STRIDE
    nch = n_in * groups_per_in * cpg
    i_idx = lax.broadcasted_iota(jnp.int32, (PERM_ROWS, PERM_ROWS), 0)
    j_idx = lax.broadcasted_iota(jnp.int32, (PERM_ROWS, PERM_ROWS), 1)
    src_row = (i_idx & (cpg - 1)) * CMP_STRIDE + (i_idx >> (cpg.bit_length() - 1))
    perm = (j_idx == src_row).astype(BF16)
    for i, xr in enumerate(x_refs):
        for p in range(groups_per_in):
            rows = pl.ds(p * PERM_ROWS, PERM_ROWS)
            xg = xr[0, rows, :] if lead else xr[rows, :]
            xg = _dot(perm, xg.astype(BF16))
            ch0 = (i * groups_per_in + p) * cpg
            for c in range(CMP_STRIDE):
                xp_scr[c, ch0:ch0 + cpg, :] = xg[c * cpg:(c + 1) * cpg, :]
    for s in range(2):
        acc = jnp.zeros((G * nch, CMP_R * CMP_HID), F32)
        for c in range(CMP_STRIDE):
            xs = jnp.concatenate([xp_scr[c, :, (s * G + g) * D:(s * G + g + 1) * D] for g in range(G)], axis=0)
            acc = acc + _dot(xs.astype(BF16), w_ref[s, c])
        for g in range(G):
            p_ref[0, s, g] = acc[g * nch:(g + 1) * nch]


def _compress_weights(w_cmp1):
    w = w_cmp1.reshape(2, CMP_R, CMP_STRIDE, HEAD_DIM, CMP_HID)
    return w.transpose(0, 2, 3, 1, 4).reshape(2, CMP_STRIDE, HEAD_DIM, CMP_R * CMP_HID).astype(BF16)


def _compress_proj_prompt(rows, w1r):
    G, D = NSA_KV_HEADS, HEAD_DIM
    nch = SEQ // CMP_STRIDE
    pw = CMP_R * CMP_HID
    blk = SEQ * 2 * G * D * 4 + w1r.size * 2 + 2 * G * nch * pw * 4 + SEQ * 2 * G * D * 4
    return pl.pallas_call(
        functools.partial(_compress_proj_kernel, n_in=1, groups_per_in=SEQ // PERM_ROWS, lead=False),
        grid=(BATCH,),
        in_specs=[pl.BlockSpec((SEQ, 2 * G * D), lambda b: (b, 0)),
                  pl.BlockSpec(w1r.shape, lambda b: (0, 0, 0, 0))],
        out_specs=pl.BlockSpec((1, 2, G, nch, pw), lambda b: (b, 0, 0, 0, 0)),
        out_shape=jax.ShapeDtypeStruct((BATCH, 2, G, nch, pw), F32),
        scratch_shapes=[pltpu.VMEM((CMP_STRIDE, nch, 2 * G * D), F32)],
        compiler_params=_params(("parallel",), blk),
        name="compress_proj_prompt",
    )(rows, w1r)


def _compress_proj_sample(cache2d, page_table_flat, w1r):
    G, D = NSA_KV_HEADS, HEAD_DIM
    n_pages = PAST_LEN // PAGE_SIZE
    pps = SAMPLE_PAGES_PER_STEP
    assert n_pages % pps == 0
    cpi = PAGE_SIZE // CMP_STRIDE
    nch_step = pps * cpi
    pw = CMP_R * CMP_HID

    def page_map(b, j, pt, *, i):
        return (pt[b * n_pages + j * pps + i], 0, 0)

    in_specs = [pl.BlockSpec((1, PAGE_SIZE, 2 * G * D), functools.partial(page_map, i=i)) for i in range(pps)]
    in_specs.append(pl.BlockSpec(w1r.shape, lambda b, j, pt: (0, 0, 0, 0)))
    blk = 2 * pps * PAGE_SIZE * 2 * G * D * 4 + w1r.size * 2 + 2 * G * nch_step * pw * 4
    return pl.pallas_call(
        functools.partial(_compress_proj_kernel, n_in=pps, groups_per_in=PAGE_SIZE // PERM_ROWS, lead=True),
        grid_spec=pltpu.PrefetchScalarGridSpec(
            num_scalar_prefetch=1,
            grid=(DEC_BATCH, n_pages // pps),
            in_specs=in_specs,
            out_specs=pl.BlockSpec((1, 2, G, nch_step, pw), lambda b, j, pt: (b, 0, 0, j, 0)),
            scratch_shapes=[pltpu.VMEM((CMP_STRIDE, nch_step, 2 * G * D), F32)],
        ),
        out_shape=jax.ShapeDtypeStruct((DEC_BATCH, 2, G, n_pages * cpi, pw), F32),
        compiler_params=_params(("parallel", "arbitrary"), blk),
        name="compress_proj_sample",
    )(page_table_flat, *([cache2d] * pps), w1r)


def _compress_finish_kernel(p_ref, pe_ref, w1_ref, w2_ref, o_ref):
    nch = p_ref.shape[3]
    pe = jnp.broadcast_to(pe_ref[0], (SUBLANES, pe_ref.shape[2])).astype(BF16)
    pe_term = _dot(pe, w1_ref[0].astype(BF16))[0:1]
    w2 = w2_ref[0].astype(BF16)
    for g in range(NSA_KV_HEADS):
        p = p_ref[0, 0, g]
        nxt = pltpu.roll(p[:, CMP_HID:], shift=nch - 1, axis=0)
        hidden = p[:, :CMP_HID] + nxt + pe_term
        o_ref[0, 0, g] = _dot(jax.nn.gelu(hidden).astype(BF16), w2)


def _compress_finish(p, pe_cmp, w_cmp1, w_cmp2):
    nb, _, G, nch, pw = p.shape
    pe = pe_cmp.reshape(2, 1, CMP_LEN * HEAD_DIM)
    blk = G * nch * (pw + HEAD_DIM) * 4 + CMP_LEN * HEAD_DIM * (CMP_HID + 1) * 4
    return pl.pallas_call(
        _compress_finish_kernel,
        grid=(nb, 2),
        in_specs=[pl.BlockSpec((1, 1, G, nch, pw), lambda b, s: (b, s, 0, 0, 0)),
                  pl.BlockSpec((1, 1, CMP_LEN * HEAD_DIM), lambda b, s: (s, 0, 0)),
                  pl.BlockSpec((1, CMP_LEN * HEAD_DIM, CMP_HID), lambda b, s: (s, 0, 0)),
                  pl.BlockSpec((1, CMP_HID, HEAD_DIM), lambda b, s: (s, 0, 0))],
        out_specs=pl.BlockSpec((1, 1, G, nch, HEAD_DIM), lambda b, s: (b, s, 0, 0, 0)),
        out_shape=jax.ShapeDtypeStruct((nb, 2, G, nch, HEAD_DIM), F32),
        compiler_params=_params(("parallel", "parallel"), blk),
        name="compress_finish",
    )(p, pe, w_cmp1, w_cmp2)


def _selection_overlap(n_cmp, n_sel, rows, cols):
    start = np.arange(n_cmp)[:, None] * CMP_STRIDE
    end = start + CMP_LEN - 1
    j = np.arange(n_sel)[None, :]
    ov = ((start <= (j + 1) * SEL_LEN - 1) & (end >= j * SEL_LEN)).astype(np.float32)
    out = np.zeros((rows, cols), np.float32)
    out[:n_cmp, :n_sel] = ov
    return jnp.asarray(out)


def _masked_softmax(s, valid):
    sm = jnp.where(valid, s, NEG)
    e = jnp.exp(sm - jnp.max(sm, axis=-1, keepdims=True))
    return jnp.where(valid, e / jnp.sum(e, axis=-1, keepdims=True), 0.0)


def _flash_step(carry, s, valid, vb):
    m, l, acc = carry
    sm = jnp.where(valid, s, NEG)
    m_new = jnp.maximum(m, jnp.max(sm, axis=-1, keepdims=True))
    alpha = jnp.exp(m - m_new)
    e = jnp.exp(sm - m_new)
    l = alpha * l + jnp.sum(e, axis=-1, keepdims=True)
    acc = alpha * acc + _dot(jnp.where(valid, e, 0.0).astype(BF16), vb)
    return m_new, l, acc


def _flash_one(carry, s_new, v_new):
    m, l, acc = carry
    m_new = jnp.maximum(m, s_new)
    alpha = jnp.exp(m - m_new)
    e = jnp.exp(s_new - m_new)
    return m_new, alpha * l + e, alpha * acc + e * v_new


def _flash_init(rows):
    return (jnp.full((rows, 1), -jnp.inf, F32), jnp.zeros((rows, 1), F32), jnp.zeros((rows, HEAD_DIM), F32))


def _top_rank(impm_c, impm_r, jp_idx, j_idx, axis):
    before = (impm_c > impm_r) | ((impm_c == impm_r) & (jp_idx < j_idx))
    return jnp.sum(before.astype(jnp.int32), axis=axis, keepdims=True)


FEAT_SIG = 96
FEAT_SIG_FULL = FEAT_SIG + 9
MASK_BIG = 2.0 ** 100
POS_RADIX = 256
LOG2E = 1.4426950408889634
assert SEQ // SEL_LEN <= FEAT_SIG and SEQ >= WINDOW + TQ


def _key_features(pos, block):
    n = pos.shape[0]
    kf = np.zeros((n, LANES), np.float32)
    if block is not None:
        kf[np.arange(n), block] = -MASK_BIG
    kf[:, FEAT_SIG:FEAT_SIG + 3] = (pos % POS_RADIX)[:, None]
    kf[:, FEAT_SIG + 3:FEAT_SIG + 6] = (pos // POS_RADIX)[:, None]
    kf[:, FEAT_SIG + 6:FEAT_SIG + 9] = 1.0
    return jnp.asarray(kf, BF16)


def _sigma_table():
    slopes = np.exp2(-8.0 * np.arange(1, NSA_HEADS + 1, dtype=np.float32) / NSA_HEADS).astype(np.float32)
    sigma = (slopes / np.float32(HEAD_DIM ** -0.5)).astype(np.float32)
    bf = lambda a: np.asarray(np.asarray(a, dtype=BF16), dtype=np.float32)
    s1 = bf(sigma)
    s2 = bf(sigma - s1)
    s3 = bf(sigma - s1 - s2)
    tab = np.zeros((NSA_HEADS, LANES), np.float32)
    for i, s in enumerate((s1, s2, s3)):
        tab[:, FEAT_SIG + i] = s
        tab[:, FEAT_SIG + 3 + i] = s * POS_RADIX
    tab[:, FEAT_SIG_FULL] = sigma
    return jnp.asarray(tab)


def _nsa_prompt_kernel(q_ref, gl_ref, sig_ref, kc_ref, vc_ref, ks_ref, vs_ref, kw_ref, vw_ref,
                       ov_ref, kfc_ref, kfs_ref, kfw_ref, out_ref,
                       qa_scr, kca_scr, vcb_scr, ksa_scr, vsb_scr, kwa_scr, vwb_scr, m_scr, acc_scr):
    qi = pl.program_id(2)
    D = HEAD_DIM
    c2 = (D ** -0.5) * LOG2E
    n_sel = SEQ // SEL_LEN
    qs = qi * TQ
    tpos = qs + lax.broadcasted_iota(jnp.int32, (TQ, 1), 0)
    tposf = tpos.astype(F32)
    lane = lax.broadcasted_iota(jnp.int32, (1, LANES), 1)
    heads = [slice(r * TQ, (r + 1) * TQ) for r in range(GRP)]

    @pl.when(qi == 0)
    def _():
        kca_scr[:, 0:D] = kc_ref[0, 0, 0].astype(BF16)
        kca_scr[:, D:2 * D] = kfc_ref[...]
        vcb_scr[...] = vc_ref[0, 0, 0].astype(BF16)
        ksa_scr[:, 0:D] = ks_ref[...].astype(BF16)
        ksa_scr[:, D:2 * D] = kfs_ref[...]
        kwa_scr[:, 0:D] = kw_ref[...].astype(BF16)
        kwa_scr[:, D:2 * D] = kfw_ref[...]
        ones_col = jnp.where(lax.broadcasted_iota(jnp.int32, (SEQ, D), 1) == 0, 1.0, 0.0).astype(BF16)
        vsb_scr[:, 0:D] = vs_ref[...].astype(BF16)
        vsb_scr[:, D:2 * D] = ones_col
        vwb_scr[:, 0:D] = vw_ref[...].astype(BF16)
        vwb_scr[:, D:2 * D] = ones_col

    def write_features(nsel):
        for r in range(GRP):
            sig_row = sig_ref[r:r + 1, :]
            c = -sig_row[:, FEAT_SIG_FULL:FEAT_SIG_FULL + 1] * tposf
            c_hi = c.astype(BF16).astype(F32)
            c_mid = (c - c_hi).astype(BF16).astype(F32)
            c_lo = c - c_hi - c_mid
            f = jnp.where(lane == FEAT_SIG + 6, c_hi,
                          jnp.where(lane == FEAT_SIG + 7, c_mid, jnp.where(lane == FEAT_SIG + 8, c_lo, sig_row)))
            if nsel is not None:
                f = jnp.where(lane < n_sel, nsel, f)
            qa_scr[heads[r], D:2 * D] = f.astype(BF16)

    for r in range(GRP):
        qa_scr[heads[r], 0:D] = q_ref[:, r * D:(r + 1) * D]
    write_features(None)

    ncp = kca_scr.shape[0]
    cpos = lax.broadcasted_iota(jnp.int32, (1, ncp), 1) * CMP_STRIDE + (CMP_LEN - 1)
    valid_c = cpos <= tpos
    s_c = _dot_nt(qa_scr[...], kca_scr[...])
    p_sum = jnp.zeros((TQ, ncp), F32)
    p_c = []
    for r in range(GRP):
        sm = jnp.where(valid_c, s_c[heads[r]], NEG)
        e = jnp.exp2((sm - jnp.max(sm, axis=-1, keepdims=True)) * c2)
        p = jnp.where(valid_c, e / jnp.sum(e, axis=-1, keepdims=True), 0.0)
        p_c.append(p.astype(BF16))
        p_sum = p_sum + p
    o_c = _dot(jnp.concatenate(p_c, axis=0), vcb_scr[...])
    imp = _dot_exact(p_sum, ov_ref[...])

    cur = tpos // SEL_LEN
    forced = (lane == 0) | (lane == cur) | (lane == cur - 1)
    impm = jnp.where(forced, jnp.inf, jnp.where(lane <= cur, imp, -jnp.inf))
    imp_t = impm.T[0:n_sel]
    j_sub = lax.broadcasted_iota(jnp.int32, (n_sel, TQ), 0)
    rank_t = jnp.zeros((n_sel, TQ), jnp.int32)
    for jp in range(n_sel):
        cj = imp_t[jp:jp + 1, :]
        rank_t = rank_t + ((cj > imp_t) | ((cj == imp_t) & (jp < j_sub))).astype(jnp.int32)
    unsel_t = jnp.where(rank_t < min(SEL_TOP, n_sel), 0.0, 1.0)
    unsel = jnp.concatenate([unsel_t, jnp.ones((LANES - n_sel, TQ), F32)], axis=0).T
    write_features(jnp.where(lane <= cur, unsel, 1.0))

    m_scr[...] = jnp.full(m_scr.shape, -jnp.inf, F32)
    acc_scr[...] = jnp.zeros(acc_scr.shape, F32)

    def sel_tile(kt, bias):
        k0 = pl.multiple_of(kt * TKS, TKS)
        s_all = _dot_nt(qa_scr[...], ksa_scr[pl.ds(k0, TKS), :])
        p_all = []
        for r in range(GRP):
            s = s_all[heads[r]]
            if bias is not None:
                s = s + bias
            m_old = m_scr[heads[r]]
            m_new = jnp.maximum(m_old, jnp.max(s, axis=-1, keepdims=True))
            alpha = jnp.exp2((m_old - m_new) * c2)
            e = jnp.exp2((s - m_new) * c2)
            m_scr[heads[r]] = m_new
            acc_scr[heads[r]] = alpha * acc_scr[heads[r]]
            p_all.append(e.astype(BF16))
        acc_scr[...] += _dot(jnp.concatenate(p_all, axis=0), vsb_scr[pl.ds(k0, TKS), :])

    n_past = qs // TKS

    def past_tile(kt, carry):
        sel_tile(kt, None)
        return carry

    lax.fori_loop(0, n_past, past_tile, 0)
    spos_d = n_past * TKS + lax.broadcasted_iota(jnp.int32, (1, TKS), 1)
    sel_tile(n_past, jnp.where(spos_d <= tpos, 0.0, -MASK_BIG))
    o_s = acc_scr[:, 0:D] / acc_scr[:, D:D + 1]

    wk = WINDOW + TQ
    w0 = pl.multiple_of(jnp.maximum(qs - WINDOW, 0), TQ)
    dist_w = tpos - (w0 + lax.broadcasted_iota(jnp.int32, (1, wk), 1))
    bias_w = jnp.where((dist_w >= 0) & (dist_w <= WINDOW), 0.0, -MASK_BIG)
    s_w = _dot_nt(qa_scr[...], kwa_scr[pl.ds(w0, wk), :])
    p_w = []
    for r in range(GRP):
        s = s_w[heads[r]] + bias_w
        p_w.append(jnp.exp2((s - jnp.max(s, axis=-1, keepdims=True)) * c2).astype(BF16))
    pv_w = _dot(jnp.concatenate(p_w, axis=0), vwb_scr[pl.ds(w0, wk), :])
    o_w = pv_w[:, 0:D] / pv_w[:, D:D + 1]

    for r in range(GRP):
        gates = jax.nn.sigmoid(gl_ref[0, :, 3 * r:3 * r + 3])
        out = gates[:, 0:1] * o_c[heads[r]] + gates[:, 1:2] * o_s[heads[r]] + gates[:, 2:3] * o_w[heads[r]]
        out_ref[:, r * D:(r + 1) * D] = out.astype(out_ref.dtype)


def _nsa_prompt(q, gate_logits, kvc, rows, ov):
    G, D = NSA_KV_HEADS, HEAD_DIM
    nq = SEQ // TQ
    ncp = kvc.shape[3]
    R = GRP * TQ
    spos = np.arange(SEQ)
    kf_cmp = _key_features(np.arange(ncp) * CMP_STRIDE + (CMP_LEN - 1), None)
    kf_sel = _key_features(spos, spos // SEL_LEN)
    kf_win = _key_features(spos, None)
    row = lambda b, g, qi: b * nq + qi
    kv_spec = lambda slot: pl.BlockSpec((SEQ, D), lambda b, g, qi: (b, slot * G + g))
    const = lambda a: pl.BlockSpec(a.shape, lambda b, g, qi: (0, 0))
    blk = (TQ * GRP * D * 4 + 2 * ncp * D * 4 + 4 * SEQ * D * 4 + 3 * SEQ * D * 2 + SEQ * D * 6 + R * D * 16
           + R * (TKS + WINDOW + TQ) * 6)
    return pl.pallas_call(
        _nsa_prompt_kernel,
        grid=(BATCH, G, nq),
        in_specs=[pl.BlockSpec((TQ, GRP * D), lambda b, g, qi: (row(b, g, qi), g)),
                  pl.BlockSpec((1, TQ, 3 * GRP), lambda b, g, qi: (g, row(b, g, qi), 0)),
                  pl.BlockSpec((GRP, LANES), lambda b, g, qi: (g, 0)),
                  pl.BlockSpec((1, 1, 1, ncp, D), lambda b, g, qi: (b, 0, g, 0, 0)),
                  pl.BlockSpec((1, 1, 1, ncp, D), lambda b, g, qi: (b, 1, g, 0, 0)),
                  kv_spec(2), kv_spec(3), kv_spec(4), kv_spec(5),
                  const(ov), const(kf_cmp), const(kf_sel), const(kf_win)],
        out_specs=pl.BlockSpec((TQ, GRP * D), lambda b, g, qi: (row(b, g, qi), g)),
        out_shape=jax.ShapeDtypeStruct((M_PAD, NSA_HEADS * D), BF16),
        scratch_shapes=[pltpu.VMEM((R, 2 * D), BF16),
                        pltpu.VMEM((ncp, 2 * D), BF16), pltpu.VMEM((ncp, D), BF16),
                        pltpu.VMEM((SEQ, 2 * D), BF16), pltpu.VMEM((SEQ, 2 * D), BF16),
                        pltpu.VMEM((SEQ, 2 * D), BF16), pltpu.VMEM((SEQ, 2 * D), BF16),
                        pltpu.VMEM((R, 1), F32), pltpu.VMEM((R, 2 * D), F32)],
        compiler_params=_params(("parallel", "parallel", "arbitrary"), blk),
        name="nsa_prompt",
    )(q, gate_logits, _sigma_table(), kvc, kvc, rows, rows, rows, rows, ov, kf_cmp, kf_sel, kf_win)


T_SAMPLE = PAST_LEN
N_SEL_S = -(-(PAST_LEN + 1) // SEL_LEN)
NB_PAST = PAST_LEN // SEL_LEN
KK_S = min(SEL_TOP, N_SEL_S)
SEL_PAD_S = _round_up(N_SEL_S, LANES)
assert PAST_LEN % SEL_LEN == 0 and PAST_LEN % CMP_STRIDE == 0


def _slope_col(slopes_ref, g):
    r_idx = lax.broadcasted_iota(jnp.int32, (GRP, 1), 0)
    col = jnp.zeros((GRP, 1), F32)
    for r in range(GRP):
        col = jnp.where(r_idx == r, slopes_ref[g * GRP + r], col)
    return col


def _nsa_sample_cmp_win_kernel(slopes_ref, q_ref, kc_ref, vc_ref, ov_ref, kw_ref, vw_ref, kwn_ref, vwn_ref,
                               oc_ref, ow_ref, idx_ref):
    g = pl.program_id(1)
    scale = HEAD_DIM ** -0.5
    q = q_ref[0, 0]
    slope = _slope_col(slopes_ref, g)

    kcb = kc_ref[0, 0, 0].astype(BF16)
    ncp = kcb.shape[0]
    cpos = lax.broadcasted_iota(jnp.int32, (1, ncp), 1) * CMP_STRIDE + (CMP_LEN - 1)
    dist_c = T_SAMPLE - cpos
    p_c = _masked_softmax(_dot_nt(q, kcb) * scale - slope * dist_c.astype(F32), dist_c >= 0)
    oc_ref[0, 0] = _dot(p_c.astype(BF16), vc_ref[0, 0, 0].astype(BF16))
    p_sum = jnp.broadcast_to(jnp.sum(p_c, axis=0, keepdims=True), (SUBLANES, ncp))
    imp = _dot_exact(p_sum, ov_ref[...])[0:1]

    NP = SEL_PAD_S
    cur = T_SAMPLE // SEL_LEN
    j_row = lax.broadcasted_iota(jnp.int32, (1, NP), 1)
    forced = (j_row == 0) | (j_row == cur) | (j_row == cur - 1)
    impm = jnp.where(forced, jnp.inf, jnp.where(j_row <= cur, imp, -jnp.inf))
    sub = lax.broadcasted_iota(jnp.int32, (NP, NP), 0)
    lane = lax.broadcasted_iota(jnp.int32, (NP, NP), 1)
    impm_r = jnp.broadcast_to(impm, (NP, NP))
    rank = _top_rank(impm_r.T, impm_r, sub, lane, 0).astype(F32)
    rank_c = jnp.broadcast_to(rank, (NP, NP)).T
    hit = (rank_c == lane.astype(F32)) & (sub <= cur)
    idx_ref[0, 0] = jnp.sum(jnp.where(hit, sub, 0), axis=0, keepdims=True)

    kwb = kw_ref[0].astype(BF16)
    wlen = kwb.shape[0]
    wpos = T_SAMPLE - wlen + lax.broadcasted_iota(jnp.int32, (1, wlen), 1)
    dist_w = T_SAMPLE - wpos
    valid_w = (dist_w >= 0) & (dist_w <= WINDOW) & (wpos >= 0)
    carry = _flash_step(_flash_init(GRP), _dot_nt(q, kwb) * scale - slope * dist_w.astype(F32), valid_w,
                        vw_ref[0].astype(BF16))
    k_new = kwn_ref[0, 0, 0].astype(BF16).astype(F32)
    v_new = vwn_ref[0, 0, 0].astype(BF16).astype(F32)
    s_new = jnp.sum(q.astype(F32) * k_new, axis=1, keepdims=True) * scale
    _, l_w, acc_w = _flash_one(carry, s_new, v_new)
    ow_ref[0, 0] = acc_w / l_w


def _nsa_sample_cmp_win(slopes, q_s, kvc, ov, win2d, rows_s):
    G, D, DB = NSA_KV_HEADS, HEAD_DIM, DEC_BATCH
    ncp = kvc.shape[3]
    wlen = win2d.shape[1]
    o_spec = pl.BlockSpec((1, 1, GRP, D), lambda b, g, sl: (b, g, 0, 0))
    blk = 2 * ncp * D * 4 + ov.size * 4 + 2 * wlen * D * 4 + 4 * SEL_PAD_S * SEL_PAD_S * 4
    return pl.pallas_call(
        _nsa_sample_cmp_win_kernel,
        grid_spec=pltpu.PrefetchScalarGridSpec(
            num_scalar_prefetch=1,
            grid=(DB, G),
            in_specs=[pl.BlockSpec((1, 1, GRP, D), lambda b, g, sl: (b, g, 0, 0)),
                      pl.BlockSpec((1, 1, 1, ncp, D), lambda b, g, sl: (b, 0, g, 0, 0)),
                      pl.BlockSpec((1, 1, 1, ncp, D), lambda b, g, sl: (b, 1, g, 0, 0)),
                      pl.BlockSpec(ov.shape, lambda b, g, sl: (0, 0)),
                      pl.BlockSpec((1, wlen, D), lambda b, g, sl: (b, 0, g)),
                      pl.BlockSpec((1, wlen, D), lambda b, g, sl: (b, 0, G + g)),
                      pl.BlockSpec((1, 1, 1, 1, D), lambda b, g, sl: (b, 4, g, 0, 0)),
                      pl.BlockSpec((1, 1, 1, 1, D), lambda b, g, sl: (b, 5, g, 0, 0))],
            out_specs=[o_spec, o_spec, pl.BlockSpec((1, 1, 1, SEL_PAD_S), lambda b, g, sl: (b, g, 0, 0))],
        ),
        out_shape=[jax.ShapeDtypeStruct((DB, G, GRP, D), F32),
                   jax.ShapeDtypeStruct((DB, G, GRP, D), F32),
                   jax.ShapeDtypeStruct((DB, G, 1, SEL_PAD_S), jnp.int32)],
        compiler_params=_params(("parallel", "parallel"), blk),
        name="nsa_sample_cmp_win",
    )(slopes, q_s, kvc, kvc, ov, win2d, win2d, rows_s, rows_s)


def _nsa_sample_sel_kernel(pt_ref, idx_ref, slopes_ref, q_ref, *refs):
    ks_refs, vs_refs = refs[:KK_S], refs[KK_S:2 * KK_S]
    ksn_ref, vsn_ref, oc_ref, ow_ref, gl_ref, out_ref = refs[2 * KK_S:]
    b, g = pl.program_id(0), pl.program_id(1)
    scale = HEAD_DIM ** -0.5
    q = q_ref[0, 0]
    slope = _slope_col(slopes_ref, g)
    off = lax.broadcasted_iota(jnp.int32, (1, SEL_LEN), 1)

    scores, valids = [], []
    has_new = False
    for kk in range(KK_S):
        j = idx_ref[(b * NSA_KV_HEADS + g) * KK_S + kk]
        in_cache = j < NB_PAST
        has_new = jnp.logical_or(has_new, jnp.logical_not(in_cache))
        dist = T_SAMPLE - (jnp.minimum(j, NB_PAST - 1) * SEL_LEN + off)
        valid = (dist >= 0) & in_cache
        s = _dot_nt(q, ks_refs[kk][0].astype(BF16)) * scale - slope * dist.astype(F32)
        scores.append(jnp.where(valid, s, NEG))
        valids.append(valid)
    k_new = ksn_ref[0, 0, 0].astype(BF16).astype(F32)
    v_new = vsn_ref[0, 0, 0].astype(BF16).astype(F32)
    s_new = jnp.sum(q.astype(F32) * k_new, axis=1, keepdims=True) * scale
    s_new = jnp.where(has_new, s_new, NEG)

    m = s_new
    for s in scores:
        m = jnp.maximum(m, jnp.max(s, axis=-1, keepdims=True))
    e_new = jnp.exp(s_new - m)
    l = e_new
    acc = jnp.where(has_new, e_new, 0.0) * v_new
    for kk in range(KK_S):
        e = jnp.exp(scores[kk] - m)
        l = l + jnp.sum(e, axis=-1, keepdims=True)
        acc = acc + _dot(jnp.where(valids[kk], e, 0.0).astype(BF16), vs_refs[kk][0].astype(BF16))
    gates = jax.nn.sigmoid(gl_ref[0, 0])
    out_ref[0, 0] = gates[0] * oc_ref[0, 0] + gates[1] * (acc / l) + gates[2] * ow_ref[0, 0]


def _nsa_sample_sel(page_table_flat, idx_flat, slopes, q_s, cache2d, rows_s, o_c, o_w, gate_logits_s):
    G, D, DB = NSA_KV_HEADS, HEAD_DIM, DEC_BATCH
    n_pages = PAST_LEN // PAGE_SIZE
    bpp = PAGE_SIZE // SEL_LEN

    def blk_map(b, g, pt, idx, sl, *, kk, slot):
        j = jnp.minimum(idx[(b * G + g) * KK_S + kk], NB_PAST - 1)
        return (pt[b * n_pages + j // bpp], j % bpp, slot * G + g)

    o_spec = pl.BlockSpec((1, 1, GRP, D), lambda b, g, pt, idx, sl: (b, g, 0, 0))
    blk_specs = [pl.BlockSpec((1, SEL_LEN, D), functools.partial(blk_map, kk=kk, slot=slot))
                 for slot in (2, 3) for kk in range(KK_S)]
    blk = 2 * KK_S * SEL_LEN * D * 4 + 6 * GRP * D * 4
    return pl.pallas_call(
        _nsa_sample_sel_kernel,
        grid_spec=pltpu.PrefetchScalarGridSpec(
            num_scalar_prefetch=3,
            grid=(DB, G),
            in_specs=[o_spec, *blk_specs,
                      pl.BlockSpec((1, 1, 1, 1, D), lambda b, g, pt, idx, sl: (b, 2, g, 0, 0)),
                      pl.BlockSpec((1, 1, 1, 1, D), lambda b, g, pt, idx, sl: (b, 3, g, 0, 0)),
                      o_spec, o_spec,
                      pl.BlockSpec((1, 1, 3, GRP, 1), lambda b, g, pt, idx, sl: (b, g, 0, 0, 0))],
            out_specs=o_spec,
        ),
        out_shape=jax.ShapeDtypeStruct((DB, G, GRP, D), F32),
        compiler_params=_params(("parallel", "parallel"), blk),
        name="nsa_sample_sel",
    )(page_table_flat, idx_flat, slopes, q_s, *([cache2d] * (2 * KK_S)), rows_s, rows_s, o_c, o_w, gate_logits_s)


def _stack_rows(prompt_rows, sample_rows):
    pad = jnp.zeros((M_PAD - M_PROMPT - M_SAMPLE, prompt_rows.shape[1]), prompt_rows.dtype)
    return jnp.concatenate([prompt_rows, sample_rows.astype(prompt_rows.dtype), pad], axis=0)


def _tail_rows_kernel(buf_ref, s_ref, o_ref):
    del buf_ref
    ns = s_ref.shape[0]
    o_ref[0:ns, :] = s_ref[...]
    o_ref[ns:, :] = jnp.zeros((o_ref.shape[0] - ns, o_ref.shape[1]), o_ref.dtype)


def _fill_tail_rows(buf, sample_rows):
    tail = M_PAD - M_PROMPT
    ns = _round_up(M_SAMPLE, 16)
    assert M_PROMPT % tail == 0 and ns <= tail and buf.dtype == BF16
    s = jnp.pad(sample_rows.astype(BF16), ((0, ns - M_SAMPLE), (0, 0)))
    cols = buf.shape[1]
    return pl.pallas_call(
        _tail_rows_kernel,
        grid=(1,),
        in_specs=[pl.BlockSpec(memory_space=pl.ANY), pl.BlockSpec((ns, cols), lambda i: (0, 0))],
        out_specs=pl.BlockSpec((tail, cols), lambda i: (M_PROMPT // tail, 0)),
        out_shape=jax.ShapeDtypeStruct(buf.shape, buf.dtype),
        input_output_aliases={0: 0},
        compiler_params=_params(("arbitrary",), (tail + ns) * cols * 2),
        name="fill_tail_rows",
    )(buf, s)


def _ffn_block(x, g, w_gate_up, w_down, layer):
    h = _rmsnorm(x, g, BF16)
    return _matmul_fullk(_ffn_up(h, w_gate_up, layer), w_down, layer, D_MODEL, F32, residual=x, k_split=2)


def kernel(x_prompt, x_sample, cache_nsa_kv, page_table, state_win_kv, state_mlstm_C, state_mlstm_n, state_mlstm_m,
           g_mix, g_ffn, g_kv, g_final, w_in_a, b_gate_a, g_head_a, w_out_a, w_qg_b, w_out_b, w_kv,
           pe_cmp, w_cmp1, w_cmp2, w_gate_up, w_down):
    H, DK, DV = MLSTM_HEADS, MLSTM_QK_DIM, MLSTM_V_DIM
    G, D, DB = NSA_KV_HEADS, HEAD_DIM, DEC_BATCH
    hd = NSA_HEADS * D
    n_main = 2 * H * DK + 2 * H * DV
    x = _stack_rows(x_prompt.reshape(M_PROMPT, D_MODEL), x_sample.reshape(M_SAMPLE, D_MODEL))
    w_in_t = jnp.swapaxes(w_in_a, 1, 2)
    w_qg_t = jnp.swapaxes(w_qg_b, 1, 2)
    w_gate_a = jnp.swapaxes(w_in_t[:, n_main:, :], 1, 2)
    w_gate_b = jnp.swapaxes(w_qg_t[:, hd:, :], 1, 2)

    c_p, n_p, m_p, c_s, n_s, m_s = [], [], [], [], [], []
    for l in range(N_A):
        h, gates = _rmsnorm_with_gates(x, g_mix[l], w_gate_a[l], b_gate_a[l], True)
        z = _matmul_fullk_nt(h, w_in_t, l, n_main, F32)
        gates_t = gates[:M_PROMPT, :2 * H].T.reshape(2, H, 1, M_PROMPT)
        hs_p, cp, np_, mp = _mlstm_prompt(z, gates_t, g_head_a[l])
        gs = gates[S0:S0 + DB, :2 * H]
        lane_b = lambda a: jnp.broadcast_to(a[:, :, None, None], (DB, H, 1, LANES))
        hs_s, cs, ns, ms = _mlstm_sample(z, lane_b(gs[:, :H]), lane_b(gs[:, H:]), lane_b(state_mlstm_m[l]),
                                         state_mlstm_C[l], state_mlstm_n[l].reshape(DB, H, 1, DK), g_head_a[l])
        hs = _fill_tail_rows(hs_p, hs_s.reshape(DB, H * DV))
        x = _matmul_fullk(hs, w_out_a, l, D_MODEL, F32, residual=x)
        x = _ffn_block(x, g_ffn[l], w_gate_up, w_down, l)
        c_p.append(cp)
        n_p.append(np_.reshape(BATCH, H, DK))
        m_p.append(mp.reshape(BATCH, H))
        c_s.append(cs)
        n_s.append(ns.reshape(DB, H, DK))
        m_s.append(ms[:, :, 0, 0])

    rows = _matmul_fullk(_rmsnorm(x, g_kv, BF16), w_kv[None], 0, 6 * G * D, F32)
    rows_s = rows[S0:S0 + DB].reshape(DB, 6, G, 1, D)
    w1r = _compress_weights(w_cmp1)
    kvc_p = _compress_finish(_compress_proj_prompt(rows, w1r), pe_cmp, w_cmp1, w_cmp2)
    n_pool = cache_nsa_kv.shape[0]
    pt_flat = page_table.reshape(-1).astype(jnp.int32)
    cache2d = cache_nsa_kv.reshape(n_pool, PAGE_SIZE, 4 * G * D)
    kvc_s = _compress_finish(_compress_proj_sample(cache2d, pt_flat, w1r), pe_cmp, w_cmp1, w_cmp2)
    win2d = state_win_kv.reshape(DB, state_win_kv.shape[1], 2 * G * D)

    slopes = jnp.exp2(-8.0 * jnp.arange(1, NSA_HEADS + 1, dtype=F32) / NSA_HEADS)
    n_chunks_p = SEQ // CMP_STRIDE
    ov_p = _selection_overlap(n_chunks_p - CMP_R + 1, SEQ // SEL_LEN, n_chunks_p, LANES)
    n_chunks_s = (PAST_LEN + M_SAMPLE // DB) // CMP_STRIDE
    ov_s = _selection_overlap(n_chunks_s - CMP_R + 1, N_SEL_S, n_chunks_s, SEL_PAD_S)

    for l in range(N_B):
        h, gate_logits = _rmsnorm_with_gates(x, g_mix[N_A + l], w_gate_b[l], jnp.zeros((3 * NSA_HEADS,), F32),
                                             False)
        q = _matmul_fullk_nt(h, w_qg_t, l, hd, BF16)
        gl = gate_logits[:, :3 * NSA_HEADS]
        gl_p = gl.reshape(M_PAD, G, 3 * GRP).transpose(1, 0, 2)
        att_p = _nsa_prompt(q, gl_p, kvc_p, rows, ov_p)
        q_s = q[S0:S0 + DB].reshape(DB, G, GRP, D)
        gl_s = gl[S0:S0 + DB].reshape(DB, G, GRP, 3).transpose(0, 1, 3, 2)[..., None]
        o_c, o_w, idx = _nsa_sample_cmp_win(slopes, q_s, kvc_s, ov_s, win2d, rows_s)
        idx_flat = idx[:, :, 0, :KK_S].reshape(-1)
        att_s = _nsa_sample_sel(pt_flat, idx_flat, slopes, q_s, cache2d, rows_s, o_c, o_w, gl_s)
        att = _fill_tail_rows(att_p, att_s.reshape(DB, hd))
        x = _matmul_fullk(att, w_out_b, l, D_MODEL, F32, residual=x)
        x = _ffn_block(x, g_ffn[N_A + l], w_gate_up, w_down, N_A + l)

    y = _rmsnorm(x, g_final, F32)
    wp = min(WINDOW, SEQ)
    rows_p = rows[:M_PROMPT].reshape(BATCH, SEQ, 6, G, D)
    rows_s6 = rows[S0:S0 + DB].reshape(DB, 1, 6, G, D)
    wbuf = state_win_kv.shape[1]
    win_kv_sample = jnp.concatenate([state_win_kv, rows_s6[:, :, 4:]], axis=1)[:, -wbuf:]
    return (y[:M_PROMPT].reshape(BATCH, SEQ, D_MODEL), y[S0:S0 + DB].reshape(DB, 1, D_MODEL),
            rows_p[:, :, :4], rows_p[:, SEQ - wp:, 4:],
            jnp.stack(c_p), jnp.stack(n_p), jnp.stack(m_p),
            rows_s6[:, :, :4], win_kv_sample,
            jnp.stack(c_s), jnp.stack(n_s), jnp.stack(m_s))
```

```python
import functools

import numpy as np
import jax
import jax.numpy as jnp
from jax import lax
from jax.experimental import pallas as pl
from jax.experimental.pallas import tpu as pltpu

D_MODEL = 4096
BATCH = 4
SEQ = 2048
DEPTH = 4
DEC_BATCH = 8
DEC_SEQ = 1
PAST_LEN = 8192
PAGE_SIZE = 128

N_A = DEPTH // 2
N_B = DEPTH - N_A
MLSTM_HEADS = 8
MLSTM_QK_DIM = D_MODEL // 16
MLSTM_V_DIM = D_MODEL // MLSTM_HEADS
GATE_CAP = 15.0
HEAD_DIM = 128
NSA_HEADS = D_MODEL // HEAD_DIM
NSA_KV_HEADS = 4
GRP = NSA_HEADS // NSA_KV_HEADS
CMP_LEN = 32
CMP_STRIDE = 16
CMP_HID = HEAD_DIM
SEL_LEN = 64
SEL_TOP = 16
WINDOW = 512
D_FF = -(-8 * D_MODEL // (3 * 256)) * 256
EPS = 1e-6
NEG = -1e30

F32 = jnp.float32
BF16 = jnp.bfloat16

LANES = 128
SUBLANES = 8
VMEM_BYTES_V7X = 64 * 1024 * 1024


def _round_up(n, m):
    return -(-n // m) * m


def _pick_tile(n, cap, align=LANES):
    best = None
    for t in range(align, min(n, cap) + 1, align):
        if n % t == 0:
            best = t
    assert best is not None, (n, cap, align)
    return best


M_PROMPT = BATCH * SEQ
M_SAMPLE = DEC_BATCH * DEC_SEQ
S0 = M_PROMPT
M_PAD = _round_up(M_PROMPT + M_SAMPLE, LANES)
TM = _pick_tile(M_PAD, 2048)
TR = _pick_tile(M_PAD, 384, align=64)
TN_FULLK = 256
MLSTM_L = 256
TQ = 128
TKS = 1024
N_GATE_PAD = LANES
D_FF_PAD = _round_up(D_FF, 1024)
CMP_R = CMP_LEN // CMP_STRIDE
SAMPLE_PAGES_PER_STEP = 8
PERM_ROWS = LANES

assert DEC_SEQ == 1 and S0 % 16 == 0
assert SEQ % PERM_ROWS == 0 and PAGE_SIZE % PERM_ROWS == 0
assert (PERM_ROWS // CMP_STRIDE) & (PERM_ROWS // CMP_STRIDE - 1) == 0
assert SEQ % MLSTM_L == 0 and SEQ % TQ == 0 and SEQ % TKS == 0 and TKS % TQ == 0
assert PAST_LEN % PAGE_SIZE == 0 and PAGE_SIZE % SEL_LEN == 0 and PAGE_SIZE % CMP_STRIDE == 0
assert CMP_R == 2 and SEQ // SEL_LEN <= LANES


def _vmem_limit(block_bytes, single_bytes):
    return int(min(single_bytes + 2 * block_bytes + (12 << 20), VMEM_BYTES_V7X - (8 << 20)))


def _params(sem, block_bytes, single_bytes=0):
    return pltpu.CompilerParams(dimension_semantics=sem,
                                vmem_limit_bytes=_vmem_limit(block_bytes, single_bytes))


def _dot(a, b):
    return jnp.dot(a, b, preferred_element_type=F32)


def _dot_nt(a, b):
    return lax.dot_general(a, b, (((1,), (1,)), ((), ())), preferred_element_type=F32)


def _dot_tn(a, b):
    return lax.dot_general(a, b, (((0,), (0,)), ((), ())), preferred_element_type=F32)


def _dot_exact(a, b):
    return jnp.dot(a, b, precision=lax.Precision.HIGHEST, preferred_element_type=F32)


def _dot_split3(a, b):
    a_hi = a.astype(BF16)
    a_lo = (a - a_hi.astype(F32)).astype(BF16)
    b_hi = b.astype(BF16)
    b_lo = (b - b_hi.astype(F32)).astype(BF16)
    return _dot(a_hi, b_hi) + (_dot(a_hi, b_lo) + _dot(a_lo, b_hi))


def _log_sigmoid(x):
    return jnp.minimum(x, 0.0) - jnp.log1p(jnp.exp(-jnp.abs(x)))


def _col_form(row, n):
    return jnp.broadcast_to(row, (n, n)).T


def _rms_kernel(x_ref, g_ref, o_ref):
    x = x_ref[...]
    y = x * lax.rsqrt(jnp.mean(x * x, axis=-1, keepdims=True) + EPS) * g_ref[...]
    o_ref[...] = y.astype(o_ref.dtype)


def _rmsnorm(x, g, out_dtype, n_rows=None):
    d = x.shape[1]
    m = x.shape[0] if n_rows is None else n_rows
    tr = _pick_tile(m, TR, align=SUBLANES * 8)
    blk = tr * d * (8 + jnp.dtype(out_dtype).itemsize)
    return pl.pallas_call(
        _rms_kernel,
        grid=(m // tr,),
        in_specs=[pl.BlockSpec((tr, d), lambda i: (i, 0)), pl.BlockSpec((1, d), lambda i: (0, 0))],
        out_specs=pl.BlockSpec((tr, d), lambda i: (i, 0)),
        out_shape=jax.ShapeDtypeStruct((m, d), out_dtype),
        compiler_params=_params(("parallel",), blk),
        name="rmsnorm",
    )(x, g.reshape(1, d))


def _rms_gate_kernel(x_ref, g_ref, w_ref, b_ref, h_ref, o_ref, *, mlstm_gates):
    x = x_ref[...]
    y = x * lax.rsqrt(jnp.mean(x * x, axis=-1, keepdims=True) + EPS) * g_ref[...]
    h_ref[...] = y.astype(h_ref.dtype)
    z = _dot_split3(y, w_ref[...]) + b_ref[...]
    if mlstm_gates:
        gl = GATE_CAP * jnp.tanh(z / GATE_CAP)
        lane = lax.broadcasted_iota(jnp.int32, z.shape, 1)
        z = jnp.where(lane < MLSTM_HEADS, gl, _log_sigmoid(gl))
    o_ref[...] = z


def _rmsnorm_with_gates(x, g, w_gate, b_gate, mlstm_gates):
    m, d = x.shape
    ng = w_gate.shape[1]
    wp = jnp.pad(w_gate, ((0, 0), (0, N_GATE_PAD - ng)))
    bp = jnp.pad(b_gate, (0, N_GATE_PAD - ng)).reshape(1, N_GATE_PAD)
    blk = TR * d * 10 + d * N_GATE_PAD * 6 + TR * N_GATE_PAD * 4
    return pl.pallas_call(
        functools.partial(_rms_gate_kernel, mlstm_gates=mlstm_gates),
        grid=(m // TR,),
        in_specs=[pl.BlockSpec((TR, d), lambda i: (i, 0)),
                  pl.BlockSpec((1, d), lambda i: (0, 0)),
                  pl.BlockSpec((d, N_GATE_PAD), lambda i: (0, 0)),
                  pl.BlockSpec((1, N_GATE_PAD), lambda i: (0, 0))],
        out_specs=[pl.BlockSpec((TR, d), lambda i: (i, 0)),
                   pl.BlockSpec((TR, N_GATE_PAD), lambda i: (i, 0))],
        out_shape=[jax.ShapeDtypeStruct((m, d), BF16), jax.ShapeDtypeStruct((m, N_GATE_PAD), F32)],
        compiler_params=_params(("parallel",), blk),
        name="rmsnorm_gates",
    )(x, g.reshape(1, d), wp, bp)


def _mm_fullk_kernel(*refs, has_res, rows_valid):
    x_ref, w_ref, o_ref = refs[0], refs[1], refs[-1]
    w = w_ref[...]
    if rows_valid < w.shape[0]:
        w = jnp.where(lax.broadcasted_iota(jnp.int32, w.shape, 0) < rows_valid, w, 0.0)
    r = _dot(x_ref[...], w.astype(BF16))
    if has_res:
        r = r + refs[2][...]
    o_ref[...] = r.astype(o_ref.dtype)


def _row_tile_spec(kb, kh=0):
    return pl.BlockSpec((TM, kb), lambda i, j: (i, kh), pipeline_mode=pl.Buffered(1))


def _matmul_fullk(x, w, layer, n_cols, out_dtype, residual=None, k_split=1):
    m, kd = x.shape
    tn = TN_FULLK
    kb = kd // k_split
    k_rows = w.shape[1]
    assert n_cols % tn == 0 and kd % k_split == 0 and kb % LANES == 0 and w.dtype == F32
    assert (k_split - 1) * kb < k_rows <= kd
    out = residual
    for kh in range(k_split):
        last = kh == k_split - 1
        dt = out_dtype if last else F32
        osz = jnp.dtype(dt).itemsize
        in_specs = [_row_tile_spec(kb, kh), pl.BlockSpec((None, kb, tn), lambda i, j, kh=kh: (layer, kh, j))]
        args = [x, w]
        streamed = kb * tn * 4 + TM * tn * osz
        if out is not None:
            in_specs.append(pl.BlockSpec((TM, tn), lambda i, j: (i, j)))
            args.append(out)
            streamed += TM * tn * 4
        out = pl.pallas_call(
            functools.partial(_mm_fullk_kernel, has_res=out is not None, rows_valid=min(kb, k_rows - kh * kb)),
            grid=(m // TM, n_cols // tn),
            in_specs=in_specs,
            out_specs=pl.BlockSpec((TM, tn), lambda i, j: (i, j)),
            out_shape=jax.ShapeDtypeStruct((m, n_cols), dt),
            compiler_params=_params(("parallel", "arbitrary"), streamed, single_bytes=TM * kb * 2),
            name="matmul_fullk",
        )(*args)
    return out


def _mm_fullk_nt_kernel(x_ref, wt_ref, o_ref):
    o_ref[...] = _dot_nt(x_ref[...], wt_ref[...].astype(BF16)).astype(o_ref.dtype)


def _matmul_fullk_nt(x, wt, layer, n_cols, out_dtype):
    m, kd = x.shape
    tn = 2 * TN_FULLK
    assert n_cols % tn == 0 and wt.shape[2] == kd and wt.dtype == F32
    streamed = kd * tn * 4 + TM * tn * jnp.dtype(out_dtype).itemsize
    return pl.pallas_call(
        _mm_fullk_nt_kernel,
        grid=(m // TM, n_cols // tn),
        in_specs=[_row_tile_spec(kd), pl.BlockSpec((None, tn, kd), lambda i, j: (layer, j, 0))],
        out_specs=pl.BlockSpec((TM, tn), lambda i, j: (i, j)),
        out_shape=jax.ShapeDtypeStruct((m, n_cols), out_dtype),
        compiler_params=_params(("parallel", "arbitrary"), streamed, single_bytes=TM * kd * 2),
        name="matmul_fullk_nt",
    )(x, wt)


def _ffn_up_kernel(x_ref, wg_ref, wu_ref, o_ref, *, n_real):
    j = pl.program_id(1)

    @pl.when(j < n_real)
    def _():
        x = x_ref[...]
        gate = _dot(x, wg_ref[...].astype(BF16))
        up = _dot(x, wu_ref[...].astype(BF16))
        o_ref[...] = (jax.nn.silu(gate) * up).astype(o_ref.dtype)

    @pl.when(j >= n_real)
    def _():
        o_ref[...] = jnp.zeros_like(o_ref)


def _ffn_up(x, w_gate_up, layer):
    m, kd = x.shape
    tn = TN_FULLK
    assert D_FF % tn == 0 and D_FF_PAD % tn == 0
    n_real = D_FF // tn
    col = lambda j: jnp.minimum(j, n_real - 1)
    streamed = 2 * kd * tn * 4 + TM * tn * 2
    return pl.pallas_call(
        functools.partial(_ffn_up_kernel, n_real=n_real),
        grid=(m // TM, D_FF_PAD // tn),
        in_specs=[_row_tile_spec(kd),
                  pl.BlockSpec((None, kd, tn), lambda i, j: (layer, 0, col(j))),
                  pl.BlockSpec((None, kd, tn), lambda i, j: (layer, 0, n_real + col(j)))],
        out_specs=pl.BlockSpec((TM, tn), lambda i, j: (i, j)),
        out_shape=jax.ShapeDtypeStruct((m, D_FF_PAD), BF16),
        compiler_params=_params(("parallel", "arbitrary"), streamed, single_bytes=TM * kd * 2),
        name="ffn_up",
    )(x, w_gate_up, w_gate_up)


def _mlstm_prompt_kernel(q_ref, k_ref, v_ref, o_ref, g_ref, gh_ref, hs_ref, c_ref, n_ref, m_ref):
    L, DK, DV = MLSTM_L, MLSTM_QK_DIM, MLSTM_V_DIM

    @pl.when(pl.program_id(1) == 0)
    def _():
        c_ref[...] = jnp.zeros_like(c_ref)
        n_ref[...] = jnp.zeros_like(n_ref)
        m_ref[...] = jnp.zeros_like(m_ref)

    t_idx = lax.broadcasted_iota(jnp.int32, (L, L), 0)
    s_idx = lax.broadcasted_iota(jnp.int32, (L, L), 1)
    for hd in range(MLSTM_HEADS):
        ig_row = g_ref[0, hd]
        lf_row = g_ref[1, hd]
        m_prev = m_ref[0, hd]
        lf_c = _col_form(lf_row, L)
        b_row = jnp.sum(jnp.where(t_idx <= s_idx, lf_c, 0.0), axis=0, keepdims=True)
        b_r = jnp.broadcast_to(b_row, (L, L))
        b_c = b_r.T
        ig_r = jnp.broadcast_to(ig_row, (L, L))
        log_d = jnp.where(s_idx <= t_idx, b_c - b_r + ig_r, -jnp.inf)
        b_col = b_c[:, 0:1]
        inter = b_col + m_prev
        m_t = jnp.maximum(jnp.max(log_d, axis=1, keepdims=True), inter)
        d = jnp.exp(log_d - m_t)
        w_inter = jnp.exp(inter - m_t)

        q = q_ref[:, hd * DK:(hd + 1) * DK] * (DK ** -0.5)
        k = k_ref[:, hd * DK:(hd + 1) * DK]
        qb = q.astype(BF16)
        kb = k.astype(BF16)
        vb = v_ref[:, hd * DV:(hd + 1) * DV].astype(BF16)
        s = _dot_nt(qb, kb) * d
        c_prev = c_ref[0, hd]
        n_prev = n_ref[0, hd]
        num = w_inter * _dot(qb, c_prev.astype(BF16)) + _dot(s.astype(BF16), vb)
        den = w_inter * jnp.sum(q * n_prev, axis=1, keepdims=True) + jnp.sum(s, axis=1, keepdims=True)
        h = num / jnp.maximum(jnp.abs(den), jnp.exp(-m_t))
        hn = h * lax.rsqrt(jnp.mean(h * h, axis=1, keepdims=True) + EPS)
        cols = slice(hd * DV, (hd + 1) * DV)
        hs_ref[:, cols] = (hn * gh_ref[:, cols] * jax.nn.sigmoid(o_ref[:, cols])).astype(hs_ref.dtype)

        m_new = m_t[L - 1:L, :]
        b_last = b_row[:, L - 1:L]
        ig_col = ig_r.T[:, 0:1]
        decay = jnp.exp(b_last - b_col + ig_col - m_new)
        carry = jnp.exp(b_last + m_prev - m_new)
        kd = k * decay
        c_ref[0, hd] = carry * c_prev + _dot_tn(kd.astype(BF16), vb)
        n_ref[0, hd] = carry * n_prev + jnp.sum(kd, axis=0, keepdims=True)
        m_ref[0, hd] = m_new


def _mlstm_prompt(z, gates_t, g_head):
    H, DK, DV, L = MLSTM_HEADS, MLSTM_QK_DIM, MLSTM_V_DIM, MLSTM_L
    nc = SEQ // L
    assert (H * DV) % (H * DK) == 0
    v0 = (2 * H * DK) // (H * DV)
    row = lambda b, c: b * nc + c
    blk = L * (2 * H * DK + 2 * H * DV) * 4 + L * H * DV * 2 + H * DK * DV * 4
    return pl.pallas_call(
        _mlstm_prompt_kernel,
        grid=(BATCH, nc),
        in_specs=[pl.BlockSpec((L, H * DK), lambda b, c: (row(b, c), 0)),
                  pl.BlockSpec((L, H * DK), lambda b, c: (row(b, c), 1)),
                  pl.BlockSpec((L, H * DV), lambda b, c: (row(b, c), v0)),
                  pl.BlockSpec((L, H * DV), lambda b, c: (row(b, c), v0 + 1)),
                  pl.BlockSpec((2, H, 1, L), lambda b, c: (0, 0, 0, row(b, c))),
                  pl.BlockSpec((1, H * DV), lambda b, c: (0, 0))],
        out_specs=[pl.BlockSpec((L, H * DV), lambda b, c: (row(b, c), 0)),
                   pl.BlockSpec((1, H, DK, DV), lambda b, c: (b, 0, 0, 0)),
                   pl.BlockSpec((1, H, 1, DK), lambda b, c: (b, 0, 0, 0)),
                   pl.BlockSpec((1, H, 1, 1), lambda b, c: (b, 0, 0, 0))],
        out_shape=[jax.ShapeDtypeStruct((M_PAD, H * DV), BF16),
                   jax.ShapeDtypeStruct((BATCH, H, DK, DV), F32),
                   jax.ShapeDtypeStruct((BATCH, H, 1, DK), F32),
                   jax.ShapeDtypeStruct((BATCH, H, 1, 1), F32)],
        compiler_params=_params(("parallel", "arbitrary"), blk),
        name="mlstm_prompt",
    )(z, z, z, z, gates_t, g_head.reshape(1, H * DV))


def _mlstm_sample_kernel(q_ref, k_ref, v_ref, o_ref, ig_ref, lf_ref, m0_ref, c0_ref, n0_ref, gh_ref,
                         hs_ref, c_ref, n_ref, m_ref):
    DK = MLSTM_QK_DIM
    b = pl.program_id(0)
    q = q_ref[pl.ds(b, 1), :] * (DK ** -0.5)
    k = k_ref[pl.ds(b, 1), :]
    v = v_ref[pl.ds(b, 1), :]
    og = o_ref[pl.ds(b, 1), :]
    ig = ig_ref[0, 0]
    lf = lf_ref[0, 0]
    m0 = m0_ref[0, 0]
    m_t = jnp.maximum(ig, lf + m0)
    d = jnp.exp(ig - m_t)[:, 0:1]
    w = jnp.exp(lf + m0 - m_t)[:, 0:1]
    c0 = c0_ref[0, 0]
    n0 = n0_ref[0, 0]
    q_col = _col_form(q, DK)[:, 0:1]
    k_col = _col_form(k, DK)[:, 0:1]
    s = jnp.sum(q * k, axis=1, keepdims=True) * d
    num = w * jnp.sum(q_col * c0, axis=0, keepdims=True) + s * v
    den = w * jnp.sum(q * n0, axis=1, keepdims=True) + s
    h = num / jnp.maximum(jnp.abs(den), jnp.exp(-m_t)[:, 0:1])
    hn = h * lax.rsqrt(jnp.mean(h * h, axis=1, keepdims=True) + EPS)
    hs_ref[0] = hn * gh_ref[...] * jax.nn.sigmoid(og)
    c_ref[0, 0] = w * c0 + d * (k_col * v)
    n_ref[0, 0] = w * n0 + d * k
    m_ref[0, 0] = m_t


def _mlstm_sample(z, ig_b, lf_b, m_b, c0, n0, g_head):
    H, DK, DV = MLSTM_HEADS, MLSTM_QK_DIM, MLSTM_V_DIM
    DB = DEC_BATCH
    r0 = S0 // SUBLANES
    v0 = (2 * H * DK) // DV
    sc = pl.BlockSpec((1, 1, 1, LANES), lambda b, h: (b, h, 0, 0))
    blk = 2 * DK * DV * 4 + SUBLANES * (2 * DK + 2 * DV) * 4
    return pl.pallas_call(
        _mlstm_sample_kernel,
        grid=(DB, H),
        in_specs=[pl.BlockSpec((SUBLANES, DK), lambda b, h: (r0, h)),
                  pl.BlockSpec((SUBLANES, DK), lambda b, h: (r0, H + h)),
                  pl.BlockSpec((SUBLANES, DV), lambda b, h: (r0, v0 + h)),
                  pl.BlockSpec((SUBLANES, DV), lambda b, h: (r0, v0 + H + h)),
                  sc, sc, sc,
                  pl.BlockSpec((1, 1, DK, DV), lambda b, h: (b, h, 0, 0)),
                  pl.BlockSpec((1, 1, 1, DK), lambda b, h: (b, h, 0, 0)),
                  pl.BlockSpec((1, DV), lambda b, h: (0, h))],
        out_specs=[pl.BlockSpec((1, 1, DV), lambda b, h: (b, 0, h)),
                   pl.BlockSpec((1, 1, DK, DV), lambda b, h: (b, h, 0, 0)),
                   pl.BlockSpec((1, 1, 1, DK), lambda b, h: (b, h, 0, 0)),
                   sc],
        out_shape=[jax.ShapeDtypeStruct((DB, 1, H * DV), F32),
                   jax.ShapeDtypeStruct((DB, H, DK, DV), F32),
                   jax.ShapeDtypeStruct((DB, H, 1, DK), F32),
                   jax.ShapeDtypeStruct((DB, H, 1, LANES), F32)],
        compiler_params=_params(("parallel", "parallel"), blk),
        name="mlstm_sample",
    )(z, z, z, z, ig_b, lf_b, m_b, c0, n0, g_head.reshape(1, H * DV))


def _compress_proj_kernel(*refs, n_in, groups_per_in, lead):
    x_refs = refs[-(n_in + 3):-3]
    w_ref, p_ref, xp_scr = refs[-3], refs[-2], refs[-1]
    G, D = NSA_KV_HEADS, HEAD_DIM
    cpg = PERM_ROWS // CMP_STRIDE
    nch = n_in * groups_per_in * cpg
    i_idx = lax.broadcasted_iota(jnp.int32, (PERM_ROWS, PERM_ROWS), 0)
    j_idx = lax.broadcasted_iota(jnp.int32, (PERM_ROWS, PERM_ROWS), 1)
    src_row = (i_idx & (cpg - 1)) * CMP_STRIDE + (i_idx >> (cpg.bit_length() - 1))
    perm = (j_idx == src_row).astype(BF16)
    for i, xr in enumerate(x_refs):
        for p in range(groups_per_in):
            rows = pl.ds(p * PERM_ROWS, PERM_ROWS)
            xg = xr[0, rows, :] if lead else xr[rows, :]
            xg = _dot(perm, xg.astype(BF16))
            ch0 = (i * groups_per_in + p) * cpg
            for c in range(CMP_STRIDE):
                xp_scr[c, ch0:ch0 + cpg, :] = xg[c * cpg:(c + 1) * cpg, :]
    for s in range(2):
        acc = jnp.zeros((G * nch, CMP_R * CMP_HID), F32)
        for c in range(CMP_STRIDE):
            xs = jnp.concatenate([xp_scr[c, :, (s * G + g) * D:(s * G + g + 1) * D] for g in range(G)], axis=0)
            acc = acc + _dot(xs.astype(BF16), w_ref[s, c])
        for g in range(G):
            p_ref[0, s, g] = acc[g * nch:(g + 1) * nch]


def _compress_weights(w_cmp1):
    w = w_cmp1.reshape(2, CMP_R, CMP_STRIDE, HEAD_DIM, CMP_HID)
    return w.transpose(0, 2, 3, 1, 4).reshape(2, CMP_STRIDE, HEAD_DIM, CMP_R * CMP_HID).astype(BF16)


def _compress_proj_prompt(rows, w1r):
    G, D = NSA_KV_HEADS, HEAD_DIM
    nch = SEQ // CMP_STRIDE
    pw = CMP_R * CMP_HID
    blk = SEQ * 2 * G * D * 4 + w1r.size * 2 + 2 * G * nch * pw * 4 + SEQ * 2 * G * D * 4
    return pl.pallas_call(
        functools.partial(_compress_proj_kernel, n_in=1, groups_per_in=SEQ // PERM_ROWS, lead=False),
        grid=(BATCH,),
        in_specs=[pl.BlockSpec((SEQ, 2 * G * D), lambda b: (b, 0)),
                  pl.BlockSpec(w1r.shape, lambda b: (0, 0, 0, 0))],
        out_specs=pl.BlockSpec((1, 2, G, nch, pw), lambda b: (b, 0, 0, 0, 0)),
        out_shape=jax.ShapeDtypeStruct((BATCH, 2, G, nch, pw), F32),
        scratch_shapes=[pltpu.VMEM((CMP_STRIDE, nch, 2 * G * D), F32)],
        compiler_params=_params(("parallel",), blk),
        name="compress_proj_prompt",
    )(rows, w1r)


def _compress_proj_sample(cache2d, page_table_flat, w1r):
    G, D = NSA_KV_HEADS, HEAD_DIM
    n_pages = PAST_LEN // PAGE_SIZE
    pps = SAMPLE_PAGES_PER_STEP
    assert n_pages % pps == 0
    cpi = PAGE_SIZE // CMP_STRIDE
    nch_step = pps * cpi
    pw = CMP_R * CMP_HID

    def page_map(b, j, pt, *, i):
        return (pt[b * n_pages + j * pps + i], 0, 0)

    in_specs = [pl.BlockSpec((1, PAGE_SIZE, 2 * G * D), functools.partial(page_map, i=i)) for i in range(pps)]
    in_specs.append(pl.BlockSpec(w1r.shape, lambda b, j, pt: (0, 0, 0, 0)))
    blk = 2 * pps * PAGE_SIZE * 2 * G * D * 4 + w1r.size * 2 + 2 * G * nch_step * pw * 4
    return pl.pallas_call(
        functools.partial(_compress_proj_kernel, n_in=pps, groups_per_in=PAGE_SIZE // PERM_ROWS, lead=True),
        grid_spec=pltpu.PrefetchScalarGridSpec(
            num_scalar_prefetch=1,
            grid=(DEC_BATCH, n_pages // pps),
            in_specs=in_specs,
            out_specs=pl.BlockSpec((1, 2, G, nch_step, pw), lambda b, j, pt: (b, 0, 0, j, 0)),
            scratch_shapes=[pltpu.VMEM((CMP_STRIDE, nch_step, 2 * G * D), F32)],
        ),
        out_shape=jax.ShapeDtypeStruct((DEC_BATCH, 2, G, n_pages * cpi, pw), F32),
        compiler_params=_params(("parallel", "arbitrary"), blk),
        name="compress_proj_sample",
    )(page_table_flat, *([cache2d] * pps), w1r)


def _compress_finish_kernel(p_ref, pe_ref, w1_ref, w2_ref, o_ref):
    nch = p_ref.shape[3]
    pe = jnp.broadcast_to(pe_ref[0], (SUBLANES, pe_ref.shape[2])).astype(BF16)
    pe_term = _dot(pe, w1_ref[0].astype(BF16))[0:1]
    w2 = w2_ref[0].astype(BF16)
    for g in range(NSA_KV_HEADS):
        p = p_ref[0, 0, g]
        nxt = pltpu.roll(p[:, CMP_HID:], shift=nch - 1, axis=0)
        hidden = p[:, :CMP_HID] + nxt + pe_term
        o_ref[0, 0, g] = _dot(jax.nn.gelu(hidden).astype(BF16), w2)


def _compress_finish(p, pe_cmp, w_cmp1, w_cmp2):
    nb, _, G, nch, pw = p.shape
    pe = pe_cmp.reshape(2, 1, CMP_LEN * HEAD_DIM)
    blk = G * nch * (pw + HEAD_DIM) * 4 + CMP_LEN * HEAD_DIM * (CMP_HID + 1) * 4
    return pl.pallas_call(
        _compress_finish_kernel,
        grid=(nb, 2),
        in_specs=[pl.BlockSpec((1, 1, G, nch, pw), lambda b, s: (b, s, 0, 0, 0)),
                  pl.BlockSpec((1, 1, CMP_LEN * HEAD_DIM), lambda b, s: (s, 0, 0)),
                  pl.BlockSpec((1, CMP_LEN * HEAD_DIM, CMP_HID), lambda b, s: (s, 0, 0)),
                  pl.BlockSpec((1, CMP_HID, HEAD_DIM), lambda b, s: (s, 0, 0))],
        out_specs=pl.BlockSpec((1, 1, G, nch, HEAD_DIM), lambda b, s: (b, s, 0, 0, 0)),
        out_shape=jax.ShapeDtypeStruct((nb, 2, G, nch, HEAD_DIM), F32),
        compiler_params=_params(("parallel", "parallel"), blk),
        name="compress_finish",
    )(p, pe, w_cmp1, w_cmp2)


def _selection_overlap(n_cmp, n_sel, rows, cols):
    start = np.arange(n_cmp)[:, None] * CMP_STRIDE
    end = start + CMP_LEN - 1
    j = np.arange(n_sel)[None, :]
    ov = ((start <= (j + 1) * SEL_LEN - 1) & (end >= j * SEL_LEN)).astype(np.float32)
    out = np.zeros((rows, cols), np.float32)
    out[:n_cmp, :n_sel] = ov
    return jnp.asarray(out)


def _masked_softmax(s, valid):
    sm = jnp.where(valid, s, NEG)
    e = jnp.exp(sm - jnp.max(sm, axis=-1, keepdims=True))
    return jnp.where(valid, e / jnp.sum(e, axis=-1, keepdims=True), 0.0)


def _flash_step(carry, s, valid, vb):
    m, l, acc = carry
    sm = jnp.where(valid, s, NEG)
    m_new = jnp.maximum(m, jnp.max(sm, axis=-1, keepdims=True))
    alpha = jnp.exp(m - m_new)
    e = jnp.exp(sm - m_new)
    l = alpha * l + jnp.sum(e, axis=-1, keepdims=True)
    acc = alpha * acc + _dot(jnp.where(valid, e, 0.0).astype(BF16), vb)
    return m_new, l, acc


def _flash_one(carry, s_new, v_new):
    m, l, acc = carry
    m_new = jnp.maximum(m, s_new)
    alpha = jnp.exp(m - m_new)
    e = jnp.exp(s_new - m_new)
    return m_new, alpha * l + e, alpha * acc + e * v_new


def _flash_init(rows):
    return (jnp.full((rows, 1), -jnp.inf, F32), jnp.zeros((rows, 1), F32), jnp.zeros((rows, HEAD_DIM), F32))


def _top_rank(impm_c, impm_r, jp_idx, j_idx, axis):
    before = (impm_c > impm_r) | ((impm_c == impm_r) & (jp_idx < j_idx))
    return jnp.sum(before.astype(jnp.int32), axis=axis, keepdims=True)


FEAT_SIG = 96
FEAT_SIG_FULL = FEAT_SIG + 9
MASK_BIG = 2.0 ** 100
POS_RADIX = 256
LOG2E = 1.4426950408889634
assert SEQ // SEL_LEN <= FEAT_SIG and SEQ >= WINDOW + TQ


def _key_features(pos, block):
    n = pos.shape[0]
    kf = np.zeros((n, LANES), np.float32)
    if block is not None:
        kf[np.arange(n), block] = -MASK_BIG
    kf[:, FEAT_SIG:FEAT_SIG + 3] = (pos % POS_RADIX)[:, None]
    kf[:, FEAT_SIG + 3:FEAT_SIG + 6] = (pos // POS_RADIX)[:, None]
    kf[:, FEAT_SIG + 6:FEAT_SIG + 9] = 1.0
    return jnp.asarray(kf, BF16)


def _sigma_table():
    slopes = np.exp2(-8.0 * np.arange(1, NSA_HEADS + 1, dtype=np.float32) / NSA_HEADS).astype(np.float32)
    sigma = (slopes / np.float32(HEAD_DIM ** -0.5)).astype(np.float32)
    bf = lambda a: np.asarray(np.asarray(a, dtype=BF16), dtype=np.float32)
    s1 = bf(sigma)
    s2 = bf(sigma - s1)
    s3 = bf(sigma - s1 - s2)
    tab = np.zeros((NSA_HEADS, LANES), np.float32)
    for i, s in enumerate((s1, s2, s3)):
        tab[:, FEAT_SIG + i] = s
        tab[:, FEAT_SIG + 3 + i] = s * POS_RADIX
    tab[:, FEAT_SIG_FULL] = sigma
    return jnp.asarray(tab)


def _nsa_prompt_kernel(q_ref, gl_ref, sig_ref, kc_ref, vc_ref, ks_ref, vs_ref, kw_ref, vw_ref,
                       ov_ref, kfc_ref, kfs_ref, kfw_ref, out_ref,
                       qa_scr, kca_scr, vcb_scr, ksa_scr, vsb_scr, kwa_scr, vwb_scr, m_scr, acc_scr):
    qi = pl.program_id(2)
    D = HEAD_DIM
    c2 = (D ** -0.5) * LOG2E
    n_sel = SEQ // SEL_LEN
    qs = qi * TQ
    tpos = qs + lax.broadcasted_iota(jnp.int32, (TQ, 1), 0)
    tposf = tpos.astype(F32)
    lane = lax.broadcasted_iota(jnp.int32, (1, LANES), 1)
    heads = [slice(r * TQ, (r + 1) * TQ) for r in range(GRP)]

    @pl.when(qi == 0)
    def _():
        kca_scr[:, 0:D] = kc_ref[0, 0, 0].astype(BF16)
        kca_scr[:, D:2 * D] = kfc_ref[...]
        vcb_scr[...] = vc_ref[0, 0, 0].astype(BF16)
        ksa_scr[:, 0:D] = ks_ref[...].astype(BF16)
        ksa_scr[:, D:2 * D] = kfs_ref[...]
        kwa_scr[:, 0:D] = kw_ref[...].astype(BF16)
        kwa_scr[:, D:2 * D] = kfw_ref[...]
        ones_col = jnp.where(lax.broadcasted_iota(jnp.int32, (SEQ, D), 1) == 0, 1.0, 0.0).astype(BF16)
        vsb_scr[:, 0:D] = vs_ref[...].astype(BF16)
        vsb_scr[:, D:2 * D] = ones_col
        vwb_scr[:, 0:D] = vw_ref[...].astype(BF16)
        vwb_scr[:, D:2 * D] = ones_col

    def write_features(nsel):
        for r in range(GRP):
            sig_row = sig_ref[r:r + 1, :]
            c = -sig_row[:, FEAT_SIG_FULL:FEAT_SIG_FULL + 1] * tposf
            c_hi = c.astype(BF16).astype(F32)
            c_mid = (c - c_hi).astype(BF16).astype(F32)
            c_lo = c - c_hi - c_mid
            f = jnp.where(lane == FEAT_SIG + 6, c_hi,
                          jnp.where(lane == FEAT_SIG + 7, c_mid, jnp.where(lane == FEAT_SIG + 8, c_lo, sig_row)))
            if nsel is not None:
                f = jnp.where(lane < n_sel, nsel, f)
            qa_scr[heads[r], D:2 * D] = f.astype(BF16)

    for r in range(GRP):
        qa_scr[heads[r], 0:D] = q_ref[:, r * D:(r + 1) * D]
    write_features(None)

    ncp = kca_scr.shape[0]
    cpos = lax.broadcasted_iota(jnp.int32, (1, ncp), 1) * CMP_STRIDE + (CMP_LEN - 1)
    valid_c = cpos <= tpos
    s_c = _dot_nt(qa_scr[...], kca_scr[...])
    p_sum = jnp.zeros((TQ, ncp), F32)
    p_c = []
    for r in range(GRP):
        sm = jnp.where(valid_c, s_c[heads[r]], NEG)
        e = jnp.exp2((sm - jnp.max(sm, axis=-1, keepdims=True)) * c2)
        p = jnp.where(valid_c, e / jnp.sum(e, axis=-1, keepdims=True), 0.0)
        p_c.append(p.astype(BF16))
        p_sum = p_sum + p
    o_c = _dot(jnp.concatenate(p_c, axis=0), vcb_scr[...])
    imp = _dot_exact(p_sum, ov_ref[...])

    cur = tpos // SEL_LEN
    forced = (lane == 0) | (lane == cur) | (lane == cur - 1)
    impm = jnp.where(forced, jnp.inf, jnp.where(lane <= cur, imp, -jnp.inf))
    imp_t = impm.T[0:n_sel]
    j_sub = lax.broadcasted_iota(jnp.int32, (n_sel, TQ), 0)
    rank_t = jnp.zeros((n_sel, TQ), jnp.int32)
    for jp in range(n_sel):
        cj = imp_t[jp:jp + 1, :]
        rank_t = rank_t + ((cj > imp_t) | ((cj == imp_t) & (jp < j_sub))).astype(jnp.int32)
    unsel_t = jnp.where(rank_t < min(SEL_TOP, n_sel), 0.0, 1.0)
    unsel = jnp.concatenate([unsel_t, jnp.ones((LANES - n_sel, TQ), F32)], axis=0).T
    write_features(jnp.where(lane <= cur, unsel, 1.0))

    m_scr[...] = jnp.full(m_scr.shape, -jnp.inf, F32)
    acc_scr[...] = jnp.zeros(acc_scr.shape, F32)

    def sel_tile(kt, bias):
        k0 = pl.multiple_of(kt * TKS, TKS)
        s_all = _dot_nt(qa_scr[...], ksa_scr[pl.ds(k0, TKS), :])
        p_all = []
        for r in range(GRP):
            s = s_all[heads[r]]
            if bias is not None:
                s = s + bias
            m_old = m_scr[heads[r]]
            m_new = jnp.maximum(m_old, jnp.max(s, axis=-1, keepdims=True))
            alpha = jnp.exp2((m_old - m_new) * c2)
            e = jnp.exp2((s - m_new) * c2)
            m_scr[heads[r]] = m_new
            acc_scr[heads[r]] = alpha * acc_scr[heads[r]]
            p_all.append(e.astype(BF16))
        acc_scr[...] += _dot(jnp.concatenate(p_all, axis=0), vsb_scr[pl.ds(k0, TKS), :])

    n_past = qs // TKS

    def past_tile(kt, carry):
        sel_tile(kt, None)
        return carry

    lax.fori_loop(0, n_past, past_tile, 0)
    spos_d = n_past * TKS + lax.broadcasted_iota(jnp.int32, (1, TKS), 1)
    sel_tile(n_past, jnp.where(spos_d <= tpos, 0.0, -MASK_BIG))
    o_s = acc_scr[:, 0:D] / acc_scr[:, D:D + 1]

    wk = WINDOW + TQ
    w0 = pl.multiple_of(jnp.maximum(qs - WINDOW, 0), TQ)
    dist_w = tpos - (w0 + lax.broadcasted_iota(jnp.int32, (1, wk), 1))
    bias_w = jnp.where((dist_w >= 0) & (dist_w <= WINDOW), 0.0, -MASK_BIG)
    s_w = _dot_nt(qa_scr[...], kwa_scr[pl.ds(w0, wk), :])
    p_w = []
    for r in range(GRP):
        s = s_w[heads[r]] + bias_w
        p_w.append(jnp.exp2((s - jnp.max(s, axis=-1, keepdims=True)) * c2).astype(BF16))
    pv_w = _dot(jnp.concatenate(p_w, axis=0), vwb_scr[pl.ds(w0, wk), :])
    o_w = pv_w[:, 0:D] / pv_w[:, D:D + 1]

    for r in range(GRP):
        gates = jax.nn.sigmoid(gl_ref[0, :, 3 * r:3 * r + 3])
        out = gates[:, 0:1] * o_c[heads[r]] + gates[:, 1:2] * o_s[heads[r]] + gates[:, 2:3] * o_w[heads[r]]
        out_ref[:, r * D:(r + 1) * D] = out.astype(out_ref.dtype)


def _nsa_prompt(q, gate_logits, kvc, rows, ov):
    G, D = NSA_KV_HEADS, HEAD_DIM
    nq = SEQ // TQ
    ncp = kvc.shape[3]
    R = GRP * TQ
    spos = np.arange(SEQ)
    kf_cmp = _key_features(np.arange(ncp) * CMP_STRIDE + (CMP_LEN - 1), None)
    kf_sel = _key_features(spos, spos // SEL_LEN)
    kf_win = _key_features(spos, None)
    row = lambda b, g, qi: b * nq + qi
    kv_spec = lambda slot: pl.BlockSpec((SEQ, D), lambda b, g, qi: (b, slot * G + g))
    const = lambda a: pl.BlockSpec(a.shape, lambda b, g, qi: (0, 0))
    blk = (TQ * GRP * D * 4 + 2 * ncp * D * 4 + 4 * SEQ * D * 4 + 3 * SEQ * D * 2 + SEQ * D * 6 + R * D * 16
           + R * (TKS + WINDOW + TQ) * 6)
    return pl.pallas_call(
        _nsa_prompt_kernel,
        grid=(BATCH, G, nq),
        in_specs=[pl.BlockSpec((TQ, GRP * D), lambda b, g, qi: (row(b, g, qi), g)),
                  pl.BlockSpec((1, TQ, 3 * GRP), lambda b, g, qi: (g, row(b, g, qi), 0)),
                  pl.BlockSpec((GRP, LANES), lambda b, g, qi: (g, 0)),
                  pl.BlockSpec((1, 1, 1, ncp, D), lambda b, g, qi: (b, 0, g, 0, 0)),
                  pl.BlockSpec((1, 1, 1, ncp, D), lambda b, g, qi: (b, 1, g, 0, 0)),
                  kv_spec(2), kv_spec(3), kv_spec(4), kv_spec(5),
                  const(ov), const(kf_cmp), const(kf_sel), const(kf_win)],
        out_specs=pl.BlockSpec((TQ, GRP * D), lambda b, g, qi: (row(b, g, qi), g)),
        out_shape=jax.ShapeDtypeStruct((M_PAD, NSA_HEADS * D), BF16),
        scratch_shapes=[pltpu.VMEM((R, 2 * D), BF16),
                        pltpu.VMEM((ncp, 2 * D), BF16), pltpu.VMEM((ncp, D), BF16),
                        pltpu.VMEM((SEQ, 2 * D), BF16), pltpu.VMEM((SEQ, 2 * D), BF16),
                        pltpu.VMEM((SEQ, 2 * D), BF16), pltpu.VMEM((SEQ, 2 * D), BF16),
                        pltpu.VMEM((R, 1), F32), pltpu.VMEM((R, 2 * D), F32)],
        compiler_params=_params(("parallel", "parallel", "arbitrary"), blk),
        name="nsa_prompt",
    )(q, gate_logits, _sigma_table(), kvc, kvc, rows, rows, rows, rows, ov, kf_cmp, kf_sel, kf_win)


T_SAMPLE = PAST_LEN
N_SEL_S = -(-(PAST_LEN + 1) // SEL_LEN)
NB_PAST = PAST_LEN // SEL_LEN
KK_S = min(SEL_TOP, N_SEL_S)
SEL_PAD_S = _round_up(N_SEL_S, LANES)
assert PAST_LEN % SEL_LEN == 0 and PAST_LEN % CMP_STRIDE == 0


def _slope_col(slopes_ref, g):
    r_idx = lax.broadcasted_iota(jnp.int32, (GRP, 1), 0)
    col = jnp.zeros((GRP, 1), F32)
    for r in range(GRP):
        col = jnp.where(r_idx == r, slopes_ref[g * GRP + r], col)
    return col


def _nsa_sample_cmp_win_kernel(slopes_ref, q_ref, kc_ref, vc_ref, ov_ref, kw_ref, vw_ref, kwn_ref, vwn_ref,
                               oc_ref, ow_ref, idx_ref):
    g = pl.program_id(1)
    scale = HEAD_DIM ** -0.5
    q = q_ref[0, 0]
    slope = _slope_col(slopes_ref, g)

    kcb = kc_ref[0, 0, 0].astype(BF16)
    ncp = kcb.shape[0]
    cpos = lax.broadcasted_iota(jnp.int32, (1, ncp), 1) * CMP_STRIDE + (CMP_LEN - 1)
    dist_c = T_SAMPLE - cpos
    p_c = _masked_softmax(_dot_nt(q, kcb) * scale - slope * dist_c.astype(F32), dist_c >= 0)
    oc_ref[0, 0] = _dot(p_c.astype(BF16), vc_ref[0, 0, 0].astype(BF16))
    p_sum = jnp.broadcast_to(jnp.sum(p_c, axis=0, keepdims=True), (SUBLANES, ncp))
    imp = _dot_exact(p_sum, ov_ref[...])[0:1]

    NP = SEL_PAD_S
    cur = T_SAMPLE // SEL_LEN
    j_row = lax.broadcasted_iota(jnp.int32, (1, NP), 1)
    forced = (j_row == 0) | (j_row == cur) | (j_row == cur - 1)
    impm = jnp.where(forced, jnp.inf, jnp.where(j_row <= cur, imp, -jnp.inf))
    sub = lax.broadcasted_iota(jnp.int32, (NP, NP), 0)
    lane = lax.broadcasted_iota(jnp.int32, (NP, NP), 1)
    impm_r = jnp.broadcast_to(impm, (NP, NP))
    rank = _top_rank(impm_r.T, impm_r, sub, lane, 0).astype(F32)
    rank_c = jnp.broadcast_to(rank, (NP, NP)).T
    hit = (rank_c == lane.astype(F32)) & (sub <= cur)
    idx_ref[0, 0] = jnp.sum(jnp.where(hit, sub, 0), axis=0, keepdims=True)

    kwb = kw_ref[0].astype(BF16)
    wlen = kwb.shape[0]
    wpos = T_SAMPLE - wlen + lax.broadcasted_iota(jnp.int32, (1, wlen), 1)
    dist_w = T_SAMPLE - wpos
    valid_w = (dist_w >= 0) & (dist_w <= WINDOW) & (wpos >= 0)
    carry = _flash_step(_flash_init(GRP), _dot_nt(q, kwb) * scale - slope * dist_w.astype(F32), valid_w,
                        vw_ref[0].astype(BF16))
    k_new = kwn_ref[0, 0, 0].astype(BF16).astype(F32)
    v_new = vwn_ref[0, 0, 0].astype(BF16).astype(F32)
    s_new = jnp.sum(q.astype(F32) * k_new, axis=1, keepdims=True) * scale
    _, l_w, acc_w = _flash_one(carry, s_new, v_new)
    ow_ref[0, 0] = acc_w / l_w


def _nsa_sample_cmp_win(slopes, q_s, kvc, ov, win2d, rows_s):
    G, D, DB = NSA_KV_HEADS, HEAD_DIM, DEC_BATCH
    ncp = kvc.shape[3]
    wlen = win2d.shape[1]
    o_spec = pl.BlockSpec((1, 1, GRP, D), lambda b, g, sl: (b, g, 0, 0))
    blk = 2 * ncp * D * 4 + ov.size * 4 + 2 * wlen * D * 4 + 4 * SEL_PAD_S * SEL_PAD_S * 4
    return pl.pallas_call(
        _nsa_sample_cmp_win_kernel,
        grid_spec=pltpu.PrefetchScalarGridSpec(
            num_scalar_prefetch=1,
            grid=(DB, G),
            in_specs=[pl.BlockSpec((1, 1, GRP, D), lambda b, g, sl: (b, g, 0, 0)),
                      pl.BlockSpec((1, 1, 1, ncp, D), lambda b, g, sl: (b, 0, g, 0, 0)),
                      pl.BlockSpec((1, 1, 1, ncp, D), lambda b, g, sl: (b, 1, g, 0, 0)),
                      pl.BlockSpec(ov.shape, lambda b, g, sl: (0, 0)),
                      pl.BlockSpec((1, wlen, D), lambda b, g, sl: (b, 0, g)),
                      pl.BlockSpec((1, wlen, D), lambda b, g, sl: (b, 0, G + g)),
                      pl.BlockSpec((1, 1, 1, 1, D), lambda b, g, sl: (b, 4, g, 0, 0)),
                      pl.BlockSpec((1, 1, 1, 1, D), lambda b, g, sl: (b, 5, g, 0, 0))],
            out_specs=[o_spec, o_spec, pl.BlockSpec((1, 1, 1, SEL_PAD_S), lambda b, g, sl: (b, g, 0, 0))],
        ),
        out_shape=[jax.ShapeDtypeStruct((DB, G, GRP, D), F32),
                   jax.ShapeDtypeStruct((DB, G, GRP, D), F32),
                   jax.ShapeDtypeStruct((DB, G, 1, SEL_PAD_S), jnp.int32)],
        compiler_params=_params(("parallel", "parallel"), blk),
        name="nsa_sample_cmp_win",
    )(slopes, q_s, kvc, kvc, ov, win2d, win2d, rows_s, rows_s)


def _nsa_sample_sel_kernel(pt_ref, idx_ref, slopes_ref, q_ref, *refs):
    ks_refs, vs_refs = refs[:KK_S], refs[KK_S:2 * KK_S]
    ksn_ref, vsn_ref, oc_ref, ow_ref, gl_ref, out_ref = refs[2 * KK_S:]
    b, g = pl.program_id(0), pl.program_id(1)
    scale = HEAD_DIM ** -0.5
    q = q_ref[0, 0]
    slope = _slope_col(slopes_ref, g)
    off = lax.broadcasted_iota(jnp.int32, (1, SEL_LEN), 1)

    scores, valids = [], []
    has_new = False
    for kk in range(KK_S):
        j = idx_ref[(b * NSA_KV_HEADS + g) * KK_S + kk]
        in_cache = j < NB_PAST
        has_new = jnp.logical_or(has_new, jnp.logical_not(in_cache))
        dist = T_SAMPLE - (jnp.minimum(j, NB_PAST - 1) * SEL_LEN + off)
        valid = (dist >= 0) & in_cache
        s = _dot_nt(q, ks_refs[kk][0].astype(BF16)) * scale - slope * dist.astype(F32)
        scores.append(jnp.where(valid, s, NEG))
        valids.append(valid)
    k_new = ksn_ref[0, 0, 0].astype(BF16).astype(F32)
    v_new = vsn_ref[0, 0, 0].astype(BF16).astype(F32)
    s_new = jnp.sum(q.astype(F32) * k_new, axis=1, keepdims=True) * scale
    s_new = jnp.where(has_new, s_new, NEG)

    m = s_new
    for s in scores:
        m = jnp.maximum(m, jnp.max(s, axis=-1, keepdims=True))
    e_new = jnp.exp(s_new - m)
    l = e_new
    acc = jnp.where(has_new, e_new, 0.0) * v_new
    for kk in range(KK_S):
        e = jnp.exp(scores[kk] - m)
        l = l + jnp.sum(e, axis=-1, keepdims=True)
        acc = acc + _dot(jnp.where(valids[kk], e, 0.0).astype(BF16), vs_refs[kk][0].astype(BF16))
    gates = jax.nn.sigmoid(gl_ref[0, 0])
    out_ref[0, 0] = gates[0] * oc_ref[0, 0] + gates[1] * (acc / l) + gates[2] * ow_ref[0, 0]


def _nsa_sample_sel(page_table_flat, idx_flat, slopes, q_s, cache2d, rows_s, o_c, o_w, gate_logits_s):
    G, D, DB = NSA_KV_HEADS, HEAD_DIM, DEC_BATCH
    n_pages = PAST_LEN // PAGE_SIZE
    bpp = PAGE_SIZE // SEL_LEN

    def blk_map(b, g, pt, idx, sl, *, kk, slot):
        j = jnp.minimum(idx[(b * G + g) * KK_S + kk], NB_PAST - 1)
        return (pt[b * n_pages + j // bpp], j % bpp, slot * G + g)

    o_spec = pl.BlockSpec((1, 1, GRP, D), lambda b, g, pt, idx, sl: (b, g, 0, 0))
    blk_specs = [pl.BlockSpec((1, SEL_LEN, D), functools.partial(blk_map, kk=kk, slot=slot))
                 for slot in (2, 3) for kk in range(KK_S)]
    blk = 2 * KK_S * SEL_LEN * D * 4 + 6 * GRP * D * 4
    return pl.pallas_call(
        _nsa_sample_sel_kernel,
        grid_spec=pltpu.PrefetchScalarGridSpec(
            num_scalar_prefetch=3,
            grid=(DB, G),
            in_specs=[o_spec, *blk_specs,
                      pl.BlockSpec((1, 1, 1, 1, D), lambda b, g, pt, idx, sl: (b, 2, g, 0, 0)),
                      pl.BlockSpec((1, 1, 1, 1, D), lambda b, g, pt, idx, sl: (b, 3, g, 0, 0)),
                      o_spec, o_spec,
                      pl.BlockSpec((1, 1, 3, GRP, 1), lambda b, g, pt, idx, sl: (b, g, 0, 0, 0))],
            out_specs=o_spec,
        ),
        out_shape=jax.ShapeDtypeStruct((DB, G, GRP, D), F32),
        compiler_params=_params(("parallel", "parallel"), blk),
        name="nsa_sample_sel",
    )(page_table_flat, idx_flat, slopes, q_s, *([cache2d] * (2 * KK_S)), rows_s, rows_s, o_c, o_w, gate_logits_s)


def _stack_rows(prompt_rows, sample_rows):
    pad = jnp.zeros((M_PAD - M_PROMPT - M_SAMPLE, prompt_rows.shape[1]), prompt_rows.dtype)
    return jnp.concatenate([prompt_rows, sample_rows.astype(prompt_rows.dtype), pad], axis=0)


def _tail_rows_kernel(buf_ref, s_ref, o_ref):
    del buf_ref
    ns = s_ref.shape[0]
    o_ref[0:ns, :] = s_ref[...]
    o_ref[ns:, :] = jnp.zeros((o_ref.shape[0] - ns, o_ref.shape[1]), o_ref.dtype)


def _fill_tail_rows(buf, sample_rows):
    tail = M_PAD - M_PROMPT
    ns = _round_up(M_SAMPLE, 16)
    assert M_PROMPT % tail == 0 and ns <= tail and buf.dtype == BF16
    s = jnp.pad(sample_rows.astype(BF16), ((0, ns - M_SAMPLE), (0, 0)))
    cols = buf.shape[1]
    return pl.pallas_call(
        _tail_rows_kernel,
        grid=(1,),
        in_specs=[pl.BlockSpec(memory_space=pl.ANY), pl.BlockSpec((ns, cols), lambda i: (0, 0))],
        out_specs=pl.BlockSpec((tail, cols), lambda i: (M_PROMPT // tail, 0)),
        out_shape=jax.ShapeDtypeStruct(buf.shape, buf.dtype),
        input_output_aliases={0: 0},
        compiler_params=_params(("arbitrary",), (tail + ns) * cols * 2),
        name="fill_tail_rows",
    )(buf, s)


def _ffn_block(x, g, w_gate_up, w_down, layer):
    h = _rmsnorm(x, g, BF16)
    return _matmul_fullk(_ffn_up(h, w_gate_up, layer), w_down, layer, D_MODEL, F32, residual=x, k_split=2)


def kernel(x_prompt, x_sample, cache_nsa_kv, page_table, state_win_kv, state_mlstm_C, state_mlstm_n, state_mlstm_m,
           g_mix, g_ffn, g_kv, g_final, w_in_a, b_gate_a, g_head_a, w_out_a, w_qg_b, w_out_b, w_kv,
           pe_cmp, w_cmp1, w_cmp2, w_gate_up, w_down):
    H, DK, DV = MLSTM_HEADS, MLSTM_QK_DIM, MLSTM_V_DIM
    G, D, DB = NSA_KV_HEADS, HEAD_DIM, DEC_BATCH
    hd = NSA_HEADS * D
    n_main = 2 * H * DK + 2 * H * DV
    x = _stack_rows(x_prompt.reshape(M_PROMPT, D_MODEL), x_sample.reshape(M_SAMPLE, D_MODEL))
    w_in_t = jnp.swapaxes(w_in_a, 1, 2)
    w_qg_t = jnp.swapaxes(w_qg_b, 1, 2)
    w_gate_a = jnp.swapaxes(w_in_t[:, n_main:, :], 1, 2)
    w_gate_b = jnp.swapaxes(w_qg_t[:, hd:, :], 1, 2)

    c_p, n_p, m_p, c_s, n_s, m_s = [], [], [], [], [], []
    for l in range(N_A):
        h, gates = _rmsnorm_with_gates(x, g_mix[l], w_gate_a[l], b_gate_a[l], True)
        z = _matmul_fullk_nt(h, w_in_t, l, n_main, F32)
        gates_t = gates[:M_PROMPT, :2 * H].T.reshape(2, H, 1, M_PROMPT)
        hs_p, cp, np_, mp = _mlstm_prompt(z, gates_t, g_head_a[l])
        gs = gates[S0:S0 + DB, :2 * H]
        lane_b = lambda a: jnp.broadcast_to(a[:, :, None, None], (DB, H, 1, LANES))
        hs_s, cs, ns, ms = _mlstm_sample(z, lane_b(gs[:, :H]), lane_b(gs[:, H:]), lane_b(state_mlstm_m[l]),
                                         state_mlstm_C[l], state_mlstm_n[l].reshape(DB, H, 1, DK), g_head_a[l])
        hs = _fill_tail_rows(hs_p, hs_s.reshape(DB, H * DV))
        x = _matmul_fullk(hs, w_out_a, l, D_MODEL, F32, residual=x)
        x = _ffn_block(x, g_ffn[l], w_gate_up, w_down, l)
        c_p.append(cp)
        n_p.append(np_.reshape(BATCH, H, DK))
        m_p.append(mp.reshape(BATCH, H))
        c_s.append(cs)
        n_s.append(ns.reshape(DB, H, DK))
        m_s.append(ms[:, :, 0, 0])

    rows = _matmul_fullk(_rmsnorm(x, g_kv, BF16), w_kv[None], 0, 6 * G * D, F32)
    rows_s = rows[S0:S0 + DB].reshape(DB, 6, G, 1, D)
    w1r = _compress_weights(w_cmp1)
    kvc_p = _compress_finish(_compress_proj_prompt(rows, w1r), pe_cmp, w_cmp1, w_cmp2)
    n_pool = cache_nsa_kv.shape[0]
    pt_flat = page_table.reshape(-1).astype(jnp.int32)
    cache2d = cache_nsa_kv.reshape(n_pool, PAGE_SIZE, 4 * G * D)
    kvc_s = _compress_finish(_compress_proj_sample(cache2d, pt_flat, w1r), pe_cmp, w_cmp1, w_cmp2)
    win2d = state_win_kv.reshape(DB, state_win_kv.shape[1], 2 * G * D)

    slopes = jnp.exp2(-8.0 * jnp.arange(1, NSA_HEADS + 1, dtype=F32) / NSA_HEADS)
    n_chunks_p = SEQ // CMP_STRIDE
    ov_p = _selection_overlap(n_chunks_p - CMP_R + 1, SEQ // SEL_LEN, n_chunks_p, LANES)
    n_chunks_s = (PAST_LEN + M_SAMPLE // DB) // CMP_STRIDE
    ov_s = _selection_overlap(n_chunks_s - CMP_R + 1, N_SEL_S, n_chunks_s, SEL_PAD_S)

    for l in range(N_B):
        h, gate_logits = _rmsnorm_with_gates(x, g_mix[N_A + l], w_gate_b[l], jnp.zeros((3 * NSA_HEADS,), F32),
                                             False)
        q = _matmul_fullk_nt(h, w_qg_t, l, hd, BF16)
        gl = gate_logits[:, :3 * NSA_HEADS]
        gl_p = gl.reshape(M_PAD, G, 3 * GRP).transpose(1, 0, 2)
        att_p = _nsa_prompt(q, gl_p, kvc_p, rows, ov_p)
        q_s = q[S0:S0 + DB].reshape(DB, G, GRP, D)
        gl_s = gl[S0:S0 + DB].reshape(DB, G, GRP, 3).transpose(0, 1, 3, 2)[..., None]
        o_c, o_w, idx = _nsa_sample_cmp_win(slopes, q_s, kvc_s, ov_s, win2d, rows_s)
        idx_flat = idx[:, :, 0, :KK_S].reshape(-1)
        att_s = _nsa_sample_sel(pt_flat, idx_flat, slopes, q_s, cache2d, rows_s, o_c, o_w, gl_s)
        att = _fill_tail_rows(att_p, att_s.reshape(DB, hd))
        x = _matmul_fullk(att, w_out_b, l, D_MODEL, F32, residual=x)
        x = _ffn_block(x, g_ffn[N_A + l], w_gate_up, w_down, N_A + l)

    y_p = _rmsnorm(x, g_final, F32, n_rows=M_PROMPT)
    y_s = _rmsnorm(x[S0:], g_final, F32)[:DB]
    wp = min(WINDOW, SEQ)
    rows_p = rows[:M_PROMPT].reshape(BATCH, SEQ, 6, G, D)
    rows_s6 = rows[S0:S0 + DB].reshape(DB, 1, 6, G, D)
    wbuf = state_win_kv.shape[1]
    win_kv_sample = jnp.concatenate([state_win_kv, rows_s6[:, :, 4:]], axis=1)[:, -wbuf:]
    return (y_p.reshape(BATCH, SEQ, D_MODEL), y_s.reshape(DB, 1, D_MODEL),
            rows_p[:, :, :4], rows_p[:, SEQ - wp:, 4:],
            jnp.stack(c_p), jnp.stack(n_p), jnp.stack(m_p),
            rows_s6[:, :, :4], win_kv_sample,
            jnp.stack(c_s), jnp.stack(n_s), jnp.stack(m_s))
```

```python
import functools

import numpy as np
import jax
import jax.numpy as jnp
from jax import lax
from jax.experimental import pallas as pl
from jax.experimental.pallas import tpu as pltpu

D_MODEL = 4096
BATCH = 4
SEQ = 2048
DEPTH = 4
DEC_BATCH = 8
DEC_SEQ = 1
PAST_LEN = 8192
PAGE_SIZE = 128

N_A = DEPTH // 2
N_B = DEPTH - N_A
MLSTM_HEADS = 8
MLSTM_QK_DIM = D_MODEL // 16
MLSTM_V_DIM = D_MODEL // MLSTM_HEADS
GATE_CAP = 15.0
HEAD_DIM = 128
NSA_HEADS = D_MODEL // HEAD_DIM
NSA_KV_HEADS = 4
GRP = NSA_HEADS // NSA_KV_HEADS
CMP_LEN = 32
CMP_STRIDE = 16
CMP_HID = HEAD_DIM
SEL_LEN = 64
SEL_TOP = 16
WINDOW = 512
D_FF = -(-8 * D_MODEL // (3 * 256)) * 256
EPS = 1e-6
NEG = -1e30

F32 = jnp.float32
BF16 = jnp.bfloat16

LANES = 128
SUBLANES = 8
VMEM_BYTES_V7X = 64 * 1024 * 1024


def _round_up(n, m):
    return -(-n // m) * m


def _pick_tile(n, cap, align=LANES):
    best = None
    for t in range(align, min(n, cap) + 1, align):
        if n % t == 0:
            best = t
    assert best is not None, (n, cap, align)
    return best


M_PROMPT = BATCH * SEQ
M_SAMPLE = DEC_BATCH * DEC_SEQ
S0 = M_PROMPT
M_PAD = _round_up(M_PROMPT + M_SAMPLE, LANES)
TM = _pick_tile(M_PAD, 2048)
TR = _pick_tile(M_PAD, 384, align=64)
TN_FULLK = 256
MLSTM_L = 256
TQ = 128
TKS = 1024
N_GATE_PAD = LANES
D_FF_PAD = _round_up(D_FF, 1024)
CMP_R = CMP_LEN // CMP_STRIDE
SAMPLE_PAGES_PER_STEP = 8
PERM_ROWS = LANES

assert DEC_SEQ == 1 and S0 % 16 == 0
assert SEQ % PERM_ROWS == 0 and PAGE_SIZE % PERM_ROWS == 0
assert (PERM_ROWS // CMP_STRIDE) & (PERM_ROWS // CMP_STRIDE - 1) == 0
assert SEQ % MLSTM_L == 0 and SEQ % TQ == 0 and SEQ % TKS == 0 and TKS % TQ == 0
assert PAST_LEN % PAGE_SIZE == 0 and PAGE_SIZE % SEL_LEN == 0 and PAGE_SIZE % CMP_STRIDE == 0
assert CMP_R == 2 and SEQ // SEL_LEN <= LANES


def _vmem_limit(block_bytes, single_bytes):
    return int(min(single_bytes + 2 * block_bytes + (12 << 20), VMEM_BYTES_V7X - (8 << 20)))


def _params(sem, block_bytes, single_bytes=0):
    return pltpu.CompilerParams(dimension_semantics=sem,
                                vmem_limit_bytes=_vmem_limit(block_bytes, single_bytes))


def _dot(a, b):
    return jnp.dot(a, b, preferred_element_type=F32)


def _dot_nt(a, b):
    return lax.dot_general(a, b, (((1,), (1,)), ((), ())), preferred_element_type=F32)


def _dot_tn(a, b):
    return lax.dot_general(a, b, (((0,), (0,)), ((), ())), preferred_element_type=F32)


def _dot_exact(a, b):
    return jnp.dot(a, b, precision=lax.Precision.HIGHEST, preferred_element_type=F32)


def _dot_split3(a, b):
    a_hi = a.astype(BF16)
    a_lo = (a - a_hi.astype(F32)).astype(BF16)
    b_hi = b.astype(BF16)
    b_lo = (b - b_hi.astype(F32)).astype(BF16)
    return _dot(a_hi, b_hi) + (_dot(a_hi, b_lo) + _dot(a_lo, b_hi))


def _log_sigmoid(x):
    return jnp.minimum(x, 0.0) - jnp.log1p(jnp.exp(-jnp.abs(x)))


def _col_form(row, n):
    return jnp.broadcast_to(row, (n, n)).T


def _rms_kernel(x_ref, g_ref, o_ref):
    x = x_ref[...]
    y = x * lax.rsqrt(jnp.mean(x * x, axis=-1, keepdims=True) + EPS) * g_ref[...]
    o_ref[...] = y.astype(o_ref.dtype)


def _rmsnorm(x, g, out_dtype, n_rows=None):
    d = x.shape[1]
    m = x.shape[0] if n_rows is None else n_rows
    tr = _pick_tile(m, TR, align=SUBLANES * 8)
    blk = tr * d * (8 + jnp.dtype(out_dtype).itemsize)
    return pl.pallas_call(
        _rms_kernel,
        grid=(m // tr,),
        in_specs=[pl.BlockSpec((tr, d), lambda i: (i, 0)), pl.BlockSpec((1, d), lambda i: (0, 0))],
        out_specs=pl.BlockSpec((tr, d), lambda i: (i, 0)),
        out_shape=jax.ShapeDtypeStruct((m, d), out_dtype),
        compiler_params=_params(("parallel",), blk),
        name="rmsnorm",
    )(x, g.reshape(1, d))


def _rms_gate_kernel(x_ref, g_ref, w_ref, b_ref, h_ref, o_ref, *, mlstm_gates):
    x = x_ref[...]
    y = x * lax.rsqrt(jnp.mean(x * x, axis=-1, keepdims=True) + EPS) * g_ref[...]
    h_ref[...] = y.astype(h_ref.dtype)
    z = _dot_split3(y, w_ref[...]) + b_ref[...]
    if mlstm_gates:
        gl = GATE_CAP * jnp.tanh(z / GATE_CAP)
        lane = lax.broadcasted_iota(jnp.int32, z.shape, 1)
        z = jnp.where(lane < MLSTM_HEADS, gl, _log_sigmoid(gl))
    o_ref[...] = z


def _rmsnorm_with_gates(x, g, w_gate, b_gate, mlstm_gates):
    m, d = x.shape
    ng = w_gate.shape[1]
    wp = jnp.pad(w_gate, ((0, 0), (0, N_GATE_PAD - ng)))
    bp = jnp.pad(b_gate, (0, N_GATE_PAD - ng)).reshape(1, N_GATE_PAD)
    blk = TR * d * 10 + d * N_GATE_PAD * 6 + TR * N_GATE_PAD * 4
    return pl.pallas_call(
        functools.partial(_rms_gate_kernel, mlstm_gates=mlstm_gates),
        grid=(m // TR,),
        in_specs=[pl.BlockSpec((TR, d), lambda i: (i, 0)),
                  pl.BlockSpec((1, d), lambda i: (0, 0)),
                  pl.BlockSpec((d, N_GATE_PAD), lambda i: (0, 0)),
                  pl.BlockSpec((1, N_GATE_PAD), lambda i: (0, 0))],
        out_specs=[pl.BlockSpec((TR, d), lambda i: (i, 0)),
                   pl.BlockSpec((TR, N_GATE_PAD), lambda i: (i, 0))],
        out_shape=[jax.ShapeDtypeStruct((m, d), BF16), jax.ShapeDtypeStruct((m, N_GATE_PAD), F32)],
        compiler_params=_params(("parallel",), blk),
        name="rmsnorm_gates",
    )(x, g.reshape(1, d), wp, bp)


def _mm_fullk_kernel(*refs, has_res, rows_valid):
    x_ref, w_ref, o_ref = refs[0], refs[1], refs[-1]
    w = w_ref[...]
    if rows_valid < w.shape[0]:
        w = jnp.where(lax.broadcasted_iota(jnp.int32, w.shape, 0) < rows_valid, w, 0.0)
    r = _dot(x_ref[...], w.astype(BF16))
    if has_res:
        r = r + refs[2][...]
    o_ref[...] = r.astype(o_ref.dtype)


def _row_tile_spec(kb, kh=0):
    return pl.BlockSpec((TM, kb), lambda i, j: (i, kh), pipeline_mode=pl.Buffered(1))


def _matmul_fullk(x, w, layer, n_cols, out_dtype, residual=None, k_split=1):
    m, kd = x.shape
    tn = TN_FULLK
    kb = kd // k_split
    k_rows = w.shape[1]
    assert n_cols % tn == 0 and kd % k_split == 0 and kb % LANES == 0 and w.dtype == F32
    assert (k_split - 1) * kb < k_rows <= kd
    out = residual
    for kh in range(k_split):
        last = kh == k_split - 1
        dt = out_dtype if last else F32
        osz = jnp.dtype(dt).itemsize
        in_specs = [_row_tile_spec(kb, kh), pl.BlockSpec((None, kb, tn), lambda i, j, kh=kh: (layer, kh, j))]
        args = [x, w]
        streamed = kb * tn * 4 + TM * tn * osz
        if out is not None:
            in_specs.append(pl.BlockSpec((TM, tn), lambda i, j: (i, j)))
            args.append(out)
            streamed += TM * tn * 4
        out = pl.pallas_call(
            functools.partial(_mm_fullk_kernel, has_res=out is not None, rows_valid=min(kb, k_rows - kh * kb)),
            grid=(m // TM, n_cols // tn),
            in_specs=in_specs,
            out_specs=pl.BlockSpec((TM, tn), lambda i, j: (i, j)),
            out_shape=jax.ShapeDtypeStruct((m, n_cols), dt),
            compiler_params=_params(("parallel", "arbitrary"), streamed, single_bytes=TM * kb * 2),
            name="matmul_fullk",
        )(*args)
    return out


def _mm_fullk_nt_kernel(x_ref, wt_ref, o_ref):
    o_ref[...] = _dot_nt(x_ref[...], wt_ref[...].astype(BF16)).astype(o_ref.dtype)


def _matmul_fullk_nt(x, wt, layer, n_cols, out_dtype):
    m, kd = x.shape
    tn = 2 * TN_FULLK
    assert n_cols % tn == 0 and wt.shape[2] == kd and wt.dtype == F32
    streamed = kd * tn * 4 + TM * tn * jnp.dtype(out_dtype).itemsize
    return pl.pallas_call(
        _mm_fullk_nt_kernel,
        grid=(m // TM, n_cols // tn),
        in_specs=[_row_tile_spec(kd), pl.BlockSpec((None, tn, kd), lambda i, j: (layer, j, 0))],
        out_specs=pl.BlockSpec((TM, tn), lambda i, j: (i, j)),
        out_shape=jax.ShapeDtypeStruct((m, n_cols), out_dtype),
        compiler_params=_params(("parallel", "arbitrary"), streamed, single_bytes=TM * kd * 2),
        name="matmul_fullk_nt",
    )(x, wt)


def _ffn_up_kernel(x_ref, wg_ref, wu_ref, o_ref, *, n_real):
    j = pl.program_id(1)

    @pl.when(j < n_real)
    def _():
        x = x_ref[...]
        gate = _dot(x, wg_ref[...].astype(BF16))
        up = _dot(x, wu_ref[...].astype(BF16))
        o_ref[...] = (jax.nn.silu(gate) * up).astype(o_ref.dtype)

    @pl.when(j >= n_real)
    def _():
        o_ref[...] = jnp.zeros_like(o_ref)


def _ffn_up(x, w_gate_up, layer):
    m, kd = x.shape
    tn = TN_FULLK
    assert D_FF % tn == 0 and D_FF_PAD % tn == 0
    n_real = D_FF // tn
    col = lambda j: jnp.minimum(j, n_real - 1)
    streamed = 2 * kd * tn * 4 + TM * tn * 2
    return pl.pallas_call(
        functools.partial(_ffn_up_kernel, n_real=n_real),
        grid=(m // TM, D_FF_PAD // tn),
        in_specs=[_row_tile_spec(kd),
                  pl.BlockSpec((None, kd, tn), lambda i, j: (layer, 0, col(j))),
                  pl.BlockSpec((None, kd, tn), lambda i, j: (layer, 0, n_real + col(j)))],
        out_specs=pl.BlockSpec((TM, tn), lambda i, j: (i, j)),
        out_shape=jax.ShapeDtypeStruct((m, D_FF_PAD), BF16),
        compiler_params=_params(("parallel", "arbitrary"), streamed, single_bytes=TM * kd * 2),
        name="ffn_up",
    )(x, w_gate_up, w_gate_up)


def _mlstm_prompt_kernel(zero_ref, q_ref, k_ref, v_ref, o_ref, g_ref, gh_ref, hs_ref, c_ref, n_ref, m_ref):
    del zero_ref
    L, DK, DV = MLSTM_L, MLSTM_QK_DIM, MLSTM_V_DIM

    @pl.when(pl.program_id(1) == 0)
    def _():
        c_ref[...] = jnp.zeros_like(c_ref)
        n_ref[...] = jnp.zeros_like(n_ref)
        m_ref[...] = jnp.zeros_like(m_ref)

    t_idx = lax.broadcasted_iota(jnp.int32, (L, L), 0)
    s_idx = lax.broadcasted_iota(jnp.int32, (L, L), 1)
    for hd in range(MLSTM_HEADS):
        ig_row = g_ref[0, hd]
        lf_row = g_ref[1, hd]
        m_prev = m_ref[0, hd]
        lf_c = _col_form(lf_row, L)
        b_row = jnp.sum(jnp.where(t_idx <= s_idx, lf_c, 0.0), axis=0, keepdims=True)
        b_r = jnp.broadcast_to(b_row, (L, L))
        b_c = b_r.T
        ig_r = jnp.broadcast_to(ig_row, (L, L))
        log_d = jnp.where(s_idx <= t_idx, b_c - b_r + ig_r, -jnp.inf)
        b_col = b_c[:, 0:1]
        inter = b_col + m_prev
        m_t = jnp.maximum(jnp.max(log_d, axis=1, keepdims=True), inter)
        d = jnp.exp(log_d - m_t)
        w_inter = jnp.exp(inter - m_t)

        q = q_ref[:, hd * DK:(hd + 1) * DK] * (DK ** -0.5)
        k = k_ref[:, hd * DK:(hd + 1) * DK]
        qb = q.astype(BF16)
        kb = k.astype(BF16)
        vb = v_ref[:, hd * DV:(hd + 1) * DV].astype(BF16)
        s = _dot_nt(qb, kb) * d
        c_prev = c_ref[0, hd]
        n_prev = n_ref[0, hd]
        num = w_inter * _dot(qb, c_prev.astype(BF16)) + _dot(s.astype(BF16), vb)
        den = w_inter * jnp.sum(q * n_prev, axis=1, keepdims=True) + jnp.sum(s, axis=1, keepdims=True)
        h = num / jnp.maximum(jnp.abs(den), jnp.exp(-m_t))
        hn = h * lax.rsqrt(jnp.mean(h * h, axis=1, keepdims=True) + EPS)
        cols = slice(hd * DV, (hd + 1) * DV)
        hs_ref[:, cols] = (hn * gh_ref[:, cols] * jax.nn.sigmoid(o_ref[:, cols])).astype(hs_ref.dtype)

        m_new = m_t[L - 1:L, :]
        b_last = b_row[:, L - 1:L]
        ig_col = ig_r.T[:, 0:1]
        decay = jnp.exp(b_last - b_col + ig_col - m_new)
        carry = jnp.exp(b_last + m_prev - m_new)
        kd = k * decay
        c_ref[0, hd] = carry * c_prev + _dot_tn(kd.astype(BF16), vb)
        n_ref[0, hd] = carry * n_prev + jnp.sum(kd, axis=0, keepdims=True)
        m_ref[0, hd] = m_new


def _mlstm_prompt(z, gates_t, g_head):
    H, DK, DV, L = MLSTM_HEADS, MLSTM_QK_DIM, MLSTM_V_DIM, MLSTM_L
    nc = SEQ // L
    assert (H * DV) % (H * DK) == 0
    v0 = (2 * H * DK) // (H * DV)
    row = lambda b, c: b * nc + c
    blk = L * (2 * H * DK + 2 * H * DV) * 4 + L * H * DV * 2 + H * DK * DV * 4
    return pl.pallas_call(
        _mlstm_prompt_kernel,
        grid=(BATCH, nc),
        in_specs=[pl.BlockSpec(memory_space=pl.ANY),
                  pl.BlockSpec((L, H * DK), lambda b, c: (row(b, c), 0)),
                  pl.BlockSpec((L, H * DK), lambda b, c: (row(b, c), 1)),
                  pl.BlockSpec((L, H * DV), lambda b, c: (row(b, c), v0)),
                  pl.BlockSpec((L, H * DV), lambda b, c: (row(b, c), v0 + 1)),
                  pl.BlockSpec((2, H, 1, L), lambda b, c: (0, 0, 0, row(b, c))),
                  pl.BlockSpec((1, H * DV), lambda b, c: (0, 0))],
        out_specs=[pl.BlockSpec((L, H * DV), lambda b, c: (row(b, c), 0)),
                   pl.BlockSpec((1, H, DK, DV), lambda b, c: (b, 0, 0, 0)),
                   pl.BlockSpec((1, H, 1, DK), lambda b, c: (b, 0, 0, 0)),
                   pl.BlockSpec((1, H, 1, 1), lambda b, c: (b, 0, 0, 0))],
        out_shape=[jax.ShapeDtypeStruct((M_PAD, H * DV), BF16),
                   jax.ShapeDtypeStruct((BATCH, H, DK, DV), F32),
                   jax.ShapeDtypeStruct((BATCH, H, 1, DK), F32),
                   jax.ShapeDtypeStruct((BATCH, H, 1, 1), F32)],
        compiler_params=_params(("parallel", "arbitrary"), blk),
        input_output_aliases={0: 0},
        name="mlstm_prompt",
    )(jnp.zeros((M_PAD, H * DV), BF16), z, z, z, z, gates_t, g_head.reshape(1, H * DV))


def _mlstm_sample_kernel(q_ref, k_ref, v_ref, o_ref, ig_ref, lf_ref, m0_ref, c0_ref, n0_ref, gh_ref,
                         hs_ref, c_ref, n_ref, m_ref):
    DK = MLSTM_QK_DIM
    b = pl.program_id(0)
    q = q_ref[pl.ds(b, 1), :] * (DK ** -0.5)
    k = k_ref[pl.ds(b, 1), :]
    v = v_ref[pl.ds(b, 1), :]
    og = o_ref[pl.ds(b, 1), :]
    ig = ig_ref[0, 0]
    lf = lf_ref[0, 0]
    m0 = m0_ref[0, 0]
    m_t = jnp.maximum(ig, lf + m0)
    d = jnp.exp(ig - m_t)[:, 0:1]
    w = jnp.exp(lf + m0 - m_t)[:, 0:1]
    c0 = c0_ref[0, 0]
    n0 = n0_ref[0, 0]
    q_col = _col_form(q, DK)[:, 0:1]
    k_col = _col_form(k, DK)[:, 0:1]
    s = jnp.sum(q * k, axis=1, keepdims=True) * d
    num = w * jnp.sum(q_col * c0, axis=0, keepdims=True) + s * v
    den = w * jnp.sum(q * n0, axis=1, keepdims=True) + s
    h = num / jnp.maximum(jnp.abs(den), jnp.exp(-m_t)[:, 0:1])
    hn = h * lax.rsqrt(jnp.mean(h * h, axis=1, keepdims=True) + EPS)
    hs_ref[0] = hn * gh_ref[...] * jax.nn.sigmoid(og)
    c_ref[0, 0] = w * c0 + d * (k_col * v)
    n_ref[0, 0] = w * n0 + d * k
    m_ref[0, 0] = m_t


def _mlstm_sample(z, ig_b, lf_b, m_b, c0, n0, g_head):
    H, DK, DV = MLSTM_HEADS, MLSTM_QK_DIM, MLSTM_V_DIM
    DB = DEC_BATCH
    r0 = S0 // SUBLANES
    v0 = (2 * H * DK) // DV
    sc = pl.BlockSpec((1, 1, 1, LANES), lambda b, h: (b, h, 0, 0))
    blk = 2 * DK * DV * 4 + SUBLANES * (2 * DK + 2 * DV) * 4
    return pl.pallas_call(
        _mlstm_sample_kernel,
        grid=(DB, H),
        in_specs=[pl.BlockSpec((SUBLANES, DK), lambda b, h: (r0, h)),
                  pl.BlockSpec((SUBLANES, DK), lambda b, h: (r0, H + h)),
                  pl.BlockSpec((SUBLANES, DV), lambda b, h: (r0, v0 + h)),
                  pl.BlockSpec((SUBLANES, DV), lambda b, h: (r0, v0 + H + h)),
                  sc, sc, sc,
                  pl.BlockSpec((1, 1, DK, DV), lambda b, h: (b, h, 0, 0)),
                  pl.BlockSpec((1, 1, 1, DK), lambda b, h: (b, h, 0, 0)),
                  pl.BlockSpec((1, DV), lambda b, h: (0, h))],
        out_specs=[pl.BlockSpec((1, 1, DV), lambda b, h: (b, 0, h)),
                   pl.BlockSpec((1, 1, DK, DV), lambda b, h: (b, h, 0, 0)),
                   pl.BlockSpec((1, 1, 1, DK), lambda b, h: (b, h, 0, 0)),
                   sc],
        out_shape=[jax.ShapeDtypeStruct((DB, 1, H * DV), F32),
                   jax.ShapeDtypeStruct((DB, H, DK, DV), F32),
                   jax.ShapeDtypeStruct((DB, H, 1, DK), F32),
                   jax.ShapeDtypeStruct((DB, H, 1, LANES), F32)],
        compiler_params=_params(("parallel", "parallel"), blk),
        name="mlstm_sample",
    )(z, z, z, z, ig_b, lf_b, m_b, c0, n0, g_head.reshape(1, H * DV))


def _compress_proj_kernel(*refs, n_in, groups_per_in, lead):
    x_refs = refs[-(n_in + 3):-3]
    w_ref, p_ref, xp_scr = refs[-3], refs[-2], refs[-1]
    G, D = NSA_KV_HEADS, HEAD_DIM
    cpg = PERM_ROWS // CMP_STRIDE
    nch = n_in * groups_per_in * cpg
    i_idx = lax.broadcasted_iota(jnp.int32, (PERM_ROWS, PERM_ROWS), 0)
    j_idx = lax.broadcasted_iota(jnp.int32, (PERM_ROWS, PERM_ROWS), 1)
    src_row = (i_idx & (cpg - 1)) * CMP_STRIDE + (i_idx >> (cpg.bit_length() - 1))
    perm = (j_idx == src_row).astype(BF16)
    for i, xr in enumerate(x_refs):
        for p in range(groups_per_in):
            rows = pl.ds(p * PERM_ROWS, PERM_ROWS)
            xg = xr[0, rows, :] if lead else xr[rows, :]
            xg = _dot(perm, xg.astype(BF16))
            ch0 = (i * groups_per_in + p) * cpg
            for c in range(CMP_STRIDE):
                xp_scr[c, ch0:ch0 + cpg, :] = xg[c * cpg:(c + 1) * cpg, :]
    for s in range(2):
        acc = jnp.zeros((G * nch, CMP_R * CMP_HID), F32)
        for c in range(CMP_STRIDE):
            xs = jnp.concatenate([xp_scr[c, :, (s * G + g) * D:(s * G + g + 1) * D] for g in range(G)], axis=0)
            acc = acc + _dot(xs.astype(BF16), w_ref[s, c])
        for g in range(G):
            p_ref[0, s, g] = acc[g * nch:(g + 1) * nch]


def _compress_weights(w_cmp1):
    w = w_cmp1.reshape(2, CMP_R, CMP_STRIDE, HEAD_DIM, CMP_HID)
    return w.transpose(0, 2, 3, 1, 4).reshape(2, CMP_STRIDE, HEAD_DIM, CMP_R * CMP_HID).astype(BF16)


def _compress_proj_prompt(rows, w1r):
    G, D = NSA_KV_HEADS, HEAD_DIM
    nch = SEQ // CMP_STRIDE
    pw = CMP_R * CMP_HID
    blk = SEQ * 2 * G * D * 4 + w1r.size * 2 + 2 * G * nch * pw * 4 + SEQ * 2 * G * D * 4
    return pl.pallas_call(
        functools.partial(_compress_proj_kernel, n_in=1, groups_per_in=SEQ // PERM_ROWS, lead=False),
        grid=(BATCH,),
        in_specs=[pl.BlockSpec((SEQ, 2 * G * D), lambda b: (b, 0)),
                  pl.BlockSpec(w1r.shape, lambda b: (0, 0, 0, 0))],
        out_specs=pl.BlockSpec((1, 2, G, nch, pw), lambda b: (b, 0, 0, 0, 0)),
        out_shape=jax.ShapeDtypeStruct((BATCH, 2, G, nch, pw), F32),
        scratch_shapes=[pltpu.VMEM((CMP_STRIDE, nch, 2 * G * D), F32)],
        compiler_params=_params(("parallel",), blk),
        name="compress_proj_prompt",
    )(rows, w1r)


def _compress_proj_sample(cache2d, page_table_flat, w1r):
    G, D = NSA_KV_HEADS, HEAD_DIM
    n_pages = PAST_LEN // PAGE_SIZE
    pps = SAMPLE_PAGES_PER_STEP
    assert n_pages % pps == 0
    cpi = PAGE_SIZE // CMP_STRIDE
    nch_step = pps * cpi
    pw = CMP_R * CMP_HID

    def page_map(b, j, pt, *, i):
        return (pt[b * n_pages + j * pps + i], 0, 0)

    in_specs = [pl.BlockSpec((1, PAGE_SIZE, 2 * G * D), functools.partial(page_map, i=i)) for i in range(pps)]
    in_specs.append(pl.BlockSpec(w1r.shape, lambda b, j, pt: (0, 0, 0, 0)))
    blk = 2 * pps * PAGE_SIZE * 2 * G * D * 4 + w1r.size * 2 + 2 * G * nch_step * pw * 4
    return pl.pallas_call(
        functools.partial(_compress_proj_kernel, n_in=pps, groups_per_in=PAGE_SIZE // PERM_ROWS, lead=True),
        grid_spec=pltpu.PrefetchScalarGridSpec(
            num_scalar_prefetch=1,
            grid=(DEC_BATCH, n_pages // pps),
            in_specs=in_specs,
            out_specs=pl.BlockSpec((1, 2, G, nch_step, pw), lambda b, j, pt: (b, 0, 0, j, 0)),
            scratch_shapes=[pltpu.VMEM((CMP_STRIDE, nch_step, 2 * G * D), F32)],
        ),
        out_shape=jax.ShapeDtypeStruct((DEC_BATCH, 2, G, n_pages * cpi, pw), F32),
        compiler_params=_params(("parallel", "arbitrary"), blk),
        name="compress_proj_sample",
    )(page_table_flat, *([cache2d] * pps), w1r)


def _compress_finish_kernel(p_ref, pe_ref, w1_ref, w2_ref, o_ref):
    nch = p_ref.shape[3]
    pe = jnp.broadcast_to(pe_ref[0], (SUBLANES, pe_ref.shape[2])).astype(BF16)
    pe_term = _dot(pe, w1_ref[0].astype(BF16))[0:1]
    w2 = w2_ref[0].astype(BF16)
    for g in range(NSA_KV_HEADS):
        p = p_ref[0, 0, g]
        nxt = pltpu.roll(p[:, CMP_HID:], shift=nch - 1, axis=0)
        hidden = p[:, :CMP_HID] + nxt + pe_term
        o_ref[0, 0, g] = _dot(jax.nn.gelu(hidden).astype(BF16), w2)


def _compress_finish(p, pe_cmp, w_cmp1, w_cmp2):
    nb, _, G, nch, pw = p.shape
    pe = pe_cmp.reshape(2, 1, CMP_LEN * HEAD_DIM)
    blk = G * nch * (pw + HEAD_DIM) * 4 + CMP_LEN * HEAD_DIM * (CMP_HID + 1) * 4
    return pl.pallas_call(
        _compress_finish_kernel,
        grid=(nb, 2),
        in_specs=[pl.BlockSpec((1, 1, G, nch, pw), lambda b, s: (b, s, 0, 0, 0)),
                  pl.BlockSpec((1, 1, CMP_LEN * HEAD_DIM), lambda b, s: (s, 0, 0)),
                  pl.BlockSpec((1, CMP_LEN * HEAD_DIM, CMP_HID), lambda b, s: (s, 0, 0)),
                  pl.BlockSpec((1, CMP_HID, HEAD_DIM), lambda b, s: (s, 0, 0))],
        out_specs=pl.BlockSpec((1, 1, G, nch, HEAD_DIM), lambda b, s: (b, s, 0, 0, 0)),
        out_shape=jax.ShapeDtypeStruct((nb, 2, G, nch, HEAD_DIM), F32),
        compiler_params=_params(("parallel", "parallel"), blk),
        name="compress_finish",
    )(p, pe, w_cmp1, w_cmp2)


def _selection_overlap(n_cmp, n_sel, rows, cols):
    start = np.arange(n_cmp)[:, None] * CMP_STRIDE
    end = start + CMP_LEN - 1
    j = np.arange(n_sel)[None, :]
    ov = ((start <= (j + 1) * SEL_LEN - 1) & (end >= j * SEL_LEN)).astype(np.float32)
    out = np.zeros((rows, cols), np.float32)
    out[:n_cmp, :n_sel] = ov
    return jnp.asarray(out)


def _masked_softmax(s, valid):
    sm = jnp.where(valid, s, NEG)
    e = jnp.exp(sm - jnp.max(sm, axis=-1, keepdims=True))
    return jnp.where(valid, e / jnp.sum(e, axis=-1, keepdims=True), 0.0)


def _flash_step(carry, s, valid, vb):
    m, l, acc = carry
    sm = jnp.where(valid, s, NEG)
    m_new = jnp.maximum(m, jnp.max(sm, axis=-1, keepdims=True))
    alpha = jnp.exp(m - m_new)
    e = jnp.exp(sm - m_new)
    l = alpha * l + jnp.sum(e, axis=-1, keepdims=True)
    acc = alpha * acc + _dot(jnp.where(valid, e, 0.0).astype(BF16), vb)
    return m_new, l, acc


def _flash_one(carry, s_new, v_new):
    m, l, acc = carry
    m_new = jnp.maximum(m, s_new)
    alpha = jnp.exp(m - m_new)
    e = jnp.exp(s_new - m_new)
    return m_new, alpha * l + e, alpha * acc + e * v_new


def _flash_init(rows):
    return (jnp.full((rows, 1), -jnp.inf, F32), jnp.zeros((rows, 1), F32), jnp.zeros((rows, HEAD_DIM), F32))


def _top_rank(impm_c, impm_r, jp_idx, j_idx, axis):
    before = (impm_c > impm_r) | ((impm_c == impm_r) & (jp_idx < j_idx))
    return jnp.sum(before.astype(jnp.int32), axis=axis, keepdims=True)


FEAT_SIG = 96
FEAT_SIG_FULL = FEAT_SIG + 9
MASK_BIG = 2.0 ** 100
POS_RADIX = 256
LOG2E = 1.4426950408889634
assert SEQ // SEL_LEN <= FEAT_SIG and SEQ >= WINDOW + TQ


def _key_features(pos, block):
    n = pos.shape[0]
    kf = np.zeros((n, LANES), np.float32)
    if block is not None:
        kf[np.arange(n), block] = -MASK_BIG
    kf[:, FEAT_SIG:FEAT_SIG + 3] = (pos % POS_RADIX)[:, None]
    kf[:, FEAT_SIG + 3:FEAT_SIG + 6] = (pos // POS_RADIX)[:, None]
    kf[:, FEAT_SIG + 6:FEAT_SIG + 9] = 1.0
    return jnp.asarray(kf, BF16)


def _sigma_table():
    slopes = np.exp2(-8.0 * np.arange(1, NSA_HEADS + 1, dtype=np.float32) / NSA_HEADS).astype(np.float32)
    sigma = (slopes / np.float32(HEAD_DIM ** -0.5)).astype(np.float32)
    bf = lambda a: np.asarray(np.asarray(a, dtype=BF16), dtype=np.float32)
    s1 = bf(sigma)
    s2 = bf(sigma - s1)
    s3 = bf(sigma - s1 - s2)
    tab = np.zeros((NSA_HEADS, LANES), np.float32)
    for i, s in enumerate((s1, s2, s3)):
        tab[:, FEAT_SIG + i] = s
        tab[:, FEAT_SIG + 3 + i] = s * POS_RADIX
    tab[:, FEAT_SIG_FULL] = sigma
    return jnp.asarray(tab)


def _nsa_prompt_kernel(zero_ref, q_ref, gl_ref, sig_ref, kc_ref, vc_ref, ks_ref, vs_ref, kw_ref, vw_ref,
                       ov_ref, kfc_ref, kfs_ref, kfw_ref, out_ref,
                       qa_scr, kca_scr, vcb_scr, ksa_scr, vsb_scr, kwa_scr, vwb_scr, m_scr, acc_scr):
    del zero_ref
    qi = pl.program_id(2)
    D = HEAD_DIM
    c2 = (D ** -0.5) * LOG2E
    n_sel = SEQ // SEL_LEN
    qs = qi * TQ
    tpos = qs + lax.broadcasted_iota(jnp.int32, (TQ, 1), 0)
    tposf = tpos.astype(F32)
    lane = lax.broadcasted_iota(jnp.int32, (1, LANES), 1)
    heads = [slice(r * TQ, (r + 1) * TQ) for r in range(GRP)]

    @pl.when(qi == 0)
    def _():
        kca_scr[:, 0:D] = kc_ref[0, 0, 0].astype(BF16)
        kca_scr[:, D:2 * D] = kfc_ref[...]
        vcb_scr[...] = vc_ref[0, 0, 0].astype(BF16)
        ksa_scr[:, 0:D] = ks_ref[...].astype(BF16)
        ksa_scr[:, D:2 * D] = kfs_ref[...]
        kwa_scr[:, 0:D] = kw_ref[...].astype(BF16)
        kwa_scr[:, D:2 * D] = kfw_ref[...]
        ones_col = jnp.where(lax.broadcasted_iota(jnp.int32, (SEQ, D), 1) == 0, 1.0, 0.0).astype(BF16)
        vsb_scr[:, 0:D] = vs_ref[...].astype(BF16)
        vsb_scr[:, D:2 * D] = ones_col
        vwb_scr[:, 0:D] = vw_ref[...].astype(BF16)
        vwb_scr[:, D:2 * D] = ones_col

    def write_features(nsel):
        for r in range(GRP):
            sig_row = sig_ref[r:r + 1, :]
            c = -sig_row[:, FEAT_SIG_FULL:FEAT_SIG_FULL + 1] * tposf
            c_hi = c.astype(BF16).astype(F32)
            c_mid = (c - c_hi).astype(BF16).astype(F32)
            c_lo = c - c_hi - c_mid
            f = jnp.where(lane == FEAT_SIG + 6, c_hi,
                          jnp.where(lane == FEAT_SIG + 7, c_mid, jnp.where(lane == FEAT_SIG + 8, c_lo, sig_row)))
            if nsel is not None:
                f = jnp.where(lane < n_sel, nsel, f)
            qa_scr[heads[r], D:2 * D] = f.astype(BF16)

    for r in range(GRP):
        qa_scr[heads[r], 0:D] = q_ref[:, r * D:(r + 1) * D]
    write_features(None)

    ncp = kca_scr.shape[0]
    cpos = lax.broadcasted_iota(jnp.int32, (1, ncp), 1) * CMP_STRIDE + (CMP_LEN - 1)
    valid_c = cpos <= tpos
    s_c = _dot_nt(qa_scr[...], kca_scr[...])
    p_sum = jnp.zeros((TQ, ncp), F32)
    p_c = []
    for r in range(GRP):
        sm = jnp.where(valid_c, s_c[heads[r]], NEG)
        e = jnp.exp2((sm - jnp.max(sm, axis=-1, keepdims=True)) * c2)
        p = jnp.where(valid_c, e / jnp.sum(e, axis=-1, keepdims=True), 0.0)
        p_c.append(p.astype(BF16))
        p_sum = p_sum + p
    o_c = _dot(jnp.concatenate(p_c, axis=0), vcb_scr[...])
    imp = _dot_exact(p_sum, ov_ref[...])

    cur = tpos // SEL_LEN
    forced = (lane == 0) | (lane == cur) | (lane == cur - 1)
    impm = jnp.where(forced, jnp.inf, jnp.where(lane <= cur, imp, -jnp.inf))
    imp_t = impm.T[0:n_sel]
    j_sub = lax.broadcasted_iota(jnp.int32, (n_sel, TQ), 0)
    rank_t = jnp.zeros((n_sel, TQ), jnp.int32)
    for jp in range(n_sel):
        cj = imp_t[jp:jp + 1, :]
        rank_t = rank_t + ((cj > imp_t) | ((cj == imp_t) & (jp < j_sub))).astype(jnp.int32)
    unsel_t = jnp.where(rank_t < min(SEL_TOP, n_sel), 0.0, 1.0)
    unsel = jnp.concatenate([unsel_t, jnp.ones((LANES - n_sel, TQ), F32)], axis=0).T
    write_features(jnp.where(lane <= cur, unsel, 1.0))

    m_scr[...] = jnp.full(m_scr.shape, -jnp.inf, F32)
    acc_scr[...] = jnp.zeros(acc_scr.shape, F32)

    def sel_tile(kt, bias):
        k0 = pl.multiple_of(kt * TKS, TKS)
        s_all = _dot_nt(qa_scr[...], ksa_scr[pl.ds(k0, TKS), :])
        p_all = []
        for r in range(GRP):
            s = s_all[heads[r]]
            if bias is not None:
                s = s + bias
            m_old = m_scr[heads[r]]
            m_new = jnp.maximum(m_old, jnp.max(s, axis=-1, keepdims=True))
            alpha = jnp.exp2((m_old - m_new) * c2)
            e = jnp.exp2((s - m_new) * c2)
            m_scr[heads[r]] = m_new
            acc_scr[heads[r]] = alpha * acc_scr[heads[r]]
            p_all.append(e.astype(BF16))
        acc_scr[...] += _dot(jnp.concatenate(p_all, axis=0), vsb_scr[pl.ds(k0, TKS), :])

    n_past = qs // TKS

    def past_tile(kt, carry):
        sel_tile(kt, None)
        return carry

    lax.fori_loop(0, n_past, past_tile, 0)
    spos_d = n_past * TKS + lax.broadcasted_iota(jnp.int32, (1, TKS), 1)
    sel_tile(n_past, jnp.where(spos_d <= tpos, 0.0, -MASK_BIG))
    o_s = acc_scr[:, 0:D] / acc_scr[:, D:D + 1]

    wk = WINDOW + TQ
    w0 = pl.multiple_of(jnp.maximum(qs - WINDOW, 0), TQ)
    dist_w = tpos - (w0 + lax.broadcasted_iota(jnp.int32, (1, wk), 1))
    bias_w = jnp.where((dist_w >= 0) & (dist_w <= WINDOW), 0.0, -MASK_BIG)
    s_w = _dot_nt(qa_scr[...], kwa_scr[pl.ds(w0, wk), :])
    p_w = []
    for r in range(GRP):
        s = s_w[heads[r]] + bias_w
        p_w.append(jnp.exp2((s - jnp.max(s, axis=-1, keepdims=True)) * c2).astype(BF16))
    pv_w = _dot(jnp.concatenate(p_w, axis=0), vwb_scr[pl.ds(w0, wk), :])
    o_w = pv_w[:, 0:D] / pv_w[:, D:D + 1]

    for r in range(GRP):
        gates = jax.nn.sigmoid(gl_ref[0, :, 3 * r:3 * r + 3])
        out = gates[:, 0:1] * o_c[heads[r]] + gates[:, 1:2] * o_s[heads[r]] + gates[:, 2:3] * o_w[heads[r]]
        out_ref[:, r * D:(r + 1) * D] = out.astype(out_ref.dtype)


def _nsa_prompt(q, gate_logits, kvc, rows, ov):
    G, D = NSA_KV_HEADS, HEAD_DIM
    nq = SEQ // TQ
    ncp = kvc.shape[3]
    R = GRP * TQ
    spos = np.arange(SEQ)
    kf_cmp = _key_features(np.arange(ncp) * CMP_STRIDE + (CMP_LEN - 1), None)
    kf_sel = _key_features(spos, spos // SEL_LEN)
    kf_win = _key_features(spos, None)
    row = lambda b, g, qi: b * nq + qi
    kv_spec = lambda slot: pl.BlockSpec((SEQ, D), lambda b, g, qi: (b, slot * G + g))
    const = lambda a: pl.BlockSpec(a.shape, lambda b, g, qi: (0, 0))
    blk = (TQ * GRP * D * 4 + 2 * ncp * D * 4 + 4 * SEQ * D * 4 + 3 * SEQ * D * 2 + SEQ * D * 6 + R * D * 16
           + R * (TKS + WINDOW + TQ) * 6)
    return pl.pallas_call(
        _nsa_prompt_kernel,
        grid=(BATCH, G, nq),
        in_specs=[pl.BlockSpec(memory_space=pl.ANY),
                  pl.BlockSpec((TQ, GRP * D), lambda b, g, qi: (row(b, g, qi), g)),
                  pl.BlockSpec((1, TQ, 3 * GRP), lambda b, g, qi: (g, row(b, g, qi), 0)),
                  pl.BlockSpec((GRP, LANES), lambda b, g, qi: (g, 0)),
                  pl.BlockSpec((1, 1, 1, ncp, D), lambda b, g, qi: (b, 0, g, 0, 0)),
                  pl.BlockSpec((1, 1, 1, ncp, D), lambda b, g, qi: (b, 1, g, 0, 0)),
                  kv_spec(2), kv_spec(3), kv_spec(4), kv_spec(5),
                  const(ov), const(kf_cmp), const(kf_sel), const(kf_win)],
        out_specs=pl.BlockSpec((TQ, GRP * D), lambda b, g, qi: (row(b, g, qi), g)),
        out_shape=jax.ShapeDtypeStruct((M_PAD, NSA_HEADS * D), BF16),
        scratch_shapes=[pltpu.VMEM((R, 2 * D), BF16),
                        pltpu.VMEM((ncp, 2 * D), BF16), pltpu.VMEM((ncp, D), BF16),
                        pltpu.VMEM((SEQ, 2 * D), BF16), pltpu.VMEM((SEQ, 2 * D), BF16),
                        pltpu.VMEM((SEQ, 2 * D), BF16), pltpu.VMEM((SEQ, 2 * D), BF16),
                        pltpu.VMEM((R, 1), F32), pltpu.VMEM((R, 2 * D), F32)],
        compiler_params=_params(("parallel", "parallel", "arbitrary"), blk),
        input_output_aliases={0: 0},
        name="nsa_prompt",
    )(jnp.zeros((M_PAD, NSA_HEADS * D), BF16), q, gate_logits, _sigma_table(), kvc, kvc, rows, rows, rows, rows,
      ov, kf_cmp, kf_sel, kf_win)


T_SAMPLE = PAST_LEN
N_SEL_S = -(-(PAST_LEN + 1) // SEL_LEN)
NB_PAST = PAST_LEN // SEL_LEN
KK_S = min(SEL_TOP, N_SEL_S)
SEL_PAD_S = _round_up(N_SEL_S, LANES)
assert PAST_LEN % SEL_LEN == 0 and PAST_LEN % CMP_STRIDE == 0


def _slope_col(slopes_ref, g):
    r_idx = lax.broadcasted_iota(jnp.int32, (GRP, 1), 0)
    col = jnp.zeros((GRP, 1), F32)
    for r in range(GRP):
        col = jnp.where(r_idx == r, slopes_ref[g * GRP + r], col)
    return col


def _nsa_sample_cmp_win_kernel(slopes_ref, q_ref, kc_ref, vc_ref, ov_ref, kw_ref, vw_ref, kwn_ref, vwn_ref,
                               oc_ref, ow_ref, idx_ref):
    g = pl.program_id(1)
    scale = HEAD_DIM ** -0.5
    q = q_ref[0, 0]
    slope = _slope_col(slopes_ref, g)

    kcb = kc_ref[0, 0, 0].astype(BF16)
    ncp = kcb.shape[0]
    cpos = lax.broadcasted_iota(jnp.int32, (1, ncp), 1) * CMP_STRIDE + (CMP_LEN - 1)
    dist_c = T_SAMPLE - cpos
    p_c = _masked_softmax(_dot_nt(q, kcb) * scale - slope * dist_c.astype(F32), dist_c >= 0)
    oc_ref[0, 0] = _dot(p_c.astype(BF16), vc_ref[0, 0, 0].astype(BF16))
    p_sum = jnp.broadcast_to(jnp.sum(p_c, axis=0, keepdims=True), (SUBLANES, ncp))
    imp = _dot_exact(p_sum, ov_ref[...])[0:1]

    NP = SEL_PAD_S
    cur = T_SAMPLE // SEL_LEN
    j_row = lax.broadcasted_iota(jnp.int32, (1, NP), 1)
    forced = (j_row == 0) | (j_row == cur) | (j_row == cur - 1)
    impm = jnp.where(forced, jnp.inf, jnp.where(j_row <= cur, imp, -jnp.inf))
    sub = lax.broadcasted_iota(jnp.int32, (NP, NP), 0)
    lane = lax.broadcasted_iota(jnp.int32, (NP, NP), 1)
    impm_r = jnp.broadcast_to(impm, (NP, NP))
    rank = _top_rank(impm_r.T, impm_r, sub, lane, 0).astype(F32)
    rank_c = jnp.broadcast_to(rank, (NP, NP)).T
    hit = (rank_c == lane.astype(F32)) & (sub <= cur)
    idx_ref[0, 0] = jnp.sum(jnp.where(hit, sub, 0), axis=0, keepdims=True)

    kwb = kw_ref[0].astype(BF16)
    wlen = kwb.shape[0]
    wpos = T_SAMPLE - wlen + lax.broadcasted_iota(jnp.int32, (1, wlen), 1)
    dist_w = T_SAMPLE - wpos
    valid_w = (dist_w >= 0) & (dist_w <= WINDOW) & (wpos >= 0)
    carry = _flash_step(_flash_init(GRP), _dot_nt(q, kwb) * scale - slope * dist_w.astype(F32), valid_w,
                        vw_ref[0].astype(BF16))
    k_new = kwn_ref[0, 0, 0].astype(BF16).astype(F32)
    v_new = vwn_ref[0, 0, 0].astype(BF16).astype(F32)
    s_new = jnp.sum(q.astype(F32) * k_new, axis=1, keepdims=True) * scale
    _, l_w, acc_w = _flash_one(carry, s_new, v_new)
    ow_ref[0, 0] = acc_w / l_w


def _nsa_sample_cmp_win(slopes, q_s, kvc, ov, win2d, rows_s):
    G, D, DB = NSA_KV_HEADS, HEAD_DIM, DEC_BATCH
    ncp = kvc.shape[3]
    wlen = win2d.shape[1]
    o_spec = pl.BlockSpec((1, 1, GRP, D), lambda b, g, sl: (b, g, 0, 0))
    blk = 2 * ncp * D * 4 + ov.size * 4 + 2 * wlen * D * 4 + 4 * SEL_PAD_S * SEL_PAD_S * 4
    return pl.pallas_call(
        _nsa_sample_cmp_win_kernel,
        grid_spec=pltpu.PrefetchScalarGridSpec(
            num_scalar_prefetch=1,
            grid=(DB, G),
            in_specs=[pl.BlockSpec((1, 1, GRP, D), lambda b, g, sl: (b, g, 0, 0)),
                      pl.BlockSpec((1, 1, 1, ncp, D), lambda b, g, sl: (b, 0, g, 0, 0)),
                      pl.BlockSpec((1, 1, 1, ncp, D), lambda b, g, sl: (b, 1, g, 0, 0)),
                      pl.BlockSpec(ov.shape, lambda b, g, sl: (0, 0)),
                      pl.BlockSpec((1, wlen, D), lambda b, g, sl: (b, 0, g)),
                      pl.BlockSpec((1, wlen, D), lambda b, g, sl: (b, 0, G + g)),
                      pl.BlockSpec((1, 1, 1, 1, D), lambda b, g, sl: (b, 4, g, 0, 0)),
                      pl.BlockSpec((1, 1, 1, 1, D), lambda b, g, sl: (b, 5, g, 0, 0))],
            out_specs=[o_spec, o_spec, pl.BlockSpec((1, 1, 1, SEL_PAD_S), lambda b, g, sl: (b, g, 0, 0))],
        ),
        out_shape=[jax.ShapeDtypeStruct((DB, G, GRP, D), F32),
                   jax.ShapeDtypeStruct((DB, G, GRP, D), F32),
                   jax.ShapeDtypeStruct((DB, G, 1, SEL_PAD_S), jnp.int32)],
        compiler_params=_params(("parallel", "parallel"), blk),
        name="nsa_sample_cmp_win",
    )(slopes, q_s, kvc, kvc, ov, win2d, win2d, rows_s, rows_s)


def _nsa_sample_sel_kernel(pt_ref, idx_ref, slopes_ref, q_ref, *refs):
    ks_refs, vs_refs = refs[:KK_S], refs[KK_S:2 * KK_S]
    ksn_ref, vsn_ref, oc_ref, ow_ref, gl_ref, out_ref = refs[2 * KK_S:]
    b, g = pl.program_id(0), pl.program_id(1)
    scale = HEAD_DIM ** -0.5
    q = q_ref[0, 0]
    slope = _slope_col(slopes_ref, g)
    off = lax.broadcasted_iota(jnp.int32, (1, SEL_LEN), 1)

    scores, valids = [], []
    has_new = False
    for kk in range(KK_S):
        j = idx_ref[(b * NSA_KV_HEADS + g) * KK_S + kk]
        in_cache = j < NB_PAST
        has_new = jnp.logical_or(has_new, jnp.logical_not(in_cache))
        dist = T_SAMPLE - (jnp.minimum(j, NB_PAST - 1) * SEL_LEN + off)
        valid = (dist >= 0) & in_cache
        s = _dot_nt(q, ks_refs[kk][0].astype(BF16)) * scale - slope * dist.astype(F32)
        scores.append(jnp.where(valid, s, NEG))
        valids.append(valid)
    k_new = ksn_ref[0, 0, 0].astype(BF16).astype(F32)
    v_new = vsn_ref[0, 0, 0].astype(BF16).astype(F32)
    s_new = jnp.sum(q.astype(F32) * k_new, axis=1, keepdims=True) * scale
    s_new = jnp.where(has_new, s_new, NEG)

    m = s_new
    for s in scores:
        m = jnp.maximum(m, jnp.max(s, axis=-1, keepdims=True))
    e_new = jnp.exp(s_new - m)
    l = e_new
    acc = jnp.where(has_new, e_new, 0.0) * v_new
    for kk in range(KK_S):
        e = jnp.exp(scores[kk] - m)
        l = l + jnp.sum(e, axis=-1, keepdims=True)
        acc = acc + _dot(jnp.where(valids[kk], e, 0.0).astype(BF16), vs_refs[kk][0].astype(BF16))
    gates = jax.nn.sigmoid(gl_ref[0, 0])
    out_ref[0, 0] = gates[0] * oc_ref[0, 0] + gates[1] * (acc / l) + gates[2] * ow_ref[0, 0]


def _nsa_sample_sel(page_table_flat, idx_flat, slopes, q_s, cache2d, rows_s, o_c, o_w, gate_logits_s):
    G, D, DB = NSA_KV_HEADS, HEAD_DIM, DEC_BATCH
    n_pages = PAST_LEN // PAGE_SIZE
    bpp = PAGE_SIZE // SEL_LEN

    def blk_map(b, g, pt, idx, sl, *, kk, slot):
        j = jnp.minimum(idx[(b * G + g) * KK_S + kk], NB_PAST - 1)
        return (pt[b * n_pages + j // bpp], j % bpp, slot * G + g)

    o_spec = pl.BlockSpec((1, 1, GRP, D), lambda b, g, pt, idx, sl: (b, g, 0, 0))
    blk_specs = [pl.BlockSpec((1, SEL_LEN, D), functools.partial(blk_map, kk=kk, slot=slot))
                 for slot in (2, 3) for kk in range(KK_S)]
    blk = 2 * KK_S * SEL_LEN * D * 4 + 6 * GRP * D * 4
    return pl.pallas_call(
        _nsa_sample_sel_kernel,
        grid_spec=pltpu.PrefetchScalarGridSpec(
            num_scalar_prefetch=3,
            grid=(DB, G),
            in_specs=[o_spec, *blk_specs,
                      pl.BlockSpec((1, 1, 1, 1, D), lambda b, g, pt, idx, sl: (b, 2, g, 0, 0)),
                      pl.BlockSpec((1, 1, 1, 1, D), lambda b, g, pt, idx, sl: (b, 3, g, 0, 0)),
                      o_spec, o_spec,
                      pl.BlockSpec((1, 1, 3, GRP, 1), lambda b, g, pt, idx, sl: (b, g, 0, 0, 0))],
            out_specs=o_spec,
        ),
        out_shape=jax.ShapeDtypeStruct((DB, G, GRP, D), F32),
        compiler_params=_params(("parallel", "parallel"), blk),
        name="nsa_sample_sel",
    )(page_table_flat, idx_flat, slopes, q_s, *([cache2d] * (2 * KK_S)), rows_s, rows_s, o_c, o_w, gate_logits_s)


def _stack_rows(prompt_rows, sample_rows):
    pad = jnp.zeros((M_PAD - M_PROMPT - M_SAMPLE, prompt_rows.shape[1]), prompt_rows.dtype)
    return jnp.concatenate([prompt_rows, sample_rows.astype(prompt_rows.dtype), pad], axis=0)


def _tail_rows_kernel(buf_ref, s_ref, o_ref):
    del buf_ref
    ns = s_ref.shape[0]
    o_ref[0:ns, :] = s_ref[...]
    o_ref[ns:, :] = jnp.zeros((o_ref.shape[0] - ns, o_ref.shape[1]), o_ref.dtype)


def _fill_tail_rows(buf, sample_rows):
    tail = M_PAD - M_PROMPT
    ns = _round_up(M_SAMPLE, 16)
    assert M_PROMPT % tail == 0 and ns <= tail and buf.dtype == BF16
    s = jnp.pad(sample_rows.astype(BF16), ((0, ns - M_SAMPLE), (0, 0)))
    cols = buf.shape[1]
    return pl.pallas_call(
        _tail_rows_kernel,
        grid=(1,),
        in_specs=[pl.BlockSpec(memory_space=pl.ANY), pl.BlockSpec((ns, cols), lambda i: (0, 0))],
        out_specs=pl.BlockSpec((tail, cols), lambda i: (M_PROMPT // tail, 0)),
        out_shape=jax.ShapeDtypeStruct(buf.shape, buf.dtype),
        input_output_aliases={0: 0},
        compiler_params=_params(("arbitrary",), (tail + ns) * cols * 2),
        name="fill_tail_rows",
    )(buf, s)


def _ffn_block(x, g, w_gate_up, w_down, layer):
    h = _rmsnorm(x, g, BF16)
    return _matmul_fullk(_ffn_up(h, w_gate_up, layer), w_down, layer, D_MODEL, F32, residual=x, k_split=2)


def kernel(x_prompt, x_sample, cache_nsa_kv, page_table, state_win_kv, state_mlstm_C, state_mlstm_n, state_mlstm_m,
           g_mix, g_ffn, g_kv, g_final, w_in_a, b_gate_a, g_head_a, w_out_a, w_qg_b, w_out_b, w_kv,
           pe_cmp, w_cmp1, w_cmp2, w_gate_up, w_down):
    H, DK, DV = MLSTM_HEADS, MLSTM_QK_DIM, MLSTM_V_DIM
    G, D, DB = NSA_KV_HEADS, HEAD_DIM, DEC_BATCH
    hd = NSA_HEADS * D
    n_main = 2 * H * DK + 2 * H * DV
    x = _stack_rows(x_prompt.reshape(M_PROMPT, D_MODEL), x_sample.reshape(M_SAMPLE, D_MODEL))
    w_in_t = jnp.swapaxes(w_in_a, 1, 2)
    w_qg_t = jnp.swapaxes(w_qg_b, 1, 2)
    w_gate_a = jnp.swapaxes(w_in_t[:, n_main:, :], 1, 2)
    w_gate_b = jnp.swapaxes(w_qg_t[:, hd:, :], 1, 2)

    c_p, n_p, m_p, c_s, n_s, m_s = [], [], [], [], [], []
    for l in range(N_A):
        h, gates = _rmsnorm_with_gates(x, g_mix[l], w_gate_a[l], b_gate_a[l], True)
        z = _matmul_fullk_nt(h, w_in_t, l, n_main, F32)
        gates_t = gates[:M_PROMPT, :2 * H].T.reshape(2, H, 1, M_PROMPT)
        hs_p, cp, np_, mp = _mlstm_prompt(z, gates_t, g_head_a[l])
        gs = gates[S0:S0 + DB, :2 * H]
        lane_b = lambda a: jnp.broadcast_to(a[:, :, None, None], (DB, H, 1, LANES))
        hs_s, cs, ns, ms = _mlstm_sample(z, lane_b(gs[:, :H]), lane_b(gs[:, H:]), lane_b(state_mlstm_m[l]),
                                         state_mlstm_C[l], state_mlstm_n[l].reshape(DB, H, 1, DK), g_head_a[l])
        hs = _fill_tail_rows(hs_p, hs_s.reshape(DB, H * DV))
        x = _matmul_fullk(hs, w_out_a, l, D_MODEL, F32, residual=x)
        x = _ffn_block(x, g_ffn[l], w_gate_up, w_down, l)
        c_p.append(cp)
        n_p.append(np_.reshape(BATCH, H, DK))
        m_p.append(mp.reshape(BATCH, H))
        c_s.append(cs)
        n_s.append(ns.reshape(DB, H, DK))
        m_s.append(ms[:, :, 0, 0])

    rows = _matmul_fullk(_rmsnorm(x, g_kv, BF16), w_kv[None], 0, 6 * G * D, F32)
    rows_s = rows[S0:S0 + DB].reshape(DB, 6, G, 1, D)
    w1r = _compress_weights(w_cmp1)
    kvc_p = _compress_finish(_compress_proj_prompt(rows, w1r), pe_cmp, w_cmp1, w_cmp2)
    n_pool = cache_nsa_kv.shape[0]
    pt_flat = page_table.reshape(-1).astype(jnp.int32)
    cache2d = cache_nsa_kv.reshape(n_pool, PAGE_SIZE, 4 * G * D)
    kvc_s = _compress_finish(_compress_proj_sample(cache2d, pt_flat, w1r), pe_cmp, w_cmp1, w_cmp2)
    win2d = state_win_kv.reshape(DB, state_win_kv.shape[1], 2 * G * D)

    slopes = jnp.exp2(-8.0 * jnp.arange(1, NSA_HEADS + 1, dtype=F32) / NSA_HEADS)
    n_chunks_p = SEQ // CMP_STRIDE
    ov_p = _selection_overlap(n_chunks_p - CMP_R + 1, SEQ // SEL_LEN, n_chunks_p, LANES)
    n_chunks_s = (PAST_LEN + M_SAMPLE // DB) // CMP_STRIDE
    ov_s = _selection_overlap(n_chunks_s - CMP_R + 1, N_SEL_S, n_chunks_s, SEL_PAD_S)

    for l in range(N_B):
        h, gate_logits = _rmsnorm_with_gates(x, g_mix[N_A + l], w_gate_b[l], jnp.zeros((3 * NSA_HEADS,), F32),
                                             False)
        q = _matmul_fullk_nt(h, w_qg_t, l, hd, BF16)
        gl = gate_logits[:, :3 * NSA_HEADS]
        gl_p = gl.reshape(M_PAD, G, 3 * GRP).transpose(1, 0, 2)
        att_p = _nsa_prompt(q, gl_p, kvc_p, rows, ov_p)
        q_s = q[S0:S0 + DB].reshape(DB, G, GRP, D)
        gl_s = gl[S0:S0 + DB].reshape(DB, G, GRP, 3).transpose(0, 1, 3, 2)[..., None]
        o_c, o_w, idx = _nsa_sample_cmp_win(slopes, q_s, kvc_s, ov_s, win2d, rows_s)
        idx_flat = idx[:, :, 0, :KK_S].reshape(-1)
        att_s = _nsa_sample_sel(pt_flat, idx_flat, slopes, q_s, cache2d, rows_s, o_c, o_w, gl_s)
        att = _fill_tail_rows(att_p, att_s.reshape(DB, hd))
        x = _matmul_fullk(att, w_out_b, l, D_MODEL, F32, residual=x)
        x = _ffn_block(x, g_ffn[N_A + l], w_gate_up, w_down, N_A + l)

    y_p = _rmsnorm(x, g_final, F32, n_rows=M_PROMPT)
    y_s = _rmsnorm(x[S0:], g_final, F32)[:DB]
    wp = min(WINDOW, SEQ)
    rows_p = rows[:M_PROMPT].reshape(BATCH, SEQ, 6, G, D)
    rows_s6 = rows[S0:S0 + DB].reshape(DB, 1, 6, G, D)
    wbuf = state_win_kv.shape[1]
    win_kv_sample = jnp.concatenate([state_win_kv, rows_s6[:, :, 4:]], axis=1)[:, -wbuf:]
    return (y_p.reshape(BATCH, SEQ, D_MODEL), y_s.reshape(DB, 1, D_MODEL),
            rows_p[:, :, :4], rows_p[:, SEQ - wp:, 4:],
            jnp.stack(c_p), jnp.stack(n_p), jnp.stack(m_p),
            rows_s6[:, :, :4], win_kv_sample,
            jnp.stack(c_s), jnp.stack(n_s), jnp.stack(m_s))
```

```python
import functools

import numpy as np
import jax
import jax.numpy as jnp
from jax import lax
from jax.experimental import pallas as pl
from jax.experimental.pallas import tpu as pltpu

D_MODEL = 4096
BATCH = 4
SEQ = 2048
DEPTH = 4
DEC_BATCH = 8
DEC_SEQ = 1
PAST_LEN = 8192
PAGE_SIZE = 128

N_A = DEPTH // 2
N_B = DEPTH - N_A
MLSTM_HEADS = 8
MLSTM_QK_DIM = D_MODEL // 16
MLSTM_V_DIM = D_MODEL // MLSTM_HEADS
GATE_CAP = 15.0
HEAD_DIM = 128
NSA_HEADS = D_MODEL // HEAD_DIM
NSA_KV_HEADS = 4
GRP = NSA_HEADS // NSA_KV_HEADS
CMP_LEN = 32
CMP_STRIDE = 16
CMP_HID = HEAD_DIM
SEL_LEN = 64
SEL_TOP = 16
WINDOW = 512
D_FF = -(-8 * D_MODEL // (3 * 256)) * 256
EPS = 1e-6
NEG = -1e30

F32 = jnp.float32
BF16 = jnp.bfloat16

LANES = 128
SUBLANES = 8
VMEM_BYTES_V7X = 64 * 1024 * 1024


def _round_up(n, m):
    return -(-n // m) * m


def _pick_tile(n, cap, align=LANES):
    best = None
    for t in range(align, min(n, cap) + 1, align):
        if n % t == 0:
            best = t
    assert best is not None, (n, cap, align)
    return best


M_PROMPT = BATCH * SEQ
M_SAMPLE = DEC_BATCH * DEC_SEQ
S0 = M_PROMPT
M_PAD = _round_up(M_PROMPT + M_SAMPLE, LANES)
TM = _pick_tile(M_PAD, 2048)
TM_WIDE = _pick_tile(M_PAD, 2304, align=16)
ROW_TILE_BYTES_MAX = 18 << 20
TR = _pick_tile(M_PAD, 384, align=64)
TN_FULLK = 256
MLSTM_L = 256
TQ = 128
TKS = 1024
N_GATE_PAD = LANES
D_FF_PAD = _round_up(D_FF, 1024)
CMP_R = CMP_LEN // CMP_STRIDE
SAMPLE_PAGES_PER_STEP = 8
PERM_ROWS = LANES

assert DEC_SEQ == 1 and S0 % 16 == 0
assert SEQ % PERM_ROWS == 0 and PAGE_SIZE % PERM_ROWS == 0
assert (PERM_ROWS // CMP_STRIDE) & (PERM_ROWS // CMP_STRIDE - 1) == 0
assert SEQ % MLSTM_L == 0 and SEQ % TQ == 0 and SEQ % TKS == 0 and TKS % TQ == 0
assert PAST_LEN % PAGE_SIZE == 0 and PAGE_SIZE % SEL_LEN == 0 and PAGE_SIZE % CMP_STRIDE == 0
assert CMP_R == 2 and SEQ // SEL_LEN <= LANES


def _vmem_limit(block_bytes, single_bytes):
    return int(min(single_bytes + 2 * block_bytes + (12 << 20), VMEM_BYTES_V7X - (8 << 20)))


def _params(sem, block_bytes, single_bytes=0):
    return pltpu.CompilerParams(dimension_semantics=sem,
                                vmem_limit_bytes=_vmem_limit(block_bytes, single_bytes))


def _dot(a, b):
    return jnp.dot(a, b, preferred_element_type=F32)


def _dot_nt(a, b):
    return lax.dot_general(a, b, (((1,), (1,)), ((), ())), preferred_element_type=F32)


def _dot_tn(a, b):
    return lax.dot_general(a, b, (((0,), (0,)), ((), ())), preferred_element_type=F32)


def _dot_exact(a, b):
    return jnp.dot(a, b, precision=lax.Precision.HIGHEST, preferred_element_type=F32)


def _dot_split3(a, b):
    a_hi = a.astype(BF16)
    a_lo = (a - a_hi.astype(F32)).astype(BF16)
    b_hi = b.astype(BF16)
    b_lo = (b - b_hi.astype(F32)).astype(BF16)
    return _dot(a_hi, b_hi) + (_dot(a_hi, b_lo) + _dot(a_lo, b_hi))


def _log_sigmoid(x):
    return jnp.minimum(x, 0.0) - jnp.log1p(jnp.exp(-jnp.abs(x)))


def _col_form(row, n):
    return jnp.broadcast_to(row, (n, n)).T


def _rms_kernel(x_ref, g_ref, o_ref):
    x = x_ref[...]
    y = x * lax.rsqrt(jnp.mean(x * x, axis=-1, keepdims=True) + EPS) * g_ref[...]
    o_ref[...] = y.astype(o_ref.dtype)


def _rmsnorm(x, g, out_dtype, n_rows=None):
    d = x.shape[1]
    m = x.shape[0] if n_rows is None else n_rows
    tr = _pick_tile(m, TR, align=SUBLANES * 8)
    blk = tr * d * (8 + jnp.dtype(out_dtype).itemsize)
    return pl.pallas_call(
        _rms_kernel,
        grid=(m // tr,),
        in_specs=[pl.BlockSpec((tr, d), lambda i: (i, 0)), pl.BlockSpec((1, d), lambda i: (0, 0))],
        out_specs=pl.BlockSpec((tr, d), lambda i: (i, 0)),
        out_shape=jax.ShapeDtypeStruct((m, d), out_dtype),
        compiler_params=_params(("parallel",), blk),
        name="rmsnorm",
    )(x, g.reshape(1, d))


def _rms_gate_kernel(x_ref, g_ref, w_ref, b_ref, h_ref, o_ref, *, mlstm_gates):
    x = x_ref[...]
    y = x * lax.rsqrt(jnp.mean(x * x, axis=-1, keepdims=True) + EPS) * g_ref[...]
    h_ref[...] = y.astype(h_ref.dtype)
    z = _dot_split3(y, w_ref[...]) + b_ref[...]
    if mlstm_gates:
        gl = GATE_CAP * jnp.tanh(z / GATE_CAP)
        lane = lax.broadcasted_iota(jnp.int32, z.shape, 1)
        z = jnp.where(lane < MLSTM_HEADS, gl, _log_sigmoid(gl))
    o_ref[...] = z


def _rmsnorm_with_gates(x, g, w_gate, b_gate, mlstm_gates):
    m, d = x.shape
    ng = w_gate.shape[1]
    wp = jnp.pad(w_gate, ((0, 0), (0, N_GATE_PAD - ng)))
    bp = jnp.pad(b_gate, (0, N_GATE_PAD - ng)).reshape(1, N_GATE_PAD)
    blk = TR * d * 10 + d * N_GATE_PAD * 6 + TR * N_GATE_PAD * 4
    return pl.pallas_call(
        functools.partial(_rms_gate_kernel, mlstm_gates=mlstm_gates),
        grid=(m // TR,),
        in_specs=[pl.BlockSpec((TR, d), lambda i: (i, 0)),
                  pl.BlockSpec((1, d), lambda i: (0, 0)),
                  pl.BlockSpec((d, N_GATE_PAD), lambda i: (0, 0)),
                  pl.BlockSpec((1, N_GATE_PAD), lambda i: (0, 0))],
        out_specs=[pl.BlockSpec((TR, d), lambda i: (i, 0)),
                   pl.BlockSpec((TR, N_GATE_PAD), lambda i: (i, 0))],
        out_shape=[jax.ShapeDtypeStruct((m, d), BF16), jax.ShapeDtypeStruct((m, N_GATE_PAD), F32)],
        compiler_params=_params(("parallel",), blk),
        name="rmsnorm_gates",
    )(x, g.reshape(1, d), wp, bp)


def _mm_fullk_kernel(*refs, has_res, rows_valid):
    x_ref, w_ref, o_ref = refs[0], refs[1], refs[-1]
    w = w_ref[...]
    if rows_valid < w.shape[0]:
        w = jnp.where(lax.broadcasted_iota(jnp.int32, w.shape, 0) < rows_valid, w, 0.0)
    r = _dot(x_ref[...], w.astype(BF16))
    if has_res:
        r = r + refs[2][...]
    o_ref[...] = r.astype(o_ref.dtype)


def _row_tile_rows(kb):
    return TM_WIDE if TM_WIDE * kb * 2 <= ROW_TILE_BYTES_MAX else TM


def _row_tile_spec(tm, kb, kh=0):
    return pl.BlockSpec((tm, kb), lambda i, j: (i, kh), pipeline_mode=pl.Buffered(1))


def _matmul_fullk(x, w, layer, n_cols, out_dtype, residual=None, k_split=1):
    m, kd = x.shape
    tn = TN_FULLK
    kb = kd // k_split
    tm = _row_tile_rows(kb)
    k_rows = w.shape[1]
    assert n_cols % tn == 0 and kd % k_split == 0 and kb % LANES == 0 and w.dtype == F32
    assert (k_split - 1) * kb < k_rows <= kd
    out = residual
    for kh in range(k_split):
        last = kh == k_split - 1
        dt = out_dtype if last else F32
        osz = jnp.dtype(dt).itemsize
        in_specs = [_row_tile_spec(tm, kb, kh), pl.BlockSpec((None, kb, tn), lambda i, j, kh=kh: (layer, kh, j))]
        args = [x, w]
        streamed = kb * tn * 4 + tm * tn * osz
        if out is not None:
            in_specs.append(pl.BlockSpec((tm, tn), lambda i, j: (i, j)))
            args.append(out)
            streamed += tm * tn * 4
        out = pl.pallas_call(
            functools.partial(_mm_fullk_kernel, has_res=out is not None, rows_valid=min(kb, k_rows - kh * kb)),
            grid=(m // tm, n_cols // tn),
            in_specs=in_specs,
            out_specs=pl.BlockSpec((tm, tn), lambda i, j: (i, j)),
            out_shape=jax.ShapeDtypeStruct((m, n_cols), dt),
            compiler_params=_params(("parallel", "arbitrary"), streamed, single_bytes=tm * kb * 2),
            name="matmul_fullk",
        )(*args)
    return out


def _mm_fullk_nt_kernel(x_ref, wt_ref, o_ref):
    o_ref[...] = _dot_nt(x_ref[...], wt_ref[...].astype(BF16)).astype(o_ref.dtype)


def _matmul_fullk_nt(x, wt, layer, n_cols, out_dtype):
    m, kd = x.shape
    tn = 2 * TN_FULLK
    tm = _row_tile_rows(kd)
    assert n_cols % tn == 0 and wt.shape[2] == kd and wt.dtype == F32
    streamed = kd * tn * 4 + tm * tn * jnp.dtype(out_dtype).itemsize
    return pl.pallas_call(
        _mm_fullk_nt_kernel,
        grid=(m // tm, n_cols // tn),
        in_specs=[_row_tile_spec(tm, kd), pl.BlockSpec((None, tn, kd), lambda i, j: (layer, j, 0))],
        out_specs=pl.BlockSpec((tm, tn), lambda i, j: (i, j)),
        out_shape=jax.ShapeDtypeStruct((m, n_cols), out_dtype),
        compiler_params=_params(("parallel", "arbitrary"), streamed, single_bytes=tm * kd * 2),
        name="matmul_fullk_nt",
    )(x, wt)


def _ffn_up_kernel(x_ref, wg_ref, wu_ref, o_ref, *, n_real):
    j = pl.program_id(1)

    @pl.when(j < n_real)
    def _():
        x = x_ref[...]
        gate = _dot(x, wg_ref[...].astype(BF16))
        up = _dot(x, wu_ref[...].astype(BF16))
        o_ref[...] = (jax.nn.silu(gate) * up).astype(o_ref.dtype)

    @pl.when(j >= n_real)
    def _():
        o_ref[...] = jnp.zeros_like(o_ref)


def _ffn_up(x, w_gate_up, layer):
    m, kd = x.shape
    tn = TN_FULLK
    tm = _row_tile_rows(kd)
    assert D_FF % tn == 0 and D_FF_PAD % tn == 0
    n_real = D_FF // tn
    col = lambda j: jnp.minimum(j, n_real - 1)
    streamed = 2 * kd * tn * 4 + tm * tn * 2
    return pl.pallas_call(
        functools.partial(_ffn_up_kernel, n_real=n_real),
        grid=(m // tm, D_FF_PAD // tn),
        in_specs=[_row_tile_spec(tm, kd),
                  pl.BlockSpec((None, kd, tn), lambda i, j: (layer, 0, col(j))),
                  pl.BlockSpec((None, kd, tn), lambda i, j: (layer, 0, n_real + col(j)))],
        out_specs=pl.BlockSpec((tm, tn), lambda i, j: (i, j)),
        out_shape=jax.ShapeDtypeStruct((m, D_FF_PAD), BF16),
        compiler_params=_params(("parallel", "arbitrary"), streamed, single_bytes=tm * kd * 2),
        name="ffn_up",
    )(x, w_gate_up, w_gate_up)


def _mlstm_prompt_kernel(zero_ref, q_ref, k_ref, v_ref, o_ref, g_ref, gh_ref, hs_ref, c_ref, n_ref, m_ref):
    del zero_ref
    L, DK, DV = MLSTM_L, MLSTM_QK_DIM, MLSTM_V_DIM

    @pl.when(pl.program_id(1) == 0)
    def _():
        c_ref[...] = jnp.zeros_like(c_ref)
        n_ref[...] = jnp.zeros_like(n_ref)
        m_ref[...] = jnp.zeros_like(m_ref)

    t_idx = lax.broadcasted_iota(jnp.int32, (L, L), 0)
    s_idx = lax.broadcasted_iota(jnp.int32, (L, L), 1)
    for hd in range(MLSTM_HEADS):
        ig_row = g_ref[0, hd]
        lf_row = g_ref[1, hd]
        m_prev = m_ref[0, hd]
        lf_c = _col_form(lf_row, L)
        b_row = jnp.sum(jnp.where(t_idx <= s_idx, lf_c, 0.0), axis=0, keepdims=True)
        b_r = jnp.broadcast_to(b_row, (L, L))
        b_c = b_r.T
        ig_r = jnp.broadcast_to(ig_row, (L, L))
        log_d = jnp.where(s_idx <= t_idx, b_c - b_r + ig_r, -jnp.inf)
        b_col = b_c[:, 0:1]
        inter = b_col + m_prev
        m_t = jnp.maximum(jnp.max(log_d, axis=1, keepdims=True), inter)
        d = jnp.exp(log_d - m_t)
        w_inter = jnp.exp(inter - m_t)

        q = q_ref[:, hd * DK:(hd + 1) * DK] * (DK ** -0.5)
        k = k_ref[:, hd * DK:(hd + 1) * DK]
        qb = q.astype(BF16)
        kb = k.astype(BF16)
        vb = v_ref[:, hd * DV:(hd + 1) * DV].astype(BF16)
        s = _dot_nt(qb, kb) * d
        c_prev = c_ref[0, hd]
        n_prev = n_ref[0, hd]
        num = w_inter * _dot(qb, c_prev.astype(BF16)) + _dot(s.astype(BF16), vb)
        den = w_inter * jnp.sum(q * n_prev, axis=1, keepdims=True) + jnp.sum(s, axis=1, keepdims=True)
        h = num / jnp.maximum(jnp.abs(den), jnp.exp(-m_t))
        hn = h * lax.rsqrt(jnp.mean(h * h, axis=1, keepdims=True) + EPS)
        cols = slice(hd * DV, (hd + 1) * DV)
        hs_ref[:, cols] = (hn * gh_ref[:, cols] * jax.nn.sigmoid(o_ref[:, cols])).astype(hs_ref.dtype)

        m_new = m_t[L - 1:L, :]
        b_last = b_row[:, L - 1:L]
        ig_col = ig_r.T[:, 0:1]
        decay = jnp.exp(b_last - b_col + ig_col - m_new)
        carry = jnp.exp(b_last + m_prev - m_new)
        kd = k * decay
        c_ref[0, hd] = carry * c_prev + _dot_tn(kd.astype(BF16), vb)
        n_ref[0, hd] = carry * n_prev + jnp.sum(kd, axis=0, keepdims=True)
        m_ref[0, hd] = m_new


def _mlstm_prompt(z, gates_t, g_head):
    H, DK, DV, L = MLSTM_HEADS, MLSTM_QK_DIM, MLSTM_V_DIM, MLSTM_L
    nc = SEQ // L
    assert (H * DV) % (H * DK) == 0
    v0 = (2 * H * DK) // (H * DV)
    row = lambda b, c: b * nc + c
    blk = L * (2 * H * DK + 2 * H * DV) * 4 + L * H * DV * 2 + H * DK * DV * 4
    return pl.pallas_call(
        _mlstm_prompt_kernel,
        grid=(BATCH, nc),
        in_specs=[pl.BlockSpec(memory_space=pl.ANY),
                  pl.BlockSpec((L, H * DK), lambda b, c: (row(b, c), 0)),
                  pl.BlockSpec((L, H * DK), lambda b, c: (row(b, c), 1)),
                  pl.BlockSpec((L, H * DV), lambda b, c: (row(b, c), v0)),
                  pl.BlockSpec((L, H * DV), lambda b, c: (row(b, c), v0 + 1)),
                  pl.BlockSpec((2, H, 1, L), lambda b, c: (0, 0, 0, row(b, c))),
                  pl.BlockSpec((1, H * DV), lambda b, c: (0, 0))],
        out_specs=[pl.BlockSpec((L, H * DV), lambda b, c: (row(b, c), 0)),
                   pl.BlockSpec((1, H, DK, DV), lambda b, c: (b, 0, 0, 0)),
                   pl.BlockSpec((1, H, 1, DK), lambda b, c: (b, 0, 0, 0)),
                   pl.BlockSpec((1, H, 1, 1), lambda b, c: (b, 0, 0, 0))],
        out_shape=[jax.ShapeDtypeStruct((M_PAD, H * DV), BF16),
                   jax.ShapeDtypeStruct((BATCH, H, DK, DV), F32),
                   jax.ShapeDtypeStruct((BATCH, H, 1, DK), F32),
                   jax.ShapeDtypeStruct((BATCH, H, 1, 1), F32)],
        compiler_params=_params(("parallel", "arbitrary"), blk),
        input_output_aliases={0: 0},
        name="mlstm_prompt",
    )(jnp.zeros((M_PAD, H * DV), BF16), z, z, z, z, gates_t, g_head.reshape(1, H * DV))


def _mlstm_sample_kernel(q_ref, k_ref, v_ref, o_ref, ig_ref, lf_ref, m0_ref, c0_ref, n0_ref, gh_ref,
                         hs_ref, c_ref, n_ref, m_ref):
    DK = MLSTM_QK_DIM
    b = pl.program_id(0)
    q = q_ref[pl.ds(b, 1), :] * (DK ** -0.5)
    k = k_ref[pl.ds(b, 1), :]
    v = v_ref[pl.ds(b, 1), :]
    og = o_ref[pl.ds(b, 1), :]
    ig = ig_ref[0, 0]
    lf = lf_ref[0, 0]
    m0 = m0_ref[0, 0]
    m_t = jnp.maximum(ig, lf + m0)
    d = jnp.exp(ig - m_t)[:, 0:1]
    w = jnp.exp(lf + m0 - m_t)[:, 0:1]
    c0 = c0_ref[0, 0]
    n0 = n0_ref[0, 0]
    q_col = _col_form(q, DK)[:, 0:1]
    k_col = _col_form(k, DK)[:, 0:1]
    s = jnp.sum(q * k, axis=1, keepdims=True) * d
    num = w * jnp.sum(q_col * c0, axis=0, keepdims=True) + s * v
    den = w * jnp.sum(q * n0, axis=1, keepdims=True) + s
    h = num / jnp.maximum(jnp.abs(den), jnp.exp(-m_t)[:, 0:1])
    hn = h * lax.rsqrt(jnp.mean(h * h, axis=1, keepdims=True) + EPS)
    hs_ref[0] = hn * gh_ref[...] * jax.nn.sigmoid(og)
    c_ref[0, 0] = w * c0 + d * (k_col * v)
    n_ref[0, 0] = w * n0 + d * k
    m_ref[0, 0] = m_t


def _mlstm_sample(z, ig_b, lf_b, m_b, c0, n0, g_head):
    H, DK, DV = MLSTM_HEADS, MLSTM_QK_DIM, MLSTM_V_DIM
    DB = DEC_BATCH
    r0 = S0 // SUBLANES
    v0 = (2 * H * DK) // DV
    sc = pl.BlockSpec((1, 1, 1, LANES), lambda b, h: (b, h, 0, 0))
    blk = 2 * DK * DV * 4 + SUBLANES * (2 * DK + 2 * DV) * 4
    return pl.pallas_call(
        _mlstm_sample_kernel,
        grid=(DB, H),
        in_specs=[pl.BlockSpec((SUBLANES, DK), lambda b, h: (r0, h)),
                  pl.BlockSpec((SUBLANES, DK), lambda b, h: (r0, H + h)),
                  pl.BlockSpec((SUBLANES, DV), lambda b, h: (r0, v0 + h)),
                  pl.BlockSpec((SUBLANES, DV), lambda b, h: (r0, v0 + H + h)),
                  sc, sc, sc,
                  pl.BlockSpec((1, 1, DK, DV), lambda b, h: (b, h, 0, 0)),
                  pl.BlockSpec((1, 1, 1, DK), lambda b, h: (b, h, 0, 0)),
                  pl.BlockSpec((1, DV), lambda b, h: (0, h))],
        out_specs=[pl.BlockSpec((1, 1, DV), lambda b, h: (b, 0, h)),
                   pl.BlockSpec((1, 1, DK, DV), lambda b, h: (b, h, 0, 0)),
                   pl.BlockSpec((1, 1, 1, DK), lambda b, h: (b, h, 0, 0)),
                   sc],
        out_shape=[jax.ShapeDtypeStruct((DB, 1, H * DV), F32),
                   jax.ShapeDtypeStruct((DB, H, DK, DV), F32),
                   jax.ShapeDtypeStruct((DB, H, 1, DK), F32),
                   jax.ShapeDtypeStruct((DB, H, 1, LANES), F32)],
        compiler_params=_params(("parallel", "parallel"), blk),
        name="mlstm_sample",
    )(z, z, z, z, ig_b, lf_b, m_b, c0, n0, g_head.reshape(1, H * DV))


def _compress_proj_kernel(*refs, n_in, groups_per_in, lead):
    x_refs = refs[-(n_in + 3):-3]
    w_ref, p_ref, xp_scr = refs[-3], refs[-2], refs[-1]
    G, D = NSA_KV_HEADS, HEAD_DIM
    cpg = PERM_ROWS // CMP_STRIDE
    nch = n_in * groups_per_in * cpg
    i_idx = lax.broadcasted_iota(jnp.int32, (PERM_ROWS, PERM_ROWS), 0)
    j_idx = lax.broadcasted_iota(jnp.int32, (PERM_ROWS, PERM_ROWS), 1)
    src_row = (i_idx & (cpg - 1)) * CMP_STRIDE + (i_idx >> (cpg.bit_length() - 1))
    perm = (j_idx == src_row).astype(BF16)
    for i, xr in enumerate(x_refs):
        for p in range(groups_per_in):
            rows = pl.ds(p * PERM_ROWS, PERM_ROWS)
            xg = xr[0, rows, :] if lead else xr[rows, :]
            xg = _dot(perm, xg.astype(BF16))
            ch0 = (i * groups_per_in + p) * cpg
            for c in range(CMP_STRIDE):
                xp_scr[c, ch0:ch0 + cpg, :] = xg[c * cpg:(c + 1) * cpg, :]
    for s in range(2):
        acc = jnp.zeros((G * nch, CMP_R * CMP_HID), F32)
        for c in range(CMP_STRIDE):
            xs = jnp.concatenate([xp_scr[c, :, (s * G + g) * D:(s * G + g + 1) * D] for g in range(G)], axis=0)
            acc = acc + _dot(xs.astype(BF16), w_ref[s, c])
        for g in range(G):
            p_ref[0, s, g] = acc[g * nch:(g + 1) * nch]


def _compress_weights(w_cmp1):
    w = w_cmp1.reshape(2, CMP_R, CMP_STRIDE, HEAD_DIM, CMP_HID)
    return w.transpose(0, 2, 3, 1, 4).reshape(2, CMP_STRIDE, HEAD_DIM, CMP_R * CMP_HID).astype(BF16)


def _compress_proj_prompt(rows, w1r):
    G, D = NSA_KV_HEADS, HEAD_DIM
    nch = SEQ // CMP_STRIDE
    pw = CMP_R * CMP_HID
    blk = SEQ * 2 * G * D * 4 + w1r.size * 2 + 2 * G * nch * pw * 4 + SEQ * 2 * G * D * 4
    return pl.pallas_call(
        functools.partial(_compress_proj_kernel, n_in=1, groups_per_in=SEQ // PERM_ROWS, lead=False),
        grid=(BATCH,),
        in_specs=[pl.BlockSpec((SEQ, 2 * G * D), lambda b: (b, 0)),
                  pl.BlockSpec(w1r.shape, lambda b: (0, 0, 0, 0))],
        out_specs=pl.BlockSpec((1, 2, G, nch, pw), lambda b: (b, 0, 0, 0, 0)),
        out_shape=jax.ShapeDtypeStruct((BATCH, 2, G, nch, pw), F32),
        scratch_shapes=[pltpu.VMEM((CMP_STRIDE, nch, 2 * G * D), F32)],
        compiler_params=_params(("parallel",), blk),
        name="compress_proj_prompt",
    )(rows, w1r)


def _compress_proj_sample(cache2d, page_table_flat, w1r):
    G, D = NSA_KV_HEADS, HEAD_DIM
    n_pages = PAST_LEN // PAGE_SIZE
    pps = SAMPLE_PAGES_PER_STEP
    assert n_pages % pps == 0
    cpi = PAGE_SIZE // CMP_STRIDE
    nch_step = pps * cpi
    pw = CMP_R * CMP_HID

    def page_map(b, j, pt, *, i):
        return (pt[b * n_pages + j * pps + i], 0, 0)

    in_specs = [pl.BlockSpec((1, PAGE_SIZE, 2 * G * D), functools.partial(page_map, i=i)) for i in range(pps)]
    in_specs.append(pl.BlockSpec(w1r.shape, lambda b, j, pt: (0, 0, 0, 0)))
    blk = 2 * pps * PAGE_SIZE * 2 * G * D * 4 + w1r.size * 2 + 2 * G * nch_step * pw * 4
    return pl.pallas_call(
        functools.partial(_compress_proj_kernel, n_in=pps, groups_per_in=PAGE_SIZE // PERM_ROWS, lead=True),
        grid_spec=pltpu.PrefetchScalarGridSpec(
            num_scalar_prefetch=1,
            grid=(DEC_BATCH, n_pages // pps),
            in_specs=in_specs,
            out_specs=pl.BlockSpec((1, 2, G, nch_step, pw), lambda b, j, pt: (b, 0, 0, j, 0)),
            scratch_shapes=[pltpu.VMEM((CMP_STRIDE, nch_step, 2 * G * D), F32)],
        ),
        out_shape=jax.ShapeDtypeStruct((DEC_BATCH, 2, G, n_pages * cpi, pw), F32),
        compiler_params=_params(("parallel", "arbitrary"), blk),
        name="compress_proj_sample",
    )(page_table_flat, *([cache2d] * pps), w1r)


def _compress_finish_kernel(p_ref, pe_ref, w1_ref, w2_ref, o_ref):
    nch = p_ref.shape[3]
    pe = jnp.broadcast_to(pe_ref[0], (SUBLANES, pe_ref.shape[2])).astype(BF16)
    pe_term = _dot(pe, w1_ref[0].astype(BF16))[0:1]
    w2 = w2_ref[0].astype(BF16)
    for g in range(NSA_KV_HEADS):
        p = p_ref[0, 0, g]
        nxt = pltpu.roll(p[:, CMP_HID:], shift=nch - 1, axis=0)
        hidden = p[:, :CMP_HID] + nxt + pe_term
        o_ref[0, 0, g] = _dot(jax.nn.gelu(hidden).astype(BF16), w2)


def _compress_finish(p, pe_cmp, w_cmp1, w_cmp2):
    nb, _, G, nch, pw = p.shape
    pe = pe_cmp.reshape(2, 1, CMP_LEN * HEAD_DIM)
    blk = G * nch * (pw + HEAD_DIM) * 4 + CMP_LEN * HEAD_DIM * (CMP_HID + 1) * 4
    return pl.pallas_call(
        _compress_finish_kernel,
        grid=(nb, 2),
        in_specs=[pl.BlockSpec((1, 1, G, nch, pw), lambda b, s: (b, s, 0, 0, 0)),
                  pl.BlockSpec((1, 1, CMP_LEN * HEAD_DIM), lambda b, s: (s, 0, 0)),
                  pl.BlockSpec((1, CMP_LEN * HEAD_DIM, CMP_HID), lambda b, s: (s, 0, 0)),
                  pl.BlockSpec((1, CMP_HID, HEAD_DIM), lambda b, s: (s, 0, 0))],
        out_specs=pl.BlockSpec((1, 1, G, nch, HEAD_DIM), lambda b, s: (b, s, 0, 0, 0)),
        out_shape=jax.ShapeDtypeStruct((nb, 2, G, nch, HEAD_DIM), F32),
        compiler_params=_params(("parallel", "parallel"), blk),
        name="compress_finish",
    )(p, pe, w_cmp1, w_cmp2)


def _selection_overlap(n_cmp, n_sel, rows, cols):
    start = np.arange(n_cmp)[:, None] * CMP_STRIDE
    end = start + CMP_LEN - 1
    j = np.arange(n_sel)[None, :]
    ov = ((start <= (j + 1) * SEL_LEN - 1) & (end >= j * SEL_LEN)).astype(np.float32)
    out = np.zeros((rows, cols), np.float32)
    out[:n_cmp, :n_sel] = ov
    return jnp.asarray(out)


def _masked_softmax(s, valid):
    sm = jnp.where(valid, s, NEG)
    e = jnp.exp(sm - jnp.max(sm, axis=-1, keepdims=True))
    return jnp.where(valid, e / jnp.sum(e, axis=-1, keepdims=True), 0.0)


def _flash_step(carry, s, valid, vb):
    m, l, acc = carry
    sm = jnp.where(valid, s, NEG)
    m_new = jnp.maximum(m, jnp.max(sm, axis=-1, keepdims=True))
    alpha = jnp.exp(m - m_new)
    e = jnp.exp(sm - m_new)
    l = alpha * l + jnp.sum(e, axis=-1, keepdims=True)
    acc = alpha * acc + _dot(jnp.where(valid, e, 0.0).astype(BF16), vb)
    return m_new, l, acc


def _flash_one(carry, s_new, v_new):
    m, l, acc = carry
    m_new = jnp.maximum(m, s_new)
    alpha = jnp.exp(m - m_new)
    e = jnp.exp(s_new - m_new)
    return m_new, alpha * l + e, alpha * acc + e * v_new


def _flash_init(rows):
    return (jnp.full((rows, 1), -jnp.inf, F32), jnp.zeros((rows, 1), F32), jnp.zeros((rows, HEAD_DIM), F32))


def _top_rank(impm_c, impm_r, jp_idx, j_idx, axis):
    before = (impm_c > impm_r) | ((impm_c == impm_r) & (jp_idx < j_idx))
    return jnp.sum(before.astype(jnp.int32), axis=axis, keepdims=True)


FEAT_SIG = 96
FEAT_SIG_FULL = FEAT_SIG + 9
MASK_BIG = 2.0 ** 100
POS_RADIX = 256
LOG2E = 1.4426950408889634
assert SEQ // SEL_LEN <= FEAT_SIG and SEQ >= WINDOW + TQ


def _key_features(pos, block):
    n = pos.shape[0]
    kf = np.zeros((n, LANES), np.float32)
    if block is not None:
        kf[np.arange(n), block] = -MASK_BIG
    kf[:, FEAT_SIG:FEAT_SIG + 3] = (pos % POS_RADIX)[:, None]
    kf[:, FEAT_SIG + 3:FEAT_SIG + 6] = (pos // POS_RADIX)[:, None]
    kf[:, FEAT_SIG + 6:FEAT_SIG + 9] = 1.0
    return jnp.asarray(kf, BF16)


def _sigma_table():
    slopes = np.exp2(-8.0 * np.arange(1, NSA_HEADS + 1, dtype=np.float32) / NSA_HEADS).astype(np.float32)
    sigma = (slopes / np.float32(HEAD_DIM ** -0.5)).astype(np.float32)
    bf = lambda a: np.asarray(np.asarray(a, dtype=BF16), dtype=np.float32)
    s1 = bf(sigma)
    s2 = bf(sigma - s1)
    s3 = bf(sigma - s1 - s2)
    tab = np.zeros((NSA_HEADS, LANES), np.float32)
    for i, s in enumerate((s1, s2, s3)):
        tab[:, FEAT_SIG + i] = s
        tab[:, FEAT_SIG + 3 + i] = s * POS_RADIX
    tab[:, FEAT_SIG_FULL] = sigma
    return jnp.asarray(tab)


def _nsa_prompt_kernel(zero_ref, q_ref, gl_ref, sig_ref, kc_ref, vc_ref, ks_ref, vs_ref, kw_ref, vw_ref,
                       ov_ref, kfc_ref, kfs_ref, kfw_ref, out_ref,
                       qa_scr, kca_scr, vcb_scr, ksa_scr, vsb_scr, kwa_scr, vwb_scr, m_scr, acc_scr):
    del zero_ref
    qi = pl.program_id(2)
    D = HEAD_DIM
    c2 = (D ** -0.5) * LOG2E
    n_sel = SEQ // SEL_LEN
    qs = qi * TQ
    tpos = qs + lax.broadcasted_iota(jnp.int32, (TQ, 1), 0)
    tposf = tpos.astype(F32)
    lane = lax.broadcasted_iota(jnp.int32, (1, LANES), 1)
    heads = [slice(r * TQ, (r + 1) * TQ) for r in range(GRP)]

    @pl.when(qi == 0)
    def _():
        kca_scr[:, 0:D] = kc_ref[0, 0, 0].astype(BF16)
        kca_scr[:, D:2 * D] = kfc_ref[...]
        vcb_scr[...] = vc_ref[0, 0, 0].astype(BF16)
        ksa_scr[:, 0:D] = ks_ref[...].astype(BF16)
        ksa_scr[:, D:2 * D] = kfs_ref[...]
        kwa_scr[:, 0:D] = kw_ref[...].astype(BF16)
        kwa_scr[:, D:2 * D] = kfw_ref[...]
        ones_col = jnp.where(lax.broadcasted_iota(jnp.int32, (SEQ, D), 1) == 0, 1.0, 0.0).astype(BF16)
        vsb_scr[:, 0:D] = vs_ref[...].astype(BF16)
        vsb_scr[:, D:2 * D] = ones_col
        vwb_scr[:, 0:D] = vw_ref[...].astype(BF16)
        vwb_scr[:, D:2 * D] = ones_col

    def write_features(nsel):
        for r in range(GRP):
            sig_row = sig_ref[r:r + 1, :]
            c = -sig_row[:, FEAT_SIG_FULL:FEAT_SIG_FULL + 1] * tposf
            c_hi = c.astype(BF16).astype(F32)
            c_mid = (c - c_hi).astype(BF16).astype(F32)
            c_lo = c - c_hi - c_mid
            f = jnp.where(lane == FEAT_SIG + 6, c_hi,
                          jnp.where(lane == FEAT_SIG + 7, c_mid, jnp.where(lane == FEAT_SIG + 8, c_lo, sig_row)))
            if nsel is not None:
                f = jnp.where(lane < n_sel, nsel, f)
            qa_scr[heads[r], D:2 * D] = f.astype(BF16)

    for r in range(GRP):
        qa_scr[heads[r], 0:D] = q_ref[:, r * D:(r + 1) * D]
    write_features(None)

    ncp = kca_scr.shape[0]
    cpos = lax.broadcasted_iota(jnp.int32, (1, ncp), 1) * CMP_STRIDE + (CMP_LEN - 1)
    valid_c = cpos <= tpos
    s_c = _dot_nt(qa_scr[...], kca_scr[...])
    p_sum = jnp.zeros((TQ, ncp), F32)
    p_c = []
    for r in range(GRP):
        sm = jnp.where(valid_c, s_c[heads[r]], NEG)
        e = jnp.exp2((sm - jnp.max(sm, axis=-1, keepdims=True)) * c2)
        p = jnp.where(valid_c, e / jnp.sum(e, axis=-1, keepdims=True), 0.0)
        p_c.append(p.astype(BF16))
        p_sum = p_sum + p
    o_c = _dot(jnp.concatenate(p_c, axis=0), vcb_scr[...])
    imp = _dot_exact(p_sum, ov_ref[...])

    cur = tpos // SEL_LEN
    forced = (lane == 0) | (lane == cur) | (lane == cur - 1)
    impm = jnp.where(forced, jnp.inf, jnp.where(lane <= cur, imp, -jnp.inf))
    imp_t = impm.T[0:n_sel]
    j_sub = lax.broadcasted_iota(jnp.int32, (n_sel, TQ), 0)
    rank_t = jnp.zeros((n_sel, TQ), jnp.int32)
    for jp in range(n_sel):
        cj = imp_t[jp:jp + 1, :]
        rank_t = rank_t + ((cj > imp_t) | ((cj == imp_t) & (jp < j_sub))).astype(jnp.int32)
    unsel_t = jnp.where(rank_t < min(SEL_TOP, n_sel), 0.0, 1.0)
    unsel = jnp.concatenate([unsel_t, jnp.ones((LANES - n_sel, TQ), F32)], axis=0).T
    write_features(jnp.where(lane <= cur, unsel, 1.0))

    m_scr[...] = jnp.full(m_scr.shape, -jnp.inf, F32)
    acc_scr[...] = jnp.zeros(acc_scr.shape, F32)

    def sel_tile(kt, bias):
        k0 = pl.multiple_of(kt * TKS, TKS)
        s_all = _dot_nt(qa_scr[...], ksa_scr[pl.ds(k0, TKS), :])
        p_all = []
        for r in range(GRP):
            s = s_all[heads[r]]
            if bias is not None:
                s = s + bias
            m_old = m_scr[heads[r]]
            m_new = jnp.maximum(m_old, jnp.max(s, axis=-1, keepdims=True))
            alpha = jnp.exp2((m_old - m_new) * c2)
            e = jnp.exp2((s - m_new) * c2)
            m_scr[heads[r]] = m_new
            acc_scr[heads[r]] = alpha * acc_scr[heads[r]]
            p_all.append(e.astype(BF16))
        acc_scr[...] += _dot(jnp.concatenate(p_all, axis=0), vsb_scr[pl.ds(k0, TKS), :])

    n_past = qs // TKS

    def past_tile(kt, carry):
        sel_tile(kt, None)
        return carry

    lax.fori_loop(0, n_past, past_tile, 0)
    spos_d = n_past * TKS + lax.broadcasted_iota(jnp.int32, (1, TKS), 1)
    sel_tile(n_past, jnp.where(spos_d <= tpos, 0.0, -MASK_BIG))
    o_s = acc_scr[:, 0:D] / acc_scr[:, D:D + 1]

    wk = WINDOW + TQ
    w0 = pl.multiple_of(jnp.maximum(qs - WINDOW, 0), TQ)
    dist_w = tpos - (w0 + lax.broadcasted_iota(jnp.int32, (1, wk), 1))
    bias_w = jnp.where((dist_w >= 0) & (dist_w <= WINDOW), 0.0, -MASK_BIG)
    s_w = _dot_nt(qa_scr[...], kwa_scr[pl.ds(w0, wk), :])
    p_w = []
    for r in range(GRP):
        s = s_w[heads[r]] + bias_w
        p_w.append(jnp.exp2((s - jnp.max(s, axis=-1, keepdims=True)) * c2).astype(BF16))
    pv_w = _dot(jnp.concatenate(p_w, axis=0), vwb_scr[pl.ds(w0, wk), :])
    o_w = pv_w[:, 0:D] / pv_w[:, D:D + 1]

    for r in range(GRP):
        gates = jax.nn.sigmoid(gl_ref[0, :, 3 * r:3 * r + 3])
        out = gates[:, 0:1] * o_c[heads[r]] + gates[:, 1:2] * o_s[heads[r]] + gates[:, 2:3] * o_w[heads[r]]
        out_ref[:, r * D:(r + 1) * D] = out.astype(out_ref.dtype)


def _nsa_prompt(q, gate_logits, kvc, rows, ov):
    G, D = NSA_KV_HEADS, HEAD_DIM
    nq = SEQ // TQ
    ncp = kvc.shape[3]
    R = GRP * TQ
    spos = np.arange(SEQ)
    kf_cmp = _key_features(np.arange(ncp) * CMP_STRIDE + (CMP_LEN - 1), None)
    kf_sel = _key_features(spos, spos // SEL_LEN)
    kf_win = _key_features(spos, None)
    row = lambda b, g, qi: b * nq + qi
    kv_spec = lambda slot: pl.BlockSpec((SEQ, D), lambda b, g, qi: (b, slot * G + g))
    const = lambda a: pl.BlockSpec(a.shape, lambda b, g, qi: (0, 0))
    blk = (TQ * GRP * D * 4 + 2 * ncp * D * 4 + 4 * SEQ * D * 4 + 3 * SEQ * D * 2 + SEQ * D * 6 + R * D * 16
           + R * (TKS + WINDOW + TQ) * 6)
    return pl.pallas_call(
        _nsa_prompt_kernel,
        grid=(BATCH, G, nq),
        in_specs=[pl.BlockSpec(memory_space=pl.ANY),
                  pl.BlockSpec((TQ, GRP * D), lambda b, g, qi: (row(b, g, qi), g)),
                  pl.BlockSpec((1, TQ, 3 * GRP), lambda b, g, qi: (g, row(b, g, qi), 0)),
                  pl.BlockSpec((GRP, LANES), lambda b, g, qi: (g, 0)),
                  pl.BlockSpec((1, 1, 1, ncp, D), lambda b, g, qi: (b, 0, g, 0, 0)),
                  pl.BlockSpec((1, 1, 1, ncp, D), lambda b, g, qi: (b, 1, g, 0, 0)),
                  kv_spec(2), kv_spec(3), kv_spec(4), kv_spec(5),
                  const(ov), const(kf_cmp), const(kf_sel), const(kf_win)],
        out_specs=pl.BlockSpec((TQ, GRP * D), lambda b, g, qi: (row(b, g, qi), g)),
        out_shape=jax.ShapeDtypeStruct((M_PAD, NSA_HEADS * D), BF16),
        scratch_shapes=[pltpu.VMEM((R, 2 * D), BF16),
                        pltpu.VMEM((ncp, 2 * D), BF16), pltpu.VMEM((ncp, D), BF16),
                        pltpu.VMEM((SEQ, 2 * D), BF16), pltpu.VMEM((SEQ, 2 * D), BF16),
                        pltpu.VMEM((SEQ, 2 * D), BF16), pltpu.VMEM((SEQ, 2 * D), BF16),
                        pltpu.VMEM((R, 1), F32), pltpu.VMEM((R, 2 * D), F32)],
        compiler_params=_params(("parallel", "parallel", "arbitrary"), blk),
        input_output_aliases={0: 0},
        name="nsa_prompt",
    )(jnp.zeros((M_PAD, NSA_HEADS * D), BF16), q, gate_logits, _sigma_table(), kvc, kvc, rows, rows, rows, rows,
      ov, kf_cmp, kf_sel, kf_win)


T_SAMPLE = PAST_LEN
N_SEL_S = -(-(PAST_LEN + 1) // SEL_LEN)
NB_PAST = PAST_LEN // SEL_LEN
KK_S = min(SEL_TOP, N_SEL_S)
SEL_PAD_S = _round_up(N_SEL_S, LANES)
assert PAST_LEN % SEL_LEN == 0 and PAST_LEN % CMP_STRIDE == 0


def _slope_col(slopes_ref, g):
    r_idx = lax.broadcasted_iota(jnp.int32, (GRP, 1), 0)
    col = jnp.zeros((GRP, 1), F32)
    for r in range(GRP):
        col = jnp.where(r_idx == r, slopes_ref[g * GRP + r], col)
    return col


def _nsa_sample_cmp_win_kernel(slopes_ref, q_ref, kc_ref, vc_ref, ov_ref, kw_ref, vw_ref, kwn_ref, vwn_ref,
                               oc_ref, ow_ref, idx_ref):
    g = pl.program_id(1)
    scale = HEAD_DIM ** -0.5
    q = q_ref[0, 0]
    slope = _slope_col(slopes_ref, g)

    kcb = kc_ref[0, 0, 0].astype(BF16)
    ncp = kcb.shape[0]
    cpos = lax.broadcasted_iota(jnp.int32, (1, ncp), 1) * CMP_STRIDE + (CMP_LEN - 1)
    dist_c = T_SAMPLE - cpos
    p_c = _masked_softmax(_dot_nt(q, kcb) * scale - slope * dist_c.astype(F32), dist_c >= 0)
    oc_ref[0, 0] = _dot(p_c.astype(BF16), vc_ref[0, 0, 0].astype(BF16))
    p_sum = jnp.broadcast_to(jnp.sum(p_c, axis=0, keepdims=True), (SUBLANES, ncp))
    imp = _dot_exact(p_sum, ov_ref[...])[0:1]

    NP = SEL_PAD_S
    cur = T_SAMPLE // SEL_LEN
    j_row = lax.broadcasted_iota(jnp.int32, (1, NP), 1)
    forced = (j_row == 0) | (j_row == cur) | (j_row == cur - 1)
    impm = jnp.where(forced, jnp.inf, jnp.where(j_row <= cur, imp, -jnp.inf))
    sub = lax.broadcasted_iota(jnp.int32, (NP, NP), 0)
    lane = lax.broadcasted_iota(jnp.int32, (NP, NP), 1)
    impm_r = jnp.broadcast_to(impm, (NP, NP))
    rank = _top_rank(impm_r.T, impm_r, sub, lane, 0).astype(F32)
    rank_c = jnp.broadcast_to(rank, (NP, NP)).T
    hit = (rank_c == lane.astype(F32)) & (sub <= cur)
    idx_ref[0, 0] = jnp.sum(jnp.where(hit, sub, 0), axis=0, keepdims=True)

    kwb = kw_ref[0].astype(BF16)
    wlen = kwb.shape[0]
    wpos = T_SAMPLE - wlen + lax.broadcasted_iota(jnp.int32, (1, wlen), 1)
    dist_w = T_SAMPLE - wpos
    valid_w = (dist_w >= 0) & (dist_w <= WINDOW) & (wpos >= 0)
    carry = _flash_step(_flash_init(GRP), _dot_nt(q, kwb) * scale - slope * dist_w.astype(F32), valid_w,
                        vw_ref[0].astype(BF16))
    k_new = kwn_ref[0, 0, 0].astype(BF16).astype(F32)
    v_new = vwn_ref[0, 0, 0].astype(BF16).astype(F32)
    s_new = jnp.sum(q.astype(F32) * k_new, axis=1, keepdims=True) * scale
    _, l_w, acc_w = _flash_one(carry, s_new, v_new)
    ow_ref[0, 0] = acc_w / l_w


def _nsa_sample_cmp_win(slopes, q_s, kvc, ov, win2d, rows_s):
    G, D, DB = NSA_KV_HEADS, HEAD_DIM, DEC_BATCH
    ncp = kvc.shape[3]
    wlen = win2d.shape[1]
    o_spec = pl.BlockSpec((1, 1, GRP, D), lambda b, g, sl: (b, g, 0, 0))
    blk = 2 * ncp * D * 4 + ov.size * 4 + 2 * wlen * D * 4 + 4 * SEL_PAD_S * SEL_PAD_S * 4
    return pl.pallas_call(
        _nsa_sample_cmp_win_kernel,
        grid_spec=pltpu.PrefetchScalarGridSpec(
            num_scalar_prefetch=1,
            grid=(DB, G),
            in_specs=[pl.BlockSpec((1, 1, GRP, D), lambda b, g, sl: (b, g, 0, 0)),
                      pl.BlockSpec((1, 1, 1, ncp, D), lambda b, g, sl: (b, 0, g, 0, 0)),
                      pl.BlockSpec((1, 1, 1, ncp, D), lambda b, g, sl: (b, 1, g, 0, 0)),
                      pl.BlockSpec(ov.shape, lambda b, g, sl: (0, 0)),
                      pl.BlockSpec((1, wlen, D), lambda b, g, sl: (b, 0, g)),
                      pl.BlockSpec((1, wlen, D), lambda b, g, sl: (b, 0, G + g)),
                      pl.BlockSpec((1, 1, 1, 1, D), lambda b, g, sl: (b, 4, g, 0, 0)),
                      pl.BlockSpec((1, 1, 1, 1, D), lambda b, g, sl: (b, 5, g, 0, 0))],
            out_specs=[o_spec, o_spec, pl.BlockSpec((1, 1, 1, SEL_PAD_S), lambda b, g, sl: (b, g, 0, 0))],
        ),
        out_shape=[jax.ShapeDtypeStruct((DB, G, GRP, D), F32),
                   jax.ShapeDtypeStruct((DB, G, GRP, D), F32),
                   jax.ShapeDtypeStruct((DB, G, 1, SEL_PAD_S), jnp.int32)],
        compiler_params=_params(("parallel", "parallel"), blk),
        name="nsa_sample_cmp_win",
    )(slopes, q_s, kvc, kvc, ov, win2d, win2d, rows_s, rows_s)


def _nsa_sample_sel_kernel(pt_ref, idx_ref, slopes_ref, q_ref, *refs):
    ks_refs, vs_refs = refs[:KK_S], refs[KK_S:2 * KK_S]
    ksn_ref, vsn_ref, oc_ref, ow_ref, gl_ref, out_ref = refs[2 * KK_S:]
    b, g = pl.program_id(0), pl.program_id(1)
    scale = HEAD_DIM ** -0.5
    q = q_ref[0, 0]
    slope = _slope_col(slopes_ref, g)
    off = lax.broadcasted_iota(jnp.int32, (1, SEL_LEN), 1)

    scores, valids = [], []
    has_new = False
    for kk in range(KK_S):
        j = idx_ref[(b * NSA_KV_HEADS + g) * KK_S + kk]
        in_cache = j < NB_PAST
        has_new = jnp.logical_or(has_new, jnp.logical_not(in_cache))
        dist = T_SAMPLE - (jnp.minimum(j, NB_PAST - 1) * SEL_LEN + off)
        valid = (dist >= 0) & in_cache
        s = _dot_nt(q, ks_refs[kk][0].astype(BF16)) * scale - slope * dist.astype(F32)
        scores.append(jnp.where(valid, s, NEG))
        valids.append(valid)
    k_new = ksn_ref[0, 0, 0].astype(BF16).astype(F32)
    v_new = vsn_ref[0, 0, 0].astype(BF16).astype(F32)
    s_new = jnp.sum(q.astype(F32) * k_new, axis=1, keepdims=True) * scale
    s_new = jnp.where(has_new, s_new, NEG)

    m = s_new
    for s in scores:
        m = jnp.maximum(m, jnp.max(s, axis=-1, keepdims=True))
    e_new = jnp.exp(s_new - m)
    l = e_new
    acc = jnp.where(has_new, e_new, 0.0) * v_new
    for kk in range(KK_S):
        e = jnp.exp(scores[kk] - m)
        l = l + jnp.sum(e, axis=-1, keepdims=True)
        acc = acc + _dot(jnp.where(valids[kk], e, 0.0).astype(BF16), vs_refs[kk][0].astype(BF16))
    gates = jax.nn.sigmoid(gl_ref[0, 0])
    out_ref[0, 0] = gates[0] * oc_ref[0, 0] + gates[1] * (acc / l) + gates[2] * ow_ref[0, 0]


def _nsa_sample_sel(page_table_flat, idx_flat, slopes, q_s, cache2d, rows_s, o_c, o_w, gate_logits_s):
    G, D, DB = NSA_KV_HEADS, HEAD_DIM, DEC_BATCH
    n_pages = PAST_LEN // PAGE_SIZE
    bpp = PAGE_SIZE // SEL_LEN

    def blk_map(b, g, pt, idx, sl, *, kk, slot):
        j = jnp.minimum(idx[(b * G + g) * KK_S + kk], NB_PAST - 1)
        return (pt[b * n_pages + j // bpp], j % bpp, slot * G + g)

    o_spec = pl.BlockSpec((1, 1, GRP, D), lambda b, g, pt, idx, sl: (b, g, 0, 0))
    blk_specs = [pl.BlockSpec((1, SEL_LEN, D), functools.partial(blk_map, kk=kk, slot=slot))
                 for slot in (2, 3) for kk in range(KK_S)]
    blk = 2 * KK_S * SEL_LEN * D * 4 + 6 * GRP * D * 4
    return pl.pallas_call(
        _nsa_sample_sel_kernel,
        grid_spec=pltpu.PrefetchScalarGridSpec(
            num_scalar_prefetch=3,
            grid=(DB, G),
            in_specs=[o_spec, *blk_specs,
                      pl.BlockSpec((1, 1, 1, 1, D), lambda b, g, pt, idx, sl: (b, 2, g, 0, 0)),
                      pl.BlockSpec((1, 1, 1, 1, D), lambda b, g, pt, idx, sl: (b, 3, g, 0, 0)),
                      o_spec, o_spec,
                      pl.BlockSpec((1, 1, 3, GRP, 1), lambda b, g, pt, idx, sl: (b, g, 0, 0, 0))],
            out_specs=o_spec,
        ),
        out_shape=jax.ShapeDtypeStruct((DB, G, GRP, D), F32),
        compiler_params=_params(("parallel", "parallel"), blk),
        name="nsa_sample_sel",
    )(page_table_flat, idx_flat, slopes, q_s, *([cache2d] * (2 * KK_S)), rows_s, rows_s, o_c, o_w, gate_logits_s)


def _stack_rows(prompt_rows, sample_rows):
    pad = jnp.zeros((M_PAD - M_PROMPT - M_SAMPLE, prompt_rows.shape[1]), prompt_rows.dtype)
    return jnp.concatenate([prompt_rows, sample_rows.astype(prompt_rows.dtype), pad], axis=0)


def _tail_rows_kernel(buf_ref, s_ref, o_ref):
    del buf_ref
    ns = s_ref.shape[0]
    o_ref[0:ns, :] = s_ref[...]
    o_ref[ns:, :] = jnp.zeros((o_ref.shape[0] - ns, o_ref.shape[1]), o_ref.dtype)


def _fill_tail_rows(buf, sample_rows):
    tail = M_PAD - M_PROMPT
    ns = _round_up(M_SAMPLE, 16)
    assert M_PROMPT % tail == 0 and ns <= tail and buf.dtype == BF16
    s = jnp.pad(sample_rows.astype(BF16), ((0, ns - M_SAMPLE), (0, 0)))
    cols = buf.shape[1]
    return pl.pallas_call(
        _tail_rows_kernel,
        grid=(1,),
        in_specs=[pl.BlockSpec(memory_space=pl.ANY), pl.BlockSpec((ns, cols), lambda i: (0, 0))],
        out_specs=pl.BlockSpec((tail, cols), lambda i: (M_PROMPT // tail, 0)),
        out_shape=jax.ShapeDtypeStruct(buf.shape, buf.dtype),
        input_output_aliases={0: 0},
        compiler_params=_params(("arbitrary",), (tail + ns) * cols * 2),
        name="fill_tail_rows",
    )(buf, s)


def _ffn_block(x, g, w_gate_up, w_down, layer):
    h = _rmsnorm(x, g, BF16)
    return _matmul_fullk(_ffn_up(h, w_gate_up, layer), w_down, layer, D_MODEL, F32, residual=x, k_split=2)


def kernel(x_prompt, x_sample, cache_nsa_kv, page_table, state_win_kv, state_mlstm_C, state_mlstm_n, state_mlstm_m,
           g_mix, g_ffn, g_kv, g_final, w_in_a, b_gate_a, g_head_a, w_out_a, w_qg_b, w_out_b, w_kv,
           pe_cmp, w_cmp1, w_cmp2, w_gate_up, w_down):
    H, DK, DV = MLSTM_HEADS, MLSTM_QK_DIM, MLSTM_V_DIM
    G, D, DB = NSA_KV_HEADS, HEAD_DIM, DEC_BATCH
    hd = NSA_HEADS * D
    n_main = 2 * H * DK + 2 * H * DV
    x = _stack_rows(x_prompt.reshape(M_PROMPT, D_MODEL), x_sample.reshape(M_SAMPLE, D_MODEL))
    w_in_t = jnp.swapaxes(w_in_a, 1, 2)
    w_qg_t = jnp.swapaxes(w_qg_b, 1, 2)
    w_gate_a = jnp.swapaxes(w_in_t[:, n_main:, :], 1, 2)
    w_gate_b = jnp.swapaxes(w_qg_t[:, hd:, :], 1, 2)

    c_p, n_p, m_p, c_s, n_s, m_s = [], [], [], [], [], []
    for l in range(N_A):
        h, gates = _rmsnorm_with_gates(x, g_mix[l], w_gate_a[l], b_gate_a[l], True)
        z = _matmul_fullk_nt(h, w_in_t, l, n_main, F32)
        gates_t = gates[:M_PROMPT, :2 * H].T.reshape(2, H, 1, M_PROMPT)
        hs_p, cp, np_, mp = _mlstm_prompt(z, gates_t, g_head_a[l])
        gs = gates[S0:S0 + DB, :2 * H]
        lane_b = lambda a: jnp.broadcast_to(a[:, :, None, None], (DB, H, 1, LANES))
        hs_s, cs, ns, ms = _mlstm_sample(z, lane_b(gs[:, :H]), lane_b(gs[:, H:]), lane_b(state_mlstm_m[l]),
                                         state_mlstm_C[l], state_mlstm_n[l].reshape(DB, H, 1, DK), g_head_a[l])
        hs = _fill_tail_rows(hs_p, hs_s.reshape(DB, H * DV))
        x = _matmul_fullk(hs, w_out_a, l, D_MODEL, F32, residual=x)
        x = _ffn_block(x, g_ffn[l], w_gate_up, w_down, l)
        c_p.append(cp)
        n_p.append(np_.reshape(BATCH, H, DK))
        m_p.append(mp.reshape(BATCH, H))
        c_s.append(cs)
        n_s.append(ns.reshape(DB, H, DK))
        m_s.append(ms[:, :, 0, 0])

    rows = _matmul_fullk(_rmsnorm(x, g_kv, BF16), w_kv[None], 0, 6 * G * D, F32)
    rows_s = rows[S0:S0 + DB].reshape(DB, 6, G, 1, D)
    w1r = _compress_weights(w_cmp1)
    kvc_p = _compress_finish(_compress_proj_prompt(rows, w1r), pe_cmp, w_cmp1, w_cmp2)
    n_pool = cache_nsa_kv.shape[0]
    pt_flat = page_table.reshape(-1).astype(jnp.int32)
    cache2d = cache_nsa_kv.reshape(n_pool, PAGE_SIZE, 4 * G * D)
    kvc_s = _compress_finish(_compress_proj_sample(cache2d, pt_flat, w1r), pe_cmp, w_cmp1, w_cmp2)
    win2d = state_win_kv.reshape(DB, state_win_kv.shape[1], 2 * G * D)

    slopes = jnp.exp2(-8.0 * jnp.arange(1, NSA_HEADS + 1, dtype=F32) / NSA_HEADS)
    n_chunks_p = SEQ // CMP_STRIDE
    ov_p = _selection_overlap(n_chunks_p - CMP_R + 1, SEQ // SEL_LEN, n_chunks_p, LANES)
    n_chunks_s = (PAST_LEN + M_SAMPLE // DB) // CMP_STRIDE
    ov_s = _selection_overlap(n_chunks_s - CMP_R + 1, N_SEL_S, n_chunks_s, SEL_PAD_S)

    for l in range(N_B):
        h, gate_logits = _rmsnorm_with_gates(x, g_mix[N_A + l], w_gate_b[l], jnp.zeros((3 * NSA_HEADS,), F32),
                                             False)
        q = _matmul_fullk_nt(h, w_qg_t, l, hd, BF16)
        gl = gate_logits[:, :3 * NSA_HEADS]
        gl_p = gl.reshape(M_PAD, G, 3 * GRP).transpose(1, 0, 2)
        att_p = _nsa_prompt(q, gl_p, kvc_p, rows, ov_p)
        q_s = q[S0:S0 + DB].reshape(DB, G, GRP, D)
        gl_s = gl[S0:S0 + DB].reshape(DB, G, GRP, 3).transpose(0, 1, 3, 2)[..., None]
        o_c, o_w, idx = _nsa_sample_cmp_win(slopes, q_s, kvc_s, ov_s, win2d, rows_s)
        idx_flat = idx[:, :, 0, :KK_S].reshape(-1)
        att_s = _nsa_sample_sel(pt_flat, idx_flat, slopes, q_s, cache2d, rows_s, o_c, o_w, gl_s)
        att = _fill_tail_rows(att_p, att_s.reshape(DB, hd))
        x = _matmul_fullk(att, w_out_b, l, D_MODEL, F32, residual=x)
        x = _ffn_block(x, g_ffn[N_A + l], w_gate_up, w_down, N_A + l)

    y_p = _rmsnorm(x, g_final, F32, n_rows=M_PROMPT)
    y_s = _rmsnorm(x[S0:], g_final, F32)[:DB]
    wp = min(WINDOW, SEQ)
    rows_p = rows[:M_PROMPT].reshape(BATCH, SEQ, 6, G, D)
    rows_s6 = rows[S0:S0 + DB].reshape(DB, 1, 6, G, D)
    wbuf = state_win_kv.shape[1]
    win_kv_sample = jnp.concatenate([state_win_kv, rows_s6[:, :, 4:]], axis=1)[:, -wbuf:]
    return (y_p.reshape(BATCH, SEQ, D_MODEL), y_s.reshape(DB, 1, D_MODEL),
            rows_p[:, :, :4], rows_p[:, SEQ - wp:, 4:],
            jnp.stack(c_p), jnp.stack(n_p), jnp.stack(m_p),
            rows_s6[:, :, :4], win_kv_sample,
            jnp.stack(c_s), jnp.stack(n_s), jnp.stack(m_s))
```

```python
import functools

import numpy as np
import jax
import jax.numpy as jnp
from jax import lax
from jax.experimental import pallas as pl
from jax.experimental.pallas import tpu as pltpu

D_MODEL = 4096
BATCH = 4
SEQ = 2048
DEPTH = 4
DEC_BATCH = 8
DEC_SEQ = 1
PAST_LEN = 8192
PAGE_SIZE = 128

N_A = DEPTH // 2
N_B = DEPTH - N_A
MLSTM_HEADS = 8
MLSTM_QK_DIM = D_MODEL // 16
MLSTM_V_DIM = D_MODEL // MLSTM_HEADS
GATE_CAP = 15.0
HEAD_DIM = 128
NSA_HEADS = D_MODEL // HEAD_DIM
NSA_KV_HEADS = 4
GRP = NSA_HEADS // NSA_KV_HEADS
CMP_LEN = 32
CMP_STRIDE = 16
CMP_HID = HEAD_DIM
SEL_LEN = 64
SEL_TOP = 16
WINDOW = 512
D_FF = -(-8 * D_MODEL // (3 * 256)) * 256
EPS = 1e-6
NEG = -1e30

F32 = jnp.float32
BF16 = jnp.bfloat16

LANES = 128
SUBLANES = 8
VMEM_BYTES_V7X = 64 * 1024 * 1024


def _round_up(n, m):
    return -(-n // m) * m


def _pick_tile(n, cap, align=LANES):
    best = None
    for t in range(align, min(n, cap) + 1, align):
        if n % t == 0:
            best = t
    assert best is not None, (n, cap, align)
    return best


M_PROMPT = BATCH * SEQ
M_SAMPLE = DEC_BATCH * DEC_SEQ
S0 = M_PROMPT
M_PAD = _round_up(M_PROMPT + M_SAMPLE, LANES)
TM = _pick_tile(M_PAD, 2048)
TM_WIDE = _pick_tile(M_PAD, 2304, align=16)
ROW_TILE_BYTES_MAX = 18 << 20
TR = _pick_tile(M_PAD, 384, align=64)
TN_FULLK = 256
MLSTM_L = 256
TQ = 128
TKS = 1024
N_GATE_PAD = LANES
D_FF_PAD = _round_up(D_FF, 1024)
CMP_R = CMP_LEN // CMP_STRIDE
SAMPLE_PAGES_PER_STEP = 8
PERM_ROWS = LANES

assert DEC_SEQ == 1 and S0 % 16 == 0
assert SEQ % PERM_ROWS == 0 and PAGE_SIZE % PERM_ROWS == 0
assert (PERM_ROWS // CMP_STRIDE) & (PERM_ROWS // CMP_STRIDE - 1) == 0
assert SEQ % MLSTM_L == 0 and SEQ % TQ == 0 and SEQ % TKS == 0 and TKS % TQ == 0
assert PAST_LEN % PAGE_SIZE == 0 and PAGE_SIZE % SEL_LEN == 0 and PAGE_SIZE % CMP_STRIDE == 0
assert CMP_R == 2 and SEQ // SEL_LEN <= LANES


def _vmem_limit(block_bytes, single_bytes):
    return int(min(single_bytes + 2 * block_bytes + (12 << 20), VMEM_BYTES_V7X - (8 << 20)))


def _params(sem, block_bytes, single_bytes=0):
    return pltpu.CompilerParams(dimension_semantics=sem,
                                vmem_limit_bytes=_vmem_limit(block_bytes, single_bytes))


def _dot(a, b):
    return jnp.dot(a, b, preferred_element_type=F32)


def _dot_nt(a, b):
    return lax.dot_general(a, b, (((1,), (1,)), ((), ())), preferred_element_type=F32)


def _dot_tn(a, b):
    return lax.dot_general(a, b, (((0,), (0,)), ((), ())), preferred_element_type=F32)


def _dot_exact(a, b):
    return jnp.dot(a, b, precision=lax.Precision.HIGHEST, preferred_element_type=F32)


def _dot_split3(a, b):
    a_hi = a.astype(BF16)
    a_lo = (a - a_hi.astype(F32)).astype(BF16)
    b_hi = b.astype(BF16)
    b_lo = (b - b_hi.astype(F32)).astype(BF16)
    return _dot(a_hi, b_hi) + (_dot(a_hi, b_lo) + _dot(a_lo, b_hi))


def _log_sigmoid(x):
    return jnp.minimum(x, 0.0) - jnp.log1p(jnp.exp(-jnp.abs(x)))


def _col_form(row, n):
    return jnp.broadcast_to(row, (n, n)).T


def _rms_kernel(x_ref, g_ref, o_ref):
    x = x_ref[...]
    y = x * lax.rsqrt(jnp.mean(x * x, axis=-1, keepdims=True) + EPS) * g_ref[...]
    o_ref[...] = y.astype(o_ref.dtype)


def _rmsnorm(x, g, out_dtype, n_rows=None):
    d = x.shape[1]
    m = x.shape[0] if n_rows is None else n_rows
    tr = _pick_tile(m, TR, align=SUBLANES * 8)
    blk = tr * d * (8 + jnp.dtype(out_dtype).itemsize)
    return pl.pallas_call(
        _rms_kernel,
        grid=(m // tr,),
        in_specs=[pl.BlockSpec((tr, d), lambda i: (i, 0)), pl.BlockSpec((1, d), lambda i: (0, 0))],
        out_specs=pl.BlockSpec((tr, d), lambda i: (i, 0)),
        out_shape=jax.ShapeDtypeStruct((m, d), out_dtype),
        compiler_params=_params(("parallel",), blk),
        name="rmsnorm",
    )(x, g.reshape(1, d))


def _rms_gate_kernel(x_ref, g_ref, w_ref, b_ref, h_ref, o_ref, *, mlstm_gates):
    x = x_ref[...]
    y = x * lax.rsqrt(jnp.mean(x * x, axis=-1, keepdims=True) + EPS) * g_ref[...]
    h_ref[...] = y.astype(h_ref.dtype)
    z = _dot_split3(y, w_ref[...]) + b_ref[...]
    if mlstm_gates:
        gl = GATE_CAP * jnp.tanh(z / GATE_CAP)
        lane = lax.broadcasted_iota(jnp.int32, z.shape, 1)
        z = jnp.where(lane < MLSTM_HEADS, gl, _log_sigmoid(gl))
    o_ref[...] = z


def _rmsnorm_with_gates(x, g, w_gate, b_gate, mlstm_gates):
    m, d = x.shape
    ng = w_gate.shape[1]
    wp = jnp.pad(w_gate, ((0, 0), (0, N_GATE_PAD - ng)))
    bp = jnp.pad(b_gate, (0, N_GATE_PAD - ng)).reshape(1, N_GATE_PAD)
    blk = TR * d * 10 + d * N_GATE_PAD * 6 + TR * N_GATE_PAD * 4
    return pl.pallas_call(
        functools.partial(_rms_gate_kernel, mlstm_gates=mlstm_gates),
        grid=(m // TR,),
        in_specs=[pl.BlockSpec((TR, d), lambda i: (i, 0)),
                  pl.BlockSpec((1, d), lambda i: (0, 0)),
                  pl.BlockSpec((d, N_GATE_PAD), lambda i: (0, 0)),
                  pl.BlockSpec((1, N_GATE_PAD), lambda i: (0, 0))],
        out_specs=[pl.BlockSpec((TR, d), lambda i: (i, 0)),
                   pl.BlockSpec((TR, N_GATE_PAD), lambda i: (i, 0))],
        out_shape=[jax.ShapeDtypeStruct((m, d), BF16), jax.ShapeDtypeStruct((m, N_GATE_PAD), F32)],
        compiler_params=_params(("parallel",), blk),
        name="rmsnorm_gates",
    )(x, g.reshape(1, d), wp, bp)


def _mm_fullk_kernel(*refs, has_res, rows_valid):
    x_ref, w_ref, o_ref = refs[0], refs[1], refs[-1]
    w = w_ref[...]
    if rows_valid < w.shape[0]:
        w = jnp.where(lax.broadcasted_iota(jnp.int32, w.shape, 0) < rows_valid, w, 0.0)
    r = _dot(x_ref[...], w.astype(BF16))
    if has_res:
        r = r + refs[2][...]
    o_ref[...] = r.astype(o_ref.dtype)


def _row_tile_rows(kb):
    return TM_WIDE if TM_WIDE * kb * 2 <= ROW_TILE_BYTES_MAX else TM


def _row_tile_spec(tm, kb, kh=0):
    return pl.BlockSpec((tm, kb), lambda i, j: (i, kh), pipeline_mode=pl.Buffered(1))


def _matmul_fullk(x, w, layer, n_cols, out_dtype, residual=None, k_split=1):
    m, kd = x.shape
    tn = TN_FULLK
    kb = kd // k_split
    tm = _row_tile_rows(kb)
    k_rows = w.shape[1]
    assert n_cols % tn == 0 and kd % k_split == 0 and kb % LANES == 0 and w.dtype == F32
    assert (k_split - 1) * kb < k_rows <= kd
    out = residual
    for kh in range(k_split):
        last = kh == k_split - 1
        dt = out_dtype if last else F32
        osz = jnp.dtype(dt).itemsize
        in_specs = [_row_tile_spec(tm, kb, kh), pl.BlockSpec((None, kb, tn), lambda i, j, kh=kh: (layer, kh, j))]
        args = [x, w]
        streamed = kb * tn * 4 + tm * tn * osz
        if out is not None:
            in_specs.append(pl.BlockSpec((tm, tn), lambda i, j: (i, j)))
            args.append(out)
            streamed += tm * tn * 4
        out = pl.pallas_call(
            functools.partial(_mm_fullk_kernel, has_res=out is not None, rows_valid=min(kb, k_rows - kh * kb)),
            grid=(m // tm, n_cols // tn),
            in_specs=in_specs,
            out_specs=pl.BlockSpec((tm, tn), lambda i, j: (i, j)),
            out_shape=jax.ShapeDtypeStruct((m, n_cols), dt),
            compiler_params=_params(("parallel", "arbitrary"), streamed, single_bytes=tm * kb * 2),
            name="matmul_fullk",
        )(*args)
    return out


def _mm_fullk_nt_kernel(x_ref, wt_ref, o_ref):
    o_ref[...] = _dot_nt(x_ref[...], wt_ref[...].astype(BF16)).astype(o_ref.dtype)


def _matmul_fullk_nt(x, wt, layer, n_cols, out_dtype):
    m, kd = x.shape
    tn = 2 * TN_FULLK
    tm = _row_tile_rows(kd)
    assert n_cols % tn == 0 and wt.shape[2] == kd and wt.dtype == F32
    streamed = kd * tn * 4 + tm * tn * jnp.dtype(out_dtype).itemsize
    return pl.pallas_call(
        _mm_fullk_nt_kernel,
        grid=(m // tm, n_cols // tn),
        in_specs=[_row_tile_spec(tm, kd), pl.BlockSpec((None, tn, kd), lambda i, j: (layer, j, 0))],
        out_specs=pl.BlockSpec((tm, tn), lambda i, j: (i, j)),
        out_shape=jax.ShapeDtypeStruct((m, n_cols), out_dtype),
        compiler_params=_params(("parallel", "arbitrary"), streamed, single_bytes=tm * kd * 2),
        name="matmul_fullk_nt",
    )(x, wt)


def _ffn_up_kernel(x_ref, wg_ref, wu_ref, o_ref, *, n_real):
    j = pl.program_id(1)

    @pl.when(j < n_real)
    def _():
        x = x_ref[...]
        gate = _dot(x, wg_ref[...].astype(BF16))
        up = _dot(x, wu_ref[...].astype(BF16))
        o_ref[...] = (jax.nn.silu(gate) * up).astype(o_ref.dtype)

    @pl.when(j >= n_real)
    def _():
        o_ref[...] = jnp.zeros_like(o_ref)


def _ffn_up(x, w_gate_up, layer):
    m, kd = x.shape
    tn = TN_FULLK
    tm = _row_tile_rows(kd)
    assert D_FF % tn == 0 and D_FF_PAD % tn == 0
    n_real = D_FF // tn
    col = lambda j: jnp.minimum(j, n_real - 1)
    streamed = 2 * kd * tn * 4 + tm * tn * 2
    return pl.pallas_call(
        functools.partial(_ffn_up_kernel, n_real=n_real),
        grid=(m // tm, D_FF_PAD // tn),
        in_specs=[_row_tile_spec(tm, kd),
                  pl.BlockSpec((None, kd, tn), lambda i, j: (layer, 0, col(j))),
                  pl.BlockSpec((None, kd, tn), lambda i, j: (layer, 0, n_real + col(j)))],
        out_specs=pl.BlockSpec((tm, tn), lambda i, j: (i, j)),
        out_shape=jax.ShapeDtypeStruct((m, D_FF_PAD), BF16),
        compiler_params=_params(("parallel", "arbitrary"), streamed, single_bytes=tm * kd * 2),
        name="ffn_up",
    )(x, w_gate_up, w_gate_up)


def _mlstm_prompt_kernel(zero_ref, q_ref, k_ref, v_ref, o_ref, g_ref, gh_ref, hs_ref, c_ref, n_ref, m_ref):
    del zero_ref
    L, DK, DV = MLSTM_L, MLSTM_QK_DIM, MLSTM_V_DIM

    @pl.when(pl.program_id(1) == 0)
    def _():
        c_ref[...] = jnp.zeros_like(c_ref)
        n_ref[...] = jnp.zeros_like(n_ref)
        m_ref[...] = jnp.zeros_like(m_ref)

    t_idx = lax.broadcasted_iota(jnp.int32, (L, L), 0)
    s_idx = lax.broadcasted_iota(jnp.int32, (L, L), 1)
    for hd in range(MLSTM_HEADS):
        ig_row = g_ref[0, hd]
        lf_row = g_ref[1, hd]
        m_prev = m_ref[0, hd]
        lf_c = _col_form(lf_row, L)
        b_row = jnp.sum(jnp.where(t_idx <= s_idx, lf_c, 0.0), axis=0, keepdims=True)
        b_r = jnp.broadcast_to(b_row, (L, L))
        b_c = b_r.T
        ig_r = jnp.broadcast_to(ig_row, (L, L))
        log_d = jnp.where(s_idx <= t_idx, b_c - b_r + ig_r, -jnp.inf)
        b_col = b_c[:, 0:1]
        inter = b_col + m_prev
        m_t = jnp.maximum(jnp.max(log_d, axis=1, keepdims=True), inter)
        d = jnp.exp(log_d - m_t)
        w_inter = jnp.exp(inter - m_t)

        q = q_ref[:, hd * DK:(hd + 1) * DK] * (DK ** -0.5)
        k = k_ref[:, hd * DK:(hd + 1) * DK]
        qb = q.astype(BF16)
        kb = k.astype(BF16)
        vb = v_ref[:, hd * DV:(hd + 1) * DV].astype(BF16)
        s = _dot_nt(qb, kb) * d
        c_prev = c_ref[0, hd]
        n_prev = n_ref[0, hd]
        num = w_inter * _dot(qb, c_prev.astype(BF16)) + _dot(s.astype(BF16), vb)
        den = w_inter * jnp.sum(q * n_prev, axis=1, keepdims=True) + jnp.sum(s, axis=1, keepdims=True)
        h = num / jnp.maximum(jnp.abs(den), jnp.exp(-m_t))
        hn = h * lax.rsqrt(jnp.mean(h * h, axis=1, keepdims=True) + EPS)
        cols = slice(hd * DV, (hd + 1) * DV)
        hs_ref[:, cols] = (hn * gh_ref[:, cols] * jax.nn.sigmoid(o_ref[:, cols])).astype(hs_ref.dtype)

        m_new = m_t[L - 1:L, :]
        b_last = b_row[:, L - 1:L]
        ig_col = ig_r.T[:, 0:1]
        decay = jnp.exp(b_last - b_col + ig_col - m_new)
        carry = jnp.exp(b_last + m_prev - m_new)
        kd = k * decay
        c_ref[0, hd] = carry * c_prev + _dot_tn(kd.astype(BF16), vb)
        n_ref[0, hd] = carry * n_prev + jnp.sum(kd, axis=0, keepdims=True)
        m_ref[0, hd] = m_new


def _mlstm_prompt(z, gates_t, g_head):
    H, DK, DV, L = MLSTM_HEADS, MLSTM_QK_DIM, MLSTM_V_DIM, MLSTM_L
    nc = SEQ // L
    assert (H * DV) % (H * DK) == 0
    v0 = (2 * H * DK) // (H * DV)
    row = lambda b, c: b * nc + c
    blk = L * (2 * H * DK + 2 * H * DV) * 4 + L * H * DV * 2 + H * DK * DV * 4
    return pl.pallas_call(
        _mlstm_prompt_kernel,
        grid=(BATCH, nc),
        in_specs=[pl.BlockSpec(memory_space=pl.ANY),
                  pl.BlockSpec((L, H * DK), lambda b, c: (row(b, c), 0)),
                  pl.BlockSpec((L, H * DK), lambda b, c: (row(b, c), 1)),
                  pl.BlockSpec((L, H * DV), lambda b, c: (row(b, c), v0)),
                  pl.BlockSpec((L, H * DV), lambda b, c: (row(b, c), v0 + 1)),
                  pl.BlockSpec((2, H, 1, L), lambda b, c: (0, 0, 0, row(b, c))),
                  pl.BlockSpec((1, H * DV), lambda b, c: (0, 0))],
        out_specs=[pl.BlockSpec((L, H * DV), lambda b, c: (row(b, c), 0)),
                   pl.BlockSpec((1, H, DK, DV), lambda b, c: (b, 0, 0, 0)),
                   pl.BlockSpec((1, H, 1, DK), lambda b, c: (b, 0, 0, 0)),
                   pl.BlockSpec((1, H, 1, 1), lambda b, c: (b, 0, 0, 0))],
        out_shape=[jax.ShapeDtypeStruct((M_PAD, H * DV), BF16),
                   jax.ShapeDtypeStruct((BATCH, H, DK, DV), F32),
                   jax.ShapeDtypeStruct((BATCH, H, 1, DK), F32),
                   jax.ShapeDtypeStruct((BATCH, H, 1, 1), F32)],
        compiler_params=_params(("parallel", "arbitrary"), blk),
        input_output_aliases={0: 0},
        name="mlstm_prompt",
    )(jnp.zeros((M_PAD, H * DV), BF16), z, z, z, z, gates_t, g_head.reshape(1, H * DV))


def _mlstm_sample_kernel(q_ref, k_ref, v_ref, o_ref, ig_ref, lf_ref, m0_ref, c0_ref, n0_ref, gh_ref,
                         hs_ref, c_ref, n_ref, m_ref):
    DK = MLSTM_QK_DIM
    b = pl.program_id(0)
    q = q_ref[pl.ds(b, 1), :] * (DK ** -0.5)
    k = k_ref[pl.ds(b, 1), :]
    v = v_ref[pl.ds(b, 1), :]
    og = o_ref[pl.ds(b, 1), :]
    ig = ig_ref[0, 0]
    lf = lf_ref[0, 0]
    m0 = m0_ref[0, 0]
    m_t = jnp.maximum(ig, lf + m0)
    d = jnp.exp(ig - m_t)[:, 0:1]
    w = jnp.exp(lf + m0 - m_t)[:, 0:1]
    c0 = c0_ref[0, 0]
    n0 = n0_ref[0, 0]
    q_col = _col_form(q, DK)[:, 0:1]
    k_col = _col_form(k, DK)[:, 0:1]
    s = jnp.sum(q * k, axis=1, keepdims=True) * d
    num = w * jnp.sum(q_col * c0, axis=0, keepdims=True) + s * v
    den = w * jnp.sum(q * n0, axis=1, keepdims=True) + s
    h = num / jnp.maximum(jnp.abs(den), jnp.exp(-m_t)[:, 0:1])
    hn = h * lax.rsqrt(jnp.mean(h * h, axis=1, keepdims=True) + EPS)
    hs_ref[0] = hn * gh_ref[...] * jax.nn.sigmoid(og)
    c_ref[0, 0] = w * c0 + d * (k_col * v)
    n_ref[0, 0] = w * n0 + d * k
    m_ref[0, 0] = m_t


def _mlstm_sample(z, ig_b, lf_b, m_b, c0, n0, g_head):
    H, DK, DV = MLSTM_HEADS, MLSTM_QK_DIM, MLSTM_V_DIM
    DB = DEC_BATCH
    r0 = S0 // SUBLANES
    v0 = (2 * H * DK) // DV
    sc = pl.BlockSpec((1, 1, 1, LANES), lambda b, h: (b, h, 0, 0))
    blk = 2 * DK * DV * 4 + SUBLANES * (2 * DK + 2 * DV) * 4
    return pl.pallas_call(
        _mlstm_sample_kernel,
        grid=(DB, H),
        in_specs=[pl.BlockSpec((SUBLANES, DK), lambda b, h: (r0, h)),
                  pl.BlockSpec((SUBLANES, DK), lambda b, h: (r0, H + h)),
                  pl.BlockSpec((SUBLANES, DV), lambda b, h: (r0, v0 + h)),
                  pl.BlockSpec((SUBLANES, DV), lambda b, h: (r0, v0 + H + h)),
                  sc, sc, sc,
                  pl.BlockSpec((1, 1, DK, DV), lambda b, h: (b, h, 0, 0)),
                  pl.BlockSpec((1, 1, 1, DK), lambda b, h: (b, h, 0, 0)),
                  pl.BlockSpec((1, DV), lambda b, h: (0, h))],
        out_specs=[pl.BlockSpec((1, 1, DV), lambda b, h: (b, 0, h)),
                   pl.BlockSpec((1, 1, DK, DV), lambda b, h: (b, h, 0, 0)),
                   pl.BlockSpec((1, 1, 1, DK), lambda b, h: (b, h, 0, 0)),
                   sc],
        out_shape=[jax.ShapeDtypeStruct((DB, 1, H * DV), F32),
                   jax.ShapeDtypeStruct((DB, H, DK, DV), F32),
                   jax.ShapeDtypeStruct((DB, H, 1, DK), F32),
                   jax.ShapeDtypeStruct((DB, H, 1, LANES), F32)],
        compiler_params=_params(("parallel", "parallel"), blk),
        name="mlstm_sample",
    )(z, z, z, z, ig_b, lf_b, m_b, c0, n0, g_head.reshape(1, H * DV))


def _compress_proj_kernel(*refs, n_in, groups_per_in, lead):
    x_refs = refs[-(n_in + 3):-3]
    w_ref, p_ref, xp_scr = refs[-3], refs[-2], refs[-1]
    G, D = NSA_KV_HEADS, HEAD_DIM
    cpg = PERM_ROWS // CMP_STRIDE
    nch = n_in * groups_per_in * cpg
    i_idx = lax.broadcasted_iota(jnp.int32, (PERM_ROWS, PERM_ROWS), 0)
    j_idx = lax.broadcasted_iota(jnp.int32, (PERM_ROWS, PERM_ROWS), 1)
    src_row = (i_idx & (cpg - 1)) * CMP_STRIDE + (i_idx >> (cpg.bit_length() - 1))
    perm = (j_idx == src_row).astype(BF16)
    for i, xr in enumerate(x_refs):
        for p in range(groups_per_in):
            rows = pl.ds(p * PERM_ROWS, PERM_ROWS)
            xg = xr[0, rows, :] if lead else xr[rows, :]
            xg = _dot(perm, xg.astype(BF16))
            ch0 = (i * groups_per_in + p) * cpg
            for c in range(CMP_STRIDE):
                xp_scr[c, ch0:ch0 + cpg, :] = xg[c * cpg:(c + 1) * cpg, :]
    for s in range(2):
        acc = jnp.zeros((G * nch, CMP_R * CMP_HID), F32)
        for c in range(CMP_STRIDE):
            xs = jnp.concatenate([xp_scr[c, :, (s * G + g) * D:(s * G + g + 1) * D] for g in range(G)], axis=0)
            acc = acc + _dot(xs.astype(BF16), w_ref[s, c])
        for g in range(G):
            p_ref[0, s, g] = acc[g * nch:(g + 1) * nch]


def _compress_weights(w_cmp1):
    w = w_cmp1.reshape(2, CMP_R, CMP_STRIDE, HEAD_DIM, CMP_HID)
    return w.transpose(0, 2, 3, 1, 4).reshape(2, CMP_STRIDE, HEAD_DIM, CMP_R * CMP_HID).astype(BF16)


def _compress_proj_prompt(rows, w1r):
    G, D = NSA_KV_HEADS, HEAD_DIM
    nch = SEQ // CMP_STRIDE
    pw = CMP_R * CMP_HID
    blk = SEQ * 2 * G * D * 4 + w1r.size * 2 + 2 * G * nch * pw * 4 + SEQ * 2 * G * D * 4
    return pl.pallas_call(
        functools.partial(_compress_proj_kernel, n_in=1, groups_per_in=SEQ // PERM_ROWS, lead=False),
        grid=(BATCH,),
        in_specs=[pl.BlockSpec((SEQ, 2 * G * D), lambda b: (b, 0)),
                  pl.BlockSpec(w1r.shape, lambda b: (0, 0, 0, 0))],
        out_specs=pl.BlockSpec((1, 2, G, nch, pw), lambda b: (b, 0, 0, 0, 0)),
        out_shape=jax.ShapeDtypeStruct((BATCH, 2, G, nch, pw), F32),
        scratch_shapes=[pltpu.VMEM((CMP_STRIDE, nch, 2 * G * D), F32)],
        compiler_params=_params(("parallel",), blk),
        name="compress_proj_prompt",
    )(rows, w1r)


def _compress_proj_sample(cache2d, page_table_flat, w1r):
    G, D = NSA_KV_HEADS, HEAD_DIM
    n_pages = PAST_LEN // PAGE_SIZE
    pps = SAMPLE_PAGES_PER_STEP
    assert n_pages % pps == 0
    cpi = PAGE_SIZE // CMP_STRIDE
    nch_step = pps * cpi
    pw = CMP_R * CMP_HID

    def page_map(b, j, pt, *, i):
        return (pt[b * n_pages + j * pps + i], 0, 0)

    in_specs = [pl.BlockSpec((1, PAGE_SIZE, 2 * G * D), functools.partial(page_map, i=i)) for i in range(pps)]
    in_specs.append(pl.BlockSpec(w1r.shape, lambda b, j, pt: (0, 0, 0, 0)))
    blk = 2 * pps * PAGE_SIZE * 2 * G * D * 4 + w1r.size * 2 + 2 * G * nch_step * pw * 4
    return pl.pallas_call(
        functools.partial(_compress_proj_kernel, n_in=pps, groups_per_in=PAGE_SIZE // PERM_ROWS, lead=True),
        grid_spec=pltpu.PrefetchScalarGridSpec(
            num_scalar_prefetch=1,
            grid=(DEC_BATCH, n_pages // pps),
            in_specs=in_specs,
            out_specs=pl.BlockSpec((1, 2, G, nch_step, pw), lambda b, j, pt: (b, 0, 0, j, 0)),
            scratch_shapes=[pltpu.VMEM((CMP_STRIDE, nch_step, 2 * G * D), F32)],
        ),
        out_shape=jax.ShapeDtypeStruct((DEC_BATCH, 2, G, n_pages * cpi, pw), F32),
        compiler_params=_params(("parallel", "arbitrary"), blk),
        name="compress_proj_sample",
    )(page_table_flat, *([cache2d] * pps), w1r)


def _compress_finish_kernel(p_ref, pe_ref, w1_ref, w2_ref, o_ref):
    nch = p_ref.shape[3]
    pe = jnp.broadcast_to(pe_ref[0], (SUBLANES, pe_ref.shape[2])).astype(BF16)
    pe_term = _dot(pe, w1_ref[0].astype(BF16))[0:1]
    w2 = w2_ref[0].astype(BF16)
    for g in range(NSA_KV_HEADS):
        p = p_ref[0, 0, g]
        nxt = pltpu.roll(p[:, CMP_HID:], shift=nch - 1, axis=0)
        hidden = p[:, :CMP_HID] + nxt + pe_term
        o_ref[0, 0, g] = _dot(jax.nn.gelu(hidden).astype(BF16), w2)


def _compress_finish(p, pe_cmp, w_cmp1, w_cmp2):
    nb, _, G, nch, pw = p.shape
    pe = pe_cmp.reshape(2, 1, CMP_LEN * HEAD_DIM)
    blk = G * nch * (pw + HEAD_DIM) * 4 + CMP_LEN * HEAD_DIM * (CMP_HID + 1) * 4
    return pl.pallas_call(
        _compress_finish_kernel,
        grid=(nb, 2),
        in_specs=[pl.BlockSpec((1, 1, G, nch, pw), lambda b, s: (b, s, 0, 0, 0)),
                  pl.BlockSpec((1, 1, CMP_LEN * HEAD_DIM), lambda b, s: (s, 0, 0)),
                  pl.BlockSpec((1, CMP_LEN * HEAD_DIM, CMP_HID), lambda b, s: (s, 0, 0)),
                  pl.BlockSpec((1, CMP_HID, HEAD_DIM), lambda b, s: (s, 0, 0))],
        out_specs=pl.BlockSpec((1, 1, G, nch, HEAD_DIM), lambda b, s: (b, s, 0, 0, 0)),
        out_shape=jax.ShapeDtypeStruct((nb, 2, G, nch, HEAD_DIM), F32),
        compiler_params=_params(("parallel", "parallel"), blk),
        name="compress_finish",
    )(p, pe, w_cmp1, w_cmp2)


def _selection_overlap(n_cmp, n_sel, rows, cols):
    start = np.arange(n_cmp)[:, None] * CMP_STRIDE
    end = start + CMP_LEN - 1
    j = np.arange(n_sel)[None, :]
    ov = ((start <= (j + 1) * SEL_LEN - 1) & (end >= j * SEL_LEN)).astype(np.float32)
    out = np.zeros((rows, cols), np.float32)
    out[:n_cmp, :n_sel] = ov
    return jnp.asarray(out)


def _masked_softmax(s, valid):
    sm = jnp.where(valid, s, NEG)
    e = jnp.exp(sm - jnp.max(sm, axis=-1, keepdims=True))
    return jnp.where(valid, e / jnp.sum(e, axis=-1, keepdims=True), 0.0)


def _flash_step(carry, s, valid, vb):
    m, l, acc = carry
    sm = jnp.where(valid, s, NEG)
    m_new = jnp.maximum(m, jnp.max(sm, axis=-1, keepdims=True))
    alpha = jnp.exp(m - m_new)
    e = jnp.exp(sm - m_new)
    l = alpha * l + jnp.sum(e, axis=-1, keepdims=True)
    acc = alpha * acc + _dot(jnp.where(valid, e, 0.0).astype(BF16), vb)
    return m_new, l, acc


def _flash_one(carry, s_new, v_new):
    m, l, acc = carry
    m_new = jnp.maximum(m, s_new)
    alpha = jnp.exp(m - m_new)
    e = jnp.exp(s_new - m_new)
    return m_new, alpha * l + e, alpha * acc + e * v_new


def _flash_init(rows):
    return (jnp.full((rows, 1), -jnp.inf, F32), jnp.zeros((rows, 1), F32), jnp.zeros((rows, HEAD_DIM), F32))


def _top_rank(impm_c, impm_r, jp_idx, j_idx, axis):
    before = (impm_c > impm_r) | ((impm_c == impm_r) & (jp_idx < j_idx))
    return jnp.sum(before.astype(jnp.int32), axis=axis, keepdims=True)


FEAT_SIG = 96
FEAT_SIG_FULL = FEAT_SIG + 9
MASK_BIG = 2.0 ** 100
POS_RADIX = 256
LOG2E = 1.4426950408889634
assert SEQ // SEL_LEN <= FEAT_SIG and SEQ >= WINDOW + TQ


def _key_features(pos, block):
    n = pos.shape[0]
    kf = np.zeros((n, LANES), np.float32)
    if block is not None:
        kf[np.arange(n), block] = -MASK_BIG
    kf[:, FEAT_SIG:FEAT_SIG + 3] = (pos % POS_RADIX)[:, None]
    kf[:, FEAT_SIG + 3:FEAT_SIG + 6] = (pos // POS_RADIX)[:, None]
    kf[:, FEAT_SIG + 6:FEAT_SIG + 9] = 1.0
    return jnp.asarray(kf, BF16)


def _sigma_table():
    slopes = np.exp2(-8.0 * np.arange(1, NSA_HEADS + 1, dtype=np.float32) / NSA_HEADS).astype(np.float32)
    sigma = (slopes / np.float32(HEAD_DIM ** -0.5)).astype(np.float32)
    bf = lambda a: np.asarray(np.asarray(a, dtype=BF16), dtype=np.float32)
    s1 = bf(sigma)
    s2 = bf(sigma - s1)
    s3 = bf(sigma - s1 - s2)
    tab = np.zeros((NSA_HEADS, LANES), np.float32)
    for i, s in enumerate((s1, s2, s3)):
        tab[:, FEAT_SIG + i] = s
        tab[:, FEAT_SIG + 3 + i] = s * POS_RADIX
    tab[:, FEAT_SIG_FULL] = sigma
    return jnp.asarray(tab)


def _nsa_prompt_kernel(zero_ref, q_ref, gl_ref, sig_ref, kc_ref, vc_ref, ks_ref, vs_ref, kw_ref, vw_ref,
                       ov_ref, kfc_ref, kfs_ref, kfw_ref, out_ref,
                       qa_scr, kca_scr, vcb_scr, ksa_scr, vsb_scr, kwa_scr, vwb_scr, m_scr, acc_scr):
    del zero_ref
    qi = pl.program_id(2)
    D = HEAD_DIM
    c2 = (D ** -0.5) * LOG2E
    n_sel = SEQ // SEL_LEN
    qs = qi * TQ
    tpos = qs + lax.broadcasted_iota(jnp.int32, (TQ, 1), 0)
    tposf = tpos.astype(F32)
    lane = lax.broadcasted_iota(jnp.int32, (1, LANES), 1)
    heads = [slice(r * TQ, (r + 1) * TQ) for r in range(GRP)]

    @pl.when(qi == 0)
    def _():
        kca_scr[:, 0:D] = kc_ref[0, 0, 0].astype(BF16)
        kca_scr[:, D:2 * D] = kfc_ref[...]
        vcb_scr[...] = vc_ref[0, 0, 0].astype(BF16)
        ksa_scr[:, 0:D] = ks_ref[...].astype(BF16)
        ksa_scr[:, D:2 * D] = kfs_ref[...]
        kwa_scr[:, 0:D] = kw_ref[...].astype(BF16)
        kwa_scr[:, D:2 * D] = kfw_ref[...]
        ones_col = jnp.where(lax.broadcasted_iota(jnp.int32, (SEQ, D), 1) == 0, 1.0, 0.0).astype(BF16)
        vsb_scr[:, 0:D] = vs_ref[...].astype(BF16)
        vsb_scr[:, D:2 * D] = ones_col
        vwb_scr[:, 0:D] = vw_ref[...].astype(BF16)
        vwb_scr[:, D:2 * D] = ones_col

    def write_features(nsel):
        for r in range(GRP):
            sig_row = sig_ref[r:r + 1, :]
            c = -sig_row[:, FEAT_SIG_FULL:FEAT_SIG_FULL + 1] * tposf
            c_hi = c.astype(BF16).astype(F32)
            c_mid = (c - c_hi).astype(BF16).astype(F32)
            c_lo = c - c_hi - c_mid
            f = jnp.where(lane == FEAT_SIG + 6, c_hi,
                          jnp.where(lane == FEAT_SIG + 7, c_mid, jnp.where(lane == FEAT_SIG + 8, c_lo, sig_row)))
            if nsel is not None:
                f = jnp.where(lane < n_sel, nsel, f)
            qa_scr[heads[r], D:2 * D] = f.astype(BF16)

    for r in range(GRP):
        qa_scr[heads[r], 0:D] = q_ref[:, r * D:(r + 1) * D]
    write_features(None)

    ncp = kca_scr.shape[0]
    cpos = lax.broadcasted_iota(jnp.int32, (1, ncp), 1) * CMP_STRIDE + (CMP_LEN - 1)
    valid_c = cpos <= tpos
    s_c = _dot_nt(qa_scr[...], kca_scr[...])
    p_sum = jnp.zeros((TQ, ncp), F32)
    p_c = []
    for r in range(GRP):
        sm = jnp.where(valid_c, s_c[heads[r]], NEG)
        e = jnp.exp2((sm - jnp.max(sm, axis=-1, keepdims=True)) * c2)
        p = jnp.where(valid_c, e / jnp.sum(e, axis=-1, keepdims=True), 0.0)
        p_c.append(p.astype(BF16))
        p_sum = p_sum + p
    o_c = _dot(jnp.concatenate(p_c, axis=0), vcb_scr[...])
    imp = _dot_exact(p_sum, ov_ref[...])

    cur = tpos // SEL_LEN
    forced = (lane == 0) | (lane == cur) | (lane == cur - 1)
    impm = jnp.where(forced, jnp.inf, jnp.where(lane <= cur, imp, -jnp.inf))
    imp_t = impm.T[0:n_sel]
    j_sub = lax.broadcasted_iota(jnp.int32, (n_sel, TQ), 0)
    rank_t = jnp.zeros((n_sel, TQ), jnp.int32)
    for jp in range(n_sel):
        cj = imp_t[jp:jp + 1, :]
        rank_t = rank_t + ((cj > imp_t) | ((cj == imp_t) & (jp < j_sub))).astype(jnp.int32)
    unsel_t = jnp.where(rank_t < min(SEL_TOP, n_sel), 0.0, 1.0)
    unsel = jnp.concatenate([unsel_t, jnp.ones((LANES - n_sel, TQ), F32)], axis=0).T
    unsel_b = jnp.where(lane <= cur, unsel, 1.0)[:, 0:n_sel].astype(BF16)
    for r in range(GRP):
        qa_scr[heads[r], D:D + n_sel] = unsel_b

    m_scr[...] = jnp.full(m_scr.shape, -jnp.inf, F32)
    acc_scr[...] = jnp.zeros(acc_scr.shape, F32)

    def sel_tile(kt, bias):
        k0 = pl.multiple_of(kt * TKS, TKS)
        s_all = _dot_nt(qa_scr[...], ksa_scr[pl.ds(k0, TKS), :])
        p_all = []
        for r in range(GRP):
            s = s_all[heads[r]]
            if bias is not None:
                s = s + bias
            m_old = m_scr[heads[r]]
            m_new = jnp.maximum(m_old, jnp.max(s, axis=-1, keepdims=True))
            alpha = jnp.exp2((m_old - m_new) * c2)
            e = jnp.exp2((s - m_new) * c2)
            m_scr[heads[r]] = m_new
            acc_scr[heads[r]] = alpha * acc_scr[heads[r]]
            p_all.append(e.astype(BF16))
        acc_scr[...] += _dot(jnp.concatenate(p_all, axis=0), vsb_scr[pl.ds(k0, TKS), :])

    n_past = qs // TKS

    def past_tile(kt, carry):
        sel_tile(kt, None)
        return carry

    lax.fori_loop(0, n_past, past_tile, 0)
    spos_d = n_past * TKS + lax.broadcasted_iota(jnp.int32, (1, TKS), 1)
    sel_tile(n_past, jnp.where(spos_d <= tpos, 0.0, -MASK_BIG))
    o_s = acc_scr[:, 0:D] / acc_scr[:, D:D + 1]

    wk = WINDOW + TQ
    w0 = pl.multiple_of(jnp.maximum(qs - WINDOW, 0), TQ)
    dist_w = tpos - (w0 + lax.broadcasted_iota(jnp.int32, (1, wk), 1))
    bias_w = jnp.where((dist_w >= 0) & (dist_w <= WINDOW), 0.0, -MASK_BIG)
    s_w = _dot_nt(qa_scr[...], kwa_scr[pl.ds(w0, wk), :])
    p_w = []
    for r in range(GRP):
        s = s_w[heads[r]] + bias_w
        p_w.append(jnp.exp2((s - jnp.max(s, axis=-1, keepdims=True)) * c2).astype(BF16))
    pv_w = _dot(jnp.concatenate(p_w, axis=0), vwb_scr[pl.ds(w0, wk), :])
    o_w = pv_w[:, 0:D] / pv_w[:, D:D + 1]

    for r in range(GRP):
        gates = jax.nn.sigmoid(gl_ref[0, :, 3 * r:3 * r + 3])
        out = gates[:, 0:1] * o_c[heads[r]] + gates[:, 1:2] * o_s[heads[r]] + gates[:, 2:3] * o_w[heads[r]]
        out_ref[:, r * D:(r + 1) * D] = out.astype(out_ref.dtype)


def _nsa_prompt(q, gate_logits, kvc, rows, ov):
    G, D = NSA_KV_HEADS, HEAD_DIM
    nq = SEQ // TQ
    ncp = kvc.shape[3]
    R = GRP * TQ
    spos = np.arange(SEQ)
    kf_cmp = _key_features(np.arange(ncp) * CMP_STRIDE + (CMP_LEN - 1), None)
    kf_sel = _key_features(spos, spos // SEL_LEN)
    kf_win = _key_features(spos, None)
    row = lambda b, g, qi: b * nq + qi
    kv_spec = lambda slot: pl.BlockSpec((SEQ, D), lambda b, g, qi: (b, slot * G + g))
    const = lambda a: pl.BlockSpec(a.shape, lambda b, g, qi: (0, 0))
    blk = (TQ * GRP * D * 4 + 2 * ncp * D * 4 + 4 * SEQ * D * 4 + 3 * SEQ * D * 2 + SEQ * D * 6 + R * D * 16
           + R * (TKS + WINDOW + TQ) * 6)
    return pl.pallas_call(
        _nsa_prompt_kernel,
        grid=(BATCH, G, nq),
        in_specs=[pl.BlockSpec(memory_space=pl.ANY),
                  pl.BlockSpec((TQ, GRP * D), lambda b, g, qi: (row(b, g, qi), g)),
                  pl.BlockSpec((1, TQ, 3 * GRP), lambda b, g, qi: (g, row(b, g, qi), 0)),
                  pl.BlockSpec((GRP, LANES), lambda b, g, qi: (g, 0)),
                  pl.BlockSpec((1, 1, 1, ncp, D), lambda b, g, qi: (b, 0, g, 0, 0)),
                  pl.BlockSpec((1, 1, 1, ncp, D), lambda b, g, qi: (b, 1, g, 0, 0)),
                  kv_spec(2), kv_spec(3), kv_spec(4), kv_spec(5),
                  const(ov), const(kf_cmp), const(kf_sel), const(kf_win)],
        out_specs=pl.BlockSpec((TQ, GRP * D), lambda b, g, qi: (row(b, g, qi), g)),
        out_shape=jax.ShapeDtypeStruct((M_PAD, NSA_HEADS * D), BF16),
        scratch_shapes=[pltpu.VMEM((R, 2 * D), BF16),
                        pltpu.VMEM((ncp, 2 * D), BF16), pltpu.VMEM((ncp, D), BF16),
                        pltpu.VMEM((SEQ, 2 * D), BF16), pltpu.VMEM((SEQ, 2 * D), BF16),
                        pltpu.VMEM((SEQ, 2 * D), BF16), pltpu.VMEM((SEQ, 2 * D), BF16),
                        pltpu.VMEM((R, 1), F32), pltpu.VMEM((R, 2 * D), F32)],
        compiler_params=_params(("parallel", "parallel", "arbitrary"), blk),
        input_output_aliases={0: 0},
        name="nsa_prompt",
    )(jnp.zeros((M_PAD, NSA_HEADS * D), BF16), q, gate_logits, _sigma_table(), kvc, kvc, rows, rows, rows, rows,
      ov, kf_cmp, kf_sel, kf_win)


T_SAMPLE = PAST_LEN
N_SEL_S = -(-(PAST_LEN + 1) // SEL_LEN)
NB_PAST = PAST_LEN // SEL_LEN
KK_S = min(SEL_TOP, N_SEL_S)
SEL_PAD_S = _round_up(N_SEL_S, LANES)
assert PAST_LEN % SEL_LEN == 0 and PAST_LEN % CMP_STRIDE == 0


def _slope_col(slopes_ref, g):
    r_idx = lax.broadcasted_iota(jnp.int32, (GRP, 1), 0)
    col = jnp.zeros((GRP, 1), F32)
    for r in range(GRP):
        col = jnp.where(r_idx == r, slopes_ref[g * GRP + r], col)
    return col


def _nsa_sample_cmp_win_kernel(slopes_ref, q_ref, kc_ref, vc_ref, ov_ref, kw_ref, vw_ref, kwn_ref, vwn_ref,
                               oc_ref, ow_ref, idx_ref):
    g = pl.program_id(1)
    scale = HEAD_DIM ** -0.5
    q = q_ref[0, 0]
    slope = _slope_col(slopes_ref, g)

    kcb = kc_ref[0, 0, 0].astype(BF16)
    ncp = kcb.shape[0]
    cpos = lax.broadcasted_iota(jnp.int32, (1, ncp), 1) * CMP_STRIDE + (CMP_LEN - 1)
    dist_c = T_SAMPLE - cpos
    p_c = _masked_softmax(_dot_nt(q, kcb) * scale - slope * dist_c.astype(F32), dist_c >= 0)
    oc_ref[0, 0] = _dot(p_c.astype(BF16), vc_ref[0, 0, 0].astype(BF16))
    p_sum = jnp.broadcast_to(jnp.sum(p_c, axis=0, keepdims=True), (SUBLANES, ncp))
    imp = _dot_exact(p_sum, ov_ref[...])[0:1]

    NP = SEL_PAD_S
    cur = T_SAMPLE // SEL_LEN
    j_row = lax.broadcasted_iota(jnp.int32, (1, NP), 1)
    forced = (j_row == 0) | (j_row == cur) | (j_row == cur - 1)
    impm = jnp.where(forced, jnp.inf, jnp.where(j_row <= cur, imp, -jnp.inf))
    sub = lax.broadcasted_iota(jnp.int32, (NP, NP), 0)
    lane = lax.broadcasted_iota(jnp.int32, (NP, NP), 1)
    impm_r = jnp.broadcast_to(impm, (NP, NP))
    rank = _top_rank(impm_r.T, impm_r, sub, lane, 0).astype(F32)
    rank_c = jnp.broadcast_to(rank, (NP, NP)).T
    hit = (rank_c == lane.astype(F32)) & (sub <= cur)
    idx_ref[0, 0] = jnp.sum(jnp.where(hit, sub, 0), axis=0, keepdims=True)

    kwb = kw_ref[0].astype(BF16)
    wlen = kwb.shape[0]
    wpos = T_SAMPLE - wlen + lax.broadcasted_iota(jnp.int32, (1, wlen), 1)
    dist_w = T_SAMPLE - wpos
    valid_w = (dist_w >= 0) & (dist_w <= WINDOW) & (wpos >= 0)
    carry = _flash_step(_flash_init(GRP), _dot_nt(q, kwb) * scale - slope * dist_w.astype(F32), valid_w,
                        vw_ref[0].astype(BF16))
    k_new = kwn_ref[0, 0, 0].astype(BF16).astype(F32)
    v_new = vwn_ref[0, 0, 0].astype(BF16).astype(F32)
    s_new = jnp.sum(q.astype(F32) * k_new, axis=1, keepdims=True) * scale
    _, l_w, acc_w = _flash_one(carry, s_new, v_new)
    ow_ref[0, 0] = acc_w / l_w


def _nsa_sample_cmp_win(slopes, q_s, kvc, ov, win2d, rows_s):
    G, D, DB = NSA_KV_HEADS, HEAD_DIM, DEC_BATCH
    ncp = kvc.shape[3]
    wlen = win2d.shape[1]
    o_spec = pl.BlockSpec((1, 1, GRP, D), lambda b, g, sl: (b, g, 0, 0))
    blk = 2 * ncp * D * 4 + ov.size * 4 + 2 * wlen * D * 4 + 4 * SEL_PAD_S * SEL_PAD_S * 4
    return pl.pallas_call(
        _nsa_sample_cmp_win_kernel,
        grid_spec=pltpu.PrefetchScalarGridSpec(
            num_scalar_prefetch=1,
            grid=(DB, G),
            in_specs=[pl.BlockSpec((1, 1, GRP, D), lambda b, g, sl: (b, g, 0, 0)),
                      pl.BlockSpec((1, 1, 1, ncp, D), lambda b, g, sl: (b, 0, g, 0, 0)),
                      pl.BlockSpec((1, 1, 1, ncp, D), lambda b, g, sl: (b, 1, g, 0, 0)),
                      pl.BlockSpec(ov.shape, lambda b, g, sl: (0, 0)),
                      pl.BlockSpec((1, wlen, D), lambda b, g, sl: (b, 0, g)),
                      pl.BlockSpec((1, wlen, D), lambda b, g, sl: (b, 0, G + g)),
                      pl.BlockSpec((1, 1, 1, 1, D), lambda b, g, sl: (b, 4, g, 0, 0)),
                      pl.BlockSpec((1, 1, 1, 1, D), lambda b, g, sl: (b, 5, g, 0, 0))],
            out_specs=[o_spec, o_spec, pl.BlockSpec((1, 1, 1, SEL_PAD_S), lambda b, g, sl: (b, g, 0, 0))],
        ),
        out_shape=[jax.ShapeDtypeStruct((DB, G, GRP, D), F32),
                   jax.ShapeDtypeStruct((DB, G, GRP, D), F32),
                   jax.ShapeDtypeStruct((DB, G, 1, SEL_PAD_S), jnp.int32)],
        compiler_params=_params(("parallel", "parallel"), blk),
        name="nsa_sample_cmp_win",
    )(slopes, q_s, kvc, kvc, ov, win2d, win2d, rows_s, rows_s)


def _nsa_sample_sel_kernel(pt_ref, idx_ref, slopes_ref, q_ref, *refs):
    ks_refs, vs_refs = refs[:KK_S], refs[KK_S:2 * KK_S]
    ksn_ref, vsn_ref, oc_ref, ow_ref, gl_ref, out_ref = refs[2 * KK_S:]
    b, g = pl.program_id(0), pl.program_id(1)
    scale = HEAD_DIM ** -0.5
    q = q_ref[0, 0]
    slope = _slope_col(slopes_ref, g)
    off = lax.broadcasted_iota(jnp.int32, (1, SEL_LEN), 1)

    scores, valids = [], []
    has_new = False
    for kk in range(KK_S):
        j = idx_ref[(b * NSA_KV_HEADS + g) * KK_S + kk]
        in_cache = j < NB_PAST
        has_new = jnp.logical_or(has_new, jnp.logical_not(in_cache))
        dist = T_SAMPLE - (jnp.minimum(j, NB_PAST - 1) * SEL_LEN + off)
        valid = (dist >= 0) & in_cache
        s = _dot_nt(q, ks_refs[kk][0].astype(BF16)) * scale - slope * dist.astype(F32)
        scores.append(jnp.where(valid, s, NEG))
        valids.append(valid)
    k_new = ksn_ref[0, 0, 0].astype(BF16).astype(F32)
    v_new = vsn_ref[0, 0, 0].astype(BF16).astype(F32)
    s_new = jnp.sum(q.astype(F32) * k_new, axis=1, keepdims=True) * scale
    s_new = jnp.where(has_new, s_new, NEG)

    m = s_new
    for s in scores:
        m = jnp.maximum(m, jnp.max(s, axis=-1, keepdims=True))
    e_new = jnp.exp(s_new - m)
    l = e_new
    acc = jnp.where(has_new, e_new, 0.0) * v_new
    for kk in range(KK_S):
        e = jnp.exp(scores[kk] - m)
        l = l + jnp.sum(e, axis=-1, keepdims=True)
        acc = acc + _dot(jnp.where(valids[kk], e, 0.0).astype(BF16), vs_refs[kk][0].astype(BF16))
    gates = jax.nn.sigmoid(gl_ref[0, 0])
    out_ref[0, 0] = gates[0] * oc_ref[0, 0] + gates[1] * (acc / l) + gates[2] * ow_ref[0, 0]


def _nsa_sample_sel(page_table_flat, idx_flat, slopes, q_s, cache2d, rows_s, o_c, o_w, gate_logits_s):
    G, D, DB = NSA_KV_HEADS, HEAD_DIM, DEC_BATCH
    n_pages = PAST_LEN // PAGE_SIZE
    bpp = PAGE_SIZE // SEL_LEN

    def blk_map(b, g, pt, idx, sl, *, kk, slot):
        j = jnp.minimum(idx[(b * G + g) * KK_S + kk], NB_PAST - 1)
        return (pt[b * n_pages + j // bpp], j % bpp, slot * G + g)

    o_spec = pl.BlockSpec((1, 1, GRP, D), lambda b, g, pt, idx, sl: (b, g, 0, 0))
    blk_specs = [pl.BlockSpec((1, SEL_LEN, D), functools.partial(blk_map, kk=kk, slot=slot))
                 for slot in (2, 3) for kk in range(KK_S)]
    blk = 2 * KK_S * SEL_LEN * D * 4 + 6 * GRP * D * 4
    return pl.pallas_call(
        _nsa_sample_sel_kernel,
        grid_spec=pltpu.PrefetchScalarGridSpec(
            num_scalar_prefetch=3,
            grid=(DB, G),
            in_specs=[o_spec, *blk_specs,
                      pl.BlockSpec((1, 1, 1, 1, D), lambda b, g, pt, idx, sl: (b, 2, g, 0, 0)),
                      pl.BlockSpec((1, 1, 1, 1, D), lambda b, g, pt, idx, sl: (b, 3, g, 0, 0)),
                      o_spec, o_spec,
                      pl.BlockSpec((1, 1, 3, GRP, 1), lambda b, g, pt, idx, sl: (b, g, 0, 0, 0))],
            out_specs=o_spec,
        ),
        out_shape=jax.ShapeDtypeStruct((DB, G, GRP, D), F32),
        compiler_params=_params(("parallel", "parallel"), blk),
        name="nsa_sample_sel",
    )(page_table_flat, idx_flat, slopes, q_s, *([cache2d] * (2 * KK_S)), rows_s, rows_s, o_c, o_w, gate_logits_s)


def _stack_rows(prompt_rows, sample_rows):
    pad = jnp.zeros((M_PAD - M_PROMPT - M_SAMPLE, prompt_rows.shape[1]), prompt_rows.dtype)
    return jnp.concatenate([prompt_rows, sample_rows.astype(prompt_rows.dtype), pad], axis=0)


def _tail_rows_kernel(buf_ref, s_ref, o_ref):
    del buf_ref
    ns = s_ref.shape[0]
    o_ref[0:ns, :] = s_ref[...]
    o_ref[ns:, :] = jnp.zeros((o_ref.shape[0] - ns, o_ref.shape[1]), o_ref.dtype)


def _fill_tail_rows(buf, sample_rows):
    tail = M_PAD - M_PROMPT
    ns = _round_up(M_SAMPLE, 16)
    assert M_PROMPT % tail == 0 and ns <= tail and buf.dtype == BF16
    s = jnp.pad(sample_rows.astype(BF16), ((0, ns - M_SAMPLE), (0, 0)))
    cols = buf.shape[1]
    return pl.pallas_call(
        _tail_rows_kernel,
        grid=(1,),
        in_specs=[pl.BlockSpec(memory_space=pl.ANY), pl.BlockSpec((ns, cols), lambda i: (0, 0))],
        out_specs=pl.BlockSpec((tail, cols), lambda i: (M_PROMPT // tail, 0)),
        out_shape=jax.ShapeDtypeStruct(buf.shape, buf.dtype),
        input_output_aliases={0: 0},
        compiler_params=_params(("arbitrary",), (tail + ns) * cols * 2),
        name="fill_tail_rows",
    )(buf, s)


def _ffn_block(x, g, w_gate_up, w_down, layer):
    h = _rmsnorm(x, g, BF16)
    return _matmul_fullk(_ffn_up(h, w_gate_up, layer), w_down, layer, D_MODEL, F32, residual=x, k_split=2)


def kernel(x_prompt, x_sample, cache_nsa_kv, page_table, state_win_kv, state_mlstm_C, state_mlstm_n, state_mlstm_m,
           g_mix, g_ffn, g_kv, g_final, w_in_a, b_gate_a, g_head_a, w_out_a, w_qg_b, w_out_b, w_kv,
           pe_cmp, w_cmp1, w_cmp2, w_gate_up, w_down):
    H, DK, DV = MLSTM_HEADS, MLSTM_QK_DIM, MLSTM_V_DIM
    G, D, DB = NSA_KV_HEADS, HEAD_DIM, DEC_BATCH
    hd = NSA_HEADS * D
    n_main = 2 * H * DK + 2 * H * DV
    x = _stack_rows(x_prompt.reshape(M_PROMPT, D_MODEL), x_sample.reshape(M_SAMPLE, D_MODEL))
    w_in_t = jnp.swapaxes(w_in_a, 1, 2)
    w_qg_t = jnp.swapaxes(w_qg_b, 1, 2)
    w_gate_a = jnp.swapaxes(w_in_t[:, n_main:, :], 1, 2)
    w_gate_b = jnp.swapaxes(w_qg_t[:, hd:, :], 1, 2)

    c_p, n_p, m_p, c_s, n_s, m_s = [], [], [], [], [], []
    for l in range(N_A):
        h, gates = _rmsnorm_with_gates(x, g_mix[l], w_gate_a[l], b_gate_a[l], True)
        z = _matmul_fullk_nt(h, w_in_t, l, n_main, F32)
        gates_t = gates[:M_PROMPT, :2 * H].T.reshape(2, H, 1, M_PROMPT)
        hs_p, cp, np_, mp = _mlstm_prompt(z, gates_t, g_head_a[l])
        gs = gates[S0:S0 + DB, :2 * H]
        lane_b = lambda a: jnp.broadcast_to(a[:, :, None, None], (DB, H, 1, LANES))
        hs_s, cs, ns, ms = _mlstm_sample(z, lane_b(gs[:, :H]), lane_b(gs[:, H:]), lane_b(state_mlstm_m[l]),
                                         state_mlstm_C[l], state_mlstm_n[l].reshape(DB, H, 1, DK), g_head_a[l])
        hs = _fill_tail_rows(hs_p, hs_s.reshape(DB, H * DV))
        x = _matmul_fullk(hs, w_out_a, l, D_MODEL, F32, residual=x)
        x = _ffn_block(x, g_ffn[l], w_gate_up, w_down, l)
        c_p.append(cp)
        n_p.append(np_.reshape(BATCH, H, DK))
        m_p.append(mp.reshape(BATCH, H))
        c_s.append(cs)
        n_s.append(ns.reshape(DB, H, DK))
        m_s.append(ms[:, :, 0, 0])

    rows = _matmul_fullk(_rmsnorm(x, g_kv, BF16), w_kv[None], 0, 6 * G * D, F32)
    rows_s = rows[S0:S0 + DB].reshape(DB, 6, G, 1, D)
    w1r = _compress_weights(w_cmp1)
    kvc_p = _compress_finish(_compress_proj_prompt(rows, w1r), pe_cmp, w_cmp1, w_cmp2)
    n_pool = cache_nsa_kv.shape[0]
    pt_flat = page_table.reshape(-1).astype(jnp.int32)
    cache2d = cache_nsa_kv.reshape(n_pool, PAGE_SIZE, 4 * G * D)
    kvc_s = _compress_finish(_compress_proj_sample(cache2d, pt_flat, w1r), pe_cmp, w_cmp1, w_cmp2)
    win2d = state_win_kv.reshape(DB, state_win_kv.shape[1], 2 * G * D)

    slopes = jnp.exp2(-8.0 * jnp.arange(1, NSA_HEADS + 1, dtype=F32) / NSA_HEADS)
    n_chunks_p = SEQ // CMP_STRIDE
    ov_p = _selection_overlap(n_chunks_p - CMP_R + 1, SEQ // SEL_LEN, n_chunks_p, LANES)
    n_chunks_s = (PAST_LEN + M_SAMPLE // DB) // CMP_STRIDE
    ov_s = _selection_overlap(n_chunks_s - CMP_R + 1, N_SEL_S, n_chunks_s, SEL_PAD_S)

    for l in range(N_B):
        h, gate_logits = _rmsnorm_with_gates(x, g_mix[N_A + l], w_gate_b[l], jnp.zeros((3 * NSA_HEADS,), F32),
                                             False)
        q = _matmul_fullk_nt(h, w_qg_t, l, hd, BF16)
        gl = gate_logits[:, :3 * NSA_HEADS]
        gl_p = gl.reshape(M_PAD, G, 3 * GRP).transpose(1, 0, 2)
        att_p = _nsa_prompt(q, gl_p, kvc_p, rows, ov_p)
        q_s = q[S0:S0 + DB].reshape(DB, G, GRP, D)
        gl_s = gl[S0:S0 + DB].reshape(DB, G, GRP, 3).transpose(0, 1, 3, 2)[..., None]
        o_c, o_w, idx = _nsa_sample_cmp_win(slopes, q_s, kvc_s, ov_s, win2d, rows_s)
        idx_flat = idx[:, :, 0, :KK_S].reshape(-1)
        att_s = _nsa_sample_sel(pt_flat, idx_flat, slopes, q_s, cache2d, rows_s, o_c, o_w, gl_s)
        att = _fill_tail_rows(att_p, att_s.reshape(DB, hd))
        x = _matmul_fullk(att, w_out_b, l, D_MODEL, F32, residual=x)
        x = _ffn_block(x, g_ffn[N_A + l], w_gate_up, w_down, N_A + l)

    y_p = _rmsnorm(x, g_final, F32, n_rows=M_PROMPT)
    y_s = _rmsnorm(x[S0:], g_final, F32)[:DB]
    wp = min(WINDOW, SEQ)
    rows_p = rows[:M_PROMPT].reshape(BATCH, SEQ, 6, G, D)
    rows_s6 = rows[S0:S0 + DB].reshape(DB, 1, 6, G, D)
    wbuf = state_win_kv.shape[1]
    win_kv_sample = jnp.concatenate([state_win_kv, rows_s6[:, :, 4:]], axis=1)[:, -wbuf:]
    return (y_p.reshape(BATCH, SEQ, D_MODEL), y_s.reshape(DB, 1, D_MODEL),
            rows_p[:, :, :4], rows_p[:, SEQ - wp:, 4:],
            jnp.stack(c_p), jnp.stack(n_p), jnp.stack(m_p),
            rows_s6[:, :, :4], win_kv_sample,
            jnp.stack(c_s), jnp.stack(n_s), jnp.stack(m_s))
```
